```python
import jax, jax.numpy as jnp
from jax import lax
import numpy as np

D_MODEL = 1024
BATCH = 4
SEQ = 4096
DEPTH = 1

MEM_LEN = 256
EPS = 1e-6
CONV_WIDTH = 3
A_WIDTH = D_MODEL // 2
HEAD_DIM = 64
ATTN_WIDTH = D_MODEL // 2
N_Q_HEADS = ATTN_WIDTH // HEAD_DIM
N_KV_HEADS = N_Q_HEADS // 4
KV_WIDTH = N_KV_HEADS * HEAD_DIM
WINDOW = 128
BLOCK = 128
ROPE_THETA = 500000.0
ROT_DIM = HEAD_DIM // 4
MEM_HEADS = 4
MEM_HEAD_DIM = D_MODEL // 8
MEM_WIDTH = MEM_HEADS * MEM_HEAD_DIM
N_BRANCHES = 3
IN_SIZES = (A_WIDTH, A_WIDTH, A_WIDTH, A_WIDTH,
            ATTN_WIDTH, KV_WIDTH, KV_WIDTH, ATTN_WIDTH,
            MEM_WIDTH, MEM_WIDTH,
            N_BRANCHES * D_MODEL)
IN_WIDTH = sum(IN_SIZES)

kernel_name = "hybrid_gated_conv_swa_memxattn_block"


def rms_norm(x, g):
    xf = x.astype(jnp.float32)
    y = xf * lax.rsqrt(jnp.mean(xf * xf, axis=-1, keepdims=True) + EPS)
    return (y * g.astype(jnp.float32)).astype(x.dtype)


def partial_rope(t, pos):
    half = ROT_DIM // 2
    inv_freq = jnp.power(jnp.float32(ROPE_THETA), -jnp.arange(half, dtype=jnp.float32) * (2.0 / ROT_DIM))
    ang = pos.astype(jnp.float32)[:, None] * inv_freq[None, :]
    cos = jnp.cos(ang)[None, :, None, :]
    sin = jnp.sin(ang)[None, :, None, :]
    tr = t[..., :ROT_DIM].astype(jnp.float32)
    t1, t2 = tr[..., :half], tr[..., half:]
    rot = jnp.concatenate([t1 * cos - t2 * sin, t2 * cos + t1 * sin], axis=-1).astype(t.dtype)
    return jnp.concatenate([rot, t[..., ROT_DIM:]], axis=-1)


def short_gated_conv(b, c, u, w_conv):
    s = u.shape[1]
    cu = jnp.pad(c * u, ((0, 0), (1, 1), (0, 0)))
    y = cu[:, :s] * w_conv[0] + cu[:, 1:s + 1] * w_conv[1] + cu[:, 2:] * w_conv[2]
    return b * y


def window_attention_with_sink(q, k, v, sink):
    bsz, s, hq, dh = q.shape
    hkv = k.shape[2]
    grp = hq // hkv
    nb = s // BLOCK
    qb = q.reshape(bsz, nb, BLOCK, hkv, grp, dh)
    pad = ((0, 0), (BLOCK, BLOCK), (0, 0), (0, 0))
    kp = jnp.pad(k, pad).reshape(bsz, nb + 2, BLOCK, hkv, dh)
    vp = jnp.pad(v, pad).reshape(bsz, nb + 2, BLOCK, hkv, dh)
    kw = jnp.concatenate([kp[:, :nb], kp[:, 1:nb + 1], kp[:, 2:]], axis=2)
    vw = jnp.concatenate([vp[:, :nb], vp[:, 1:nb + 1], vp[:, 2:]], axis=2)
    qpos = jnp.arange(nb)[:, None] * BLOCK + jnp.arange(BLOCK)[None, :]
    kpos = (jnp.arange(nb)[:, None] - 1) * BLOCK + jnp.arange(3 * BLOCK)[None, :]
    valid = ((jnp.abs(qpos[:, :, None] - kpos[:, None, :]) <= WINDOW)
             & (kpos >= 0)[:, None, :] & (kpos < s)[:, None, :])
    scores = jnp.einsum('bnqhgd,bnkhd->bnhgqk', qb, kw,
                        preferred_element_type=jnp.float32) * (dh ** -0.5)
    scores = jnp.where(valid[None, :, None, None], scores, -jnp.inf)
    sink_l = sink.astype(jnp.float32).reshape(hkv, grp)[None, None, :, :, None, None]
    m = jnp.maximum(jnp.max(scores, axis=-1, keepdims=True), sink_l)
    p = jnp.exp(scores - m)
    p = p / (jnp.sum(p, axis=-1, keepdims=True) + jnp.exp(sink_l - m))
    out = jnp.einsum('bnhgqk,bnkhd->bnqhgd', p.astype(v.dtype), vw)
    return out.reshape(bsz, s, hq * dh)


def memory_cross_attention(q, mk, mv):
    dm = q.shape[-1]
    scores = jnp.einsum('bshd,bmhd->bhsm', q, mk,
                        preferred_element_type=jnp.float32) * (dm ** -0.5)
    p = jax.nn.softmax(scores, axis=-1)
    out = jnp.einsum('bhsm,bmhd->bshd', p.astype(mv.dtype), mv)
    return out.reshape(q.shape[0], q.shape[1], -1)


def hybrid_layer(x, mem, g_pre, w_in, w_conv, attn_sink, g_mem, w_mem_kv,
                 w_up_a, w_up_b, w_up_m, w_out, g_post):
    bsz, s, d = x.shape
    h = rms_norm(x, g_pre)
    proj = h @ w_in
    idx = list(np.cumsum(IN_SIZES)[:-1])
    (a_b, a_c, a_x, a_z, b_q, b_k, b_v, b_z, m_q, m_z, gate_logits) = jnp.split(proj, idx, axis=-1)

    ya = short_gated_conv(a_b, a_c, a_x, w_conv) * jax.nn.silu(a_z)
    ua = ya @ w_up_a

    pos = jnp.arange(s)
    q = partial_rope(b_q.reshape(bsz, s, N_Q_HEADS, HEAD_DIM), pos)
    k = partial_rope(b_k.reshape(bsz, s, N_KV_HEADS, HEAD_DIM), pos)
    v = b_v.reshape(bsz, s, N_KV_HEADS, HEAD_DIM)
    yb = window_attention_with_sink(q, k, v, attn_sink) * jax.nn.silu(b_z)
    ub = yb @ w_up_b

    mn = rms_norm(mem, g_mem)
    mkv = mn @ w_mem_kv
    mk, mv = jnp.split(mkv, 2, axis=-1)
    mlen = mem.shape[1]
    ym = memory_cross_attention(m_q.reshape(bsz, s, MEM_HEADS, MEM_HEAD_DIM),
                                mk.reshape(bsz, mlen, MEM_HEADS, MEM_HEAD_DIM),
                                mv.reshape(bsz, mlen, MEM_HEADS, MEM_HEAD_DIM)) * jax.nn.silu(m_z)
    um = ym @ w_up_m

    gates = jax.nn.sigmoid(gate_logits.astype(jnp.float32)).astype(x.dtype).reshape(bsz, s, N_BRANCHES, d)
    merged = gates[:, :, 0] * ua + gates[:, :, 1] * ub + gates[:, :, 2] * um
    out = merged @ w_out
    return x + rms_norm(out, g_post)


def setup_inputs(seed: int = 0) -> dict:
    key = jax.random.key(seed)
    ks = jax.random.split(key, 14)
    f32 = jnp.float32
    nrm = lambda k, shape, fan_in: jax.random.normal(k, shape, f32) * (fan_in ** -0.5)
    gain = lambda k, shape: 1.0 + 0.05 * jax.random.normal(k, shape, f32)
    return {
        "x": jax.random.normal(ks[0], (BATCH, SEQ, D_MODEL), f32),
        "mem": jax.random.normal(ks[1], (BATCH, MEM_LEN, D_MODEL), f32),
        "g_pre": gain(ks[2], (DEPTH, D_MODEL)),
        "w_in": nrm(ks[3], (DEPTH, D_MODEL, IN_WIDTH), D_MODEL),
        "w_conv": nrm(ks[4], (DEPTH, CONV_WIDTH, A_WIDTH), CONV_WIDTH),
        "attn_sink": 0.5 * jax.random.normal(ks[5], (DEPTH, N_Q_HEADS), f32),
        "g_mem": gain(ks[6], (DEPTH, D_MODEL)),
        "w_mem_kv": nrm(ks[7], (DEPTH, D_MODEL, 2 * MEM_WIDTH), D_MODEL),
        "w_up_a": nrm(ks[8], (DEPTH, A_WIDTH, D_MODEL), A_WIDTH),
        "w_up_b": nrm(ks[9], (DEPTH, ATTN_WIDTH, D_MODEL), ATTN_WIDTH),
        "w_up_m": nrm(ks[10], (DEPTH, MEM_WIDTH, D_MODEL), MEM_WIDTH),
        "w_out": nrm(ks[11], (DEPTH, D_MODEL, D_MODEL), D_MODEL),
        "g_post": gain(ks[12], (DEPTH, D_MODEL)),
    }


def reference(x, mem, g_pre, w_in, w_conv, attn_sink, g_mem, w_mem_kv,
              w_up_a, w_up_b, w_up_m, w_out, g_post):
    for l in range(DEPTH):
        x = hybrid_layer(x, mem, g_pre[l], w_in[l], w_conv[l], attn_sink[l], g_mem[l],
                         w_mem_kv[l], w_up_a[l], w_up_b[l], w_up_m[l], w_out[l], g_post[l])
    return x
```

```python
import functools

import numpy as np
import jax
import jax.numpy as jnp
from jax import lax
from jax.experimental import pallas as pl
from jax.experimental.pallas import tpu as pltpu

F32 = jnp.float32
BF16 = jnp.bfloat16

D_MODEL = 1024
SEQ = 4096
MEM_LEN = 256
EPS = 1e-6
CONV_WIDTH = 3
A_WIDTH = 512
HEAD_DIM = 64
ATTN_WIDTH = 512
N_Q_HEADS = 8
KV_WIDTH = 128
WINDOW = 128
ROPE_THETA = 500000.0
ROT_DIM = 16
MEM_HEADS = 4
MEM_HEAD_DIM = 128
MEM_WIDTH = 512

C_AB, C_AC, C_AX, C_AZ = 0, 512, 1024, 1536
C_BQ, C_BK, C_BV, C_BZ = 2048, 2560, 2688, 2816
C_MQ, C_MZ = 3328, 3840
C_G0, C_G1, C_G2 = 4352, 5376, 6400
IN_WIDTH = 7424

LANES = 128
BF16_ROWS = 16
VMEM_LIMIT_BYTES = 56 * 1024 * 1024

TS = 256
QB = WINDOW
HALO = WINDOW
CONV_HALO = BF16_ROWS


def _sigmoid(v):
    return 1.0 / (1.0 + jnp.exp(-v))


def _silu(v):
    return v * _sigmoid(v)


def _rms(v, g):
    ms = jnp.mean(v * v, axis=-1, keepdims=True)
    return v * lax.rsqrt(ms + EPS) * g


def _dot(a, b):
    return jnp.dot(a, b, preferred_element_type=F32)


def _dot_nt(a, b):
    return lax.dot_general(a, b, (((1,), (1,)), ((), ())), preferred_element_type=F32)


def _rope(t, cos, ssin, low8):
    partner = jnp.where(low8, pltpu.roll(t, LANES - ROT_DIM // 2, 1), pltpu.roll(t, ROT_DIM // 2, 1))
    return t * cos + partner * ssin


def _rope_tables():
    half = ROT_DIM // 2
    inv_freq = np.power(np.float32(ROPE_THETA), -np.arange(half, dtype=np.float32) * np.float32(2.0 / ROT_DIM))
    pos = (np.arange(SEQ + 2 * HALO) - HALO).astype(np.float32)
    ang = (pos[:, None] * inv_freq[None, :]).astype(np.float32)
    cos, sin = np.cos(ang).astype(np.float32), np.sin(ang).astype(np.float32)
    ct = np.ones((SEQ + 2 * HALO, HEAD_DIM), np.float32)
    st = np.zeros((SEQ + 2 * HALO, HEAD_DIM), np.float32)
    ct[:, :half], ct[:, half:ROT_DIM] = cos, cos
    st[:, :half], st[:, half:ROT_DIM] = -sin, sin
    return np.tile(ct, (1, LANES // HEAD_DIM)), np.tile(st, (1, LANES // HEAD_DIM))


def _mem_kv_kernel(mem_ref, g_ref, w_ref, mkt_ref, mv_ref):
    mn = _rms(mem_ref[0], g_ref[...]).astype(BF16)
    kv = _dot(mn, w_ref[...])
    mkt_ref[0] = kv[:, :MEM_WIDTH].T.astype(BF16)
    mv_ref[0] = kv[:, MEM_WIDTH:].astype(BF16)


def _window_attention(q_lhs, k_win, v_win, valid, sink_col):
    s = _dot_nt(q_lhs, k_win)
    s = jnp.where(valid, s, -jnp.inf)
    m = jnp.maximum(jnp.max(s, axis=1, keepdims=True), sink_col)
    p = jnp.exp(s - m)
    denom = jnp.sum(p, axis=1, keepdims=True) + jnp.exp(sink_col - m)
    return _dot(p.astype(BF16), v_win) / denom


def _layer_kernel(sink_ref, x_ref, xp_ref, xn_ref, gpre_ref, win_ref, wconv_ref, cos_ref, sin_ref,
                  mkt_ref, mv_ref, wua_ref, wub_ref, wum_ref, wout_ref, gpost_ref,
                  out_ref, h_s, k_s, ksw_s, v_s, vsw_s, cu_s, mrg_s):
    i = pl.program_id(1)
    row0 = pl.multiple_of(i * TS, TS)
    ext = TS + 2 * HALO

    g_pre = gpre_ref[...]
    h_s[0:HALO] = _rms(xp_ref[0], g_pre).astype(BF16)
    h_s[HALO:HALO + TS] = _rms(x_ref[0], g_pre).astype(BF16)
    h_s[HALO + TS:ext] = _rms(xn_ref[0], g_pre).astype(BF16)

    lane = lax.broadcasted_iota(jnp.int32, (1, LANES), 1)
    low_head = lane < HEAD_DIM
    low8 = (lane % HEAD_DIM) < (ROT_DIM // 2)

    kv_chunk = 256
    for c in range(ext // kv_chunk):
        r0 = c * kv_chunk
        kv = _dot(h_s[r0:r0 + kv_chunk], win_ref[:, C_BK:C_BV + KV_WIDTH])
        cos = cos_ref[pl.ds(row0 + r0, kv_chunk), :]
        ssin = sin_ref[pl.ds(row0 + r0, kv_chunk), :]
        k = _rope(kv[:, :KV_WIDTH], cos, ssin, low8)
        v = kv[:, KV_WIDTH:]
        k_s[r0:r0 + kv_chunk] = k.astype(BF16)
        ksw_s[r0:r0 + kv_chunk] = pltpu.roll(k, HEAD_DIM, 1).astype(BF16)
        v_s[r0:r0 + kv_chunk] = v.astype(BF16)
        vsw_s[r0:r0 + kv_chunk] = pltpu.roll(v, HEAD_DIM, 1).astype(BF16)

    h = h_s[HALO:HALO + TS]

    scale = HEAD_DIM ** -0.5
    qcos = cos_ref[pl.ds(row0 + HALO, TS), :] * scale
    qsin = sin_ref[pl.ds(row0 + HALO, TS), :] * scale
    q = _dot(h, win_ref[:, C_BQ:C_BQ + ATTN_WIDTH])
    q_lo, q_hi = [], []
    for mblk in range(ATTN_WIDTH // LANES):
        qr = _rope(q[:, mblk * LANES:(mblk + 1) * LANES], qcos, qsin, low8)
        q_lo.append(jnp.where(low_head, qr, 0.0).astype(BF16))
        q_hi.append(jnp.where(low_head, 0.0, qr).astype(BF16))

    rows4 = lax.broadcasted_iota(jnp.int32, (4 * QB, 1), 0)
    def sink_rows(h0, h1, h2, h3):
        return jnp.where(rows4 < QB, sink_ref[h0],
                         jnp.where(rows4 < 2 * QB, sink_ref[h1],
                                   jnp.where(rows4 < 3 * QB, sink_ref[h2], sink_ref[h3])))
    sink_a = sink_rows(0, 2, 5, 7)
    sink_b = sink_rows(1, 3, 4, 6)

    qrow = lax.broadcasted_iota(jnp.int32, (4 * QB, 3 * QB), 0) % QB
    kcol = lax.broadcasted_iota(jnp.int32, (4 * QB, 3 * QB), 1)
    band = (kcol >= qrow) & (kcol <= qrow + 2 * WINDOW)

    yb_blocks = []
    for j in range(TS // QB):
        qs = slice(j * QB, (j + 1) * QB)
        kpos = kcol + (row0 + j * QB - HALO)
        valid = band & (kpos >= 0) & (kpos < SEQ)
        lhs_a = jnp.concatenate([q_lo[0][qs], q_lo[1][qs], q_hi[2][qs], q_hi[3][qs]], axis=0)
        lhs_b = jnp.concatenate([q_hi[0][qs], q_hi[1][qs], q_lo[2][qs], q_lo[3][qs]], axis=0)
        ws = slice(j * QB, j * QB + 3 * QB)
        o_a = _window_attention(lhs_a, k_s[ws], v_s[ws], valid, sink_a)
        o_b = _window_attention(lhs_b, ksw_s[ws], vsw_s[ws], valid, sink_b)
        cols = [jnp.where(low_head, o_a[0:QB], o_b[0:QB]),
                jnp.where(low_head, o_a[QB:2 * QB], o_b[QB:2 * QB]),
                jnp.where(low_head, o_b[2 * QB:3 * QB], o_a[2 * QB:3 * QB]),
                jnp.where(low_head, o_b[3 * QB:4 * QB], o_a[3 * QB:4 * QB])]
        yb_blocks.append(jnp.concatenate(cols, axis=1))
    yb = jnp.concatenate(yb_blocks, axis=0)
    yb = yb * _silu(_dot(h, win_ref[:, C_BZ:C_BZ + ATTN_WIDTH]))
    ub = _dot(yb.astype(BF16), wub_ref[...])
    mrg_s[...] = _sigmoid(_dot(h, win_ref[:, C_G1:C_G1 + D_MODEL])) * ub

    hx = h_s[HALO - CONV_HALO:HALO + TS + CONV_HALO]
    cx = _dot(hx, win_ref[:, C_AC:C_AX + A_WIDTH])
    cu_s[...] = cx[:, :A_WIDTH] * cx[:, A_WIDTH:]
    pos = row0 + lax.broadcasted_iota(jnp.int32, (TS, 1), 0)
    cu_prev = jnp.where(pos == 0, 0.0, cu_s[CONV_HALO - 1:CONV_HALO - 1 + TS])
    cu_next = jnp.where(pos == SEQ - 1, 0.0, cu_s[CONV_HALO + 1:CONV_HALO + 1 + TS])
    y = cu_prev * wconv_ref[0:1] + cu_s[CONV_HALO:CONV_HALO + TS] * wconv_ref[1:2] + cu_next * wconv_ref[2:3]
    ya = _dot(h, win_ref[:, C_AB:C_AB + A_WIDTH]) * y * _silu(_dot(h, win_ref[:, C_AZ:C_AZ + A_WIDTH]))
    ua = _dot(ya.astype(BF16), wua_ref[...])
    mrg_s[...] += _sigmoid(_dot(h, win_ref[:, C_G0:C_G0 + D_MODEL])) * ua

    mq = _dot(h, win_ref[:, C_MQ:C_MQ + MEM_WIDTH]) * (MEM_HEAD_DIM ** -0.5)
    ym_heads = []
    for hh in range(MEM_HEADS):
        hs = slice(hh * MEM_HEAD_DIM, (hh + 1) * MEM_HEAD_DIM)
        s = _dot(mq[:, hs].astype(BF16), mkt_ref[0, hs, :])
        p = jnp.exp(s - jnp.max(s, axis=1, keepdims=True))
        o = _dot(p.astype(BF16), mv_ref[0, :, hs]) / jnp.sum(p, axis=1, keepdims=True)
        ym_heads.append(o)
    ym = jnp.concatenate(ym_heads, axis=1) * _silu(_dot(h, win_ref[:, C_MZ:C_MZ + MEM_WIDTH]))
    um = _dot(ym.astype(BF16), wum_ref[...])
    merged = mrg_s[...] + _sigmoid(_dot(h, win_ref[:, C_G2:C_G2 + D_MODEL])) * um

    o = _dot(merged.astype(BF16), wout_ref[...])
    out_ref[0] = x_ref[0] + _rms(o, gpost_ref[...])


def _resident(shape):
    return pl.BlockSpec(shape, lambda b, i: (0,) * len(shape), pipeline_mode=pl.Buffered(1))


def _mem_kv(mem, g_mem, w_mem_kv):
    bsz = mem.shape[0]
    return pl.pallas_call(
        _mem_kv_kernel,
        grid=(bsz,),
        in_specs=[pl.BlockSpec((1, MEM_LEN, D_MODEL), lambda b: (b, 0, 0)),
                  pl.BlockSpec((1, D_MODEL), lambda b: (0, 0)),
                  pl.BlockSpec((D_MODEL, 2 * MEM_WIDTH), lambda b: (0, 0))],
        out_specs=[pl.BlockSpec((1, MEM_WIDTH, MEM_LEN), lambda b: (b, 0, 0)),
                   pl.BlockSpec((1, MEM_LEN, MEM_WIDTH), lambda b: (b, 0, 0))],
        out_shape=[jax.ShapeDtypeStruct((bsz, MEM_WIDTH, MEM_LEN), BF16),
                   jax.ShapeDtypeStruct((bsz, MEM_LEN, MEM_WIDTH), BF16)],
        compiler_params=pltpu.CompilerParams(dimension_semantics=("arbitrary",)),
        name="mem_kv",
    )(mem, g_mem.reshape(1, D_MODEL), w_mem_kv.astype(BF16))


def _layer(x, mkt, mv, g_pre, w_in, w_conv, attn_sink, w_up_a, w_up_b, w_up_m, w_out, g_post, cos_t, sin_t):
    bsz, s, d = x.shape
    assert (s, d) == (SEQ, D_MODEL) and s % TS == 0 and TS % QB == 0 and (TS + 2 * HALO) % 256 == 0
    nb = TS // HALO
    ext = TS + 2 * HALO
    in_specs = [
        pl.BlockSpec(memory_space=pltpu.SMEM),
        pl.BlockSpec((1, TS, d), lambda b, i: (b, i, 0)),
        pl.BlockSpec((1, HALO, d), lambda b, i: (b, jnp.maximum(i * nb - 1, 0), 0)),
        pl.BlockSpec((1, HALO, d), lambda b, i: (b, jnp.minimum((i + 1) * nb, s // HALO - 1), 0)),
        _resident((1, d)),
        _resident((d, IN_WIDTH)),
        _resident((CONV_WIDTH, A_WIDTH)),
        _resident((s + 2 * HALO, LANES)),
        _resident((s + 2 * HALO, LANES)),
        pl.BlockSpec((1, MEM_WIDTH, MEM_LEN), lambda b, i: (b, 0, 0)),
        pl.BlockSpec((1, MEM_LEN, MEM_WIDTH), lambda b, i: (b, 0, 0)),
        _resident((A_WIDTH, d)), _resident((ATTN_WIDTH, d)), _resident((MEM_WIDTH, d)),
        _resident((d, d)),
        _resident((1, d)),
    ]
    scratch = [
        pltpu.VMEM((ext, d), BF16),
        pltpu.VMEM((ext, KV_WIDTH), BF16),
        pltpu.VMEM((ext, KV_WIDTH), BF16),
        pltpu.VMEM((ext, KV_WIDTH), BF16),
        pltpu.VMEM((ext, KV_WIDTH), BF16),
        pltpu.VMEM((TS + 2 * CONV_HALO, A_WIDTH), F32),
        pltpu.VMEM((TS, d), F32),
    ]
    return pl.pallas_call(
        _layer_kernel,
        grid=(bsz, s // TS),
        in_specs=in_specs,
        out_specs=pl.BlockSpec((1, TS, d), lambda b, i: (b, i, 0)),
        out_shape=jax.ShapeDtypeStruct(x.shape, x.dtype),
        scratch_shapes=scratch,
        compiler_params=pltpu.CompilerParams(dimension_semantics=("arbitrary", "arbitrary"),
                                             vmem_limit_bytes=VMEM_LIMIT_BYTES),
        name="hybrid_layer",
    )(attn_sink, x, x, x, g_pre.reshape(1, d), w_in.astype(BF16), w_conv, cos_t, sin_t, mkt, mv,
      w_up_a.astype(BF16), w_up_b.astype(BF16), w_up_m.astype(BF16), w_out.astype(BF16), g_post.reshape(1, d))


def kernel(x, mem, g_pre, w_in, w_conv, attn_sink, g_mem, w_mem_kv, w_up_a, w_up_b, w_up_m, w_out, g_post):
    cos_np, sin_np = _rope_tables()
    cos_t, sin_t = jnp.asarray(cos_np), jnp.asarray(sin_np)
    for l in range(g_pre.shape[0]):
        mkt, mv = _mem_kv(mem, g_mem[l], w_mem_kv[l])
        x = _layer(x, mkt, mv, g_pre[l], w_in[l], w_conv[l], attn_sink[l], w_up_a[l], w_up_b[l], w_up_m[l],
                   w_out[l], g_post[l], cos_t, sin_t)
    return x
```

```python
import math

import numpy as np
import jax
import jax.numpy as jnp
from jax import lax
from jax.experimental import pallas as pl
from jax.experimental.pallas import tpu as pltpu

F32 = jnp.float32
BF16 = jnp.bfloat16

D_MODEL = 1024
SEQ = 4096
MEM_LEN = 256
EPS = 1e-6
CONV_WIDTH = 3
A_WIDTH = 512
HEAD_DIM = 64
ATTN_WIDTH = 512
N_Q_HEADS = 8
KV_WIDTH = 128
WINDOW = 128
ROPE_THETA = 500000.0
ROT_DIM = 16
MEM_HEADS = 4
MEM_HEAD_DIM = 128
MEM_WIDTH = 512

C_AB, C_AC, C_AX, C_AZ = 0, 512, 1024, 1536
C_BQ, C_BK, C_BV, C_BZ = 2048, 2560, 2688, 2816
C_MQ, C_MZ = 3328, 3840
C_G0, C_G1, C_G2 = 4352, 5376, 6400
IN_WIDTH = 7424

LANES = 128
BF16_ROWS = 16
VMEM_LIMIT_BYTES = 56 * 1024 * 1024

TS = 256
QB = WINDOW
HALO = WINDOW
CONV_HALO = BF16_ROWS

LOG2E = math.log2(math.e)


def _sigmoid(v):
    return 1.0 / (1.0 + jnp.exp(-v))


def _silu(v):
    return v * _sigmoid(v)


def _rms(v, g):
    ms = jnp.mean(v * v, axis=-1, keepdims=True)
    return v * lax.rsqrt(ms + EPS) * g


def _dot(a, b):
    return jnp.dot(a, b, preferred_element_type=F32)


def _dot_nt(a, b):
    return lax.dot_general(a, b, (((1,), (1,)), ((), ())), preferred_element_type=F32)


def _rope(t, cos, ssin, low8):
    partner = jnp.where(low8, pltpu.roll(t, LANES - ROT_DIM // 2, 1), pltpu.roll(t, ROT_DIM // 2, 1))
    return t * cos + partner * ssin


def _rope_tables():
    half = ROT_DIM // 2
    inv_freq = np.power(np.float32(ROPE_THETA), -np.arange(half, dtype=np.float32) * np.float32(2.0 / ROT_DIM))
    pos = (np.arange(SEQ + 2 * HALO) - HALO).astype(np.float32)
    ang = (pos[:, None] * inv_freq[None, :]).astype(np.float32)
    cos, sin = np.cos(ang).astype(np.float32), np.sin(ang).astype(np.float32)
    ct = np.ones((SEQ + 2 * HALO, HEAD_DIM), np.float32)
    st = np.zeros((SEQ + 2 * HALO, HEAD_DIM), np.float32)
    ct[:, :half], ct[:, half:ROT_DIM] = cos, cos
    st[:, :half], st[:, half:ROT_DIM] = -sin, sin
    return np.tile(ct, (1, LANES // HEAD_DIM)), np.tile(st, (1, LANES // HEAD_DIM))


def _band_bias():
    r = np.arange(QB)[:, None]
    c = np.arange(3 * QB)[None, :]
    band = (c >= r) & (c <= r + 2 * WINDOW)
    masks = [band, band & (c >= QB), band & (c < 2 * QB)]
    return np.stack([np.where(m, 0.0, -np.inf) for m in masks]).astype(np.float32)


def _mem_kv_kernel(mem_ref, g_ref, w_ref, mkt_ref, mvx_ref):
    mn = _rms(mem_ref[0], g_ref[...]).astype(BF16)
    kv = _dot(mn, w_ref[...])
    mkt_ref[0] = kv[:, :MEM_WIDTH].T.astype(BF16)
    ones = jnp.ones((MEM_LEN, MEM_HEAD_DIM), F32)
    for hh in range(MEM_HEADS):
        c0 = MEM_WIDTH + hh * MEM_HEAD_DIM
        mvx_ref[0, hh] = jnp.concatenate([kv[:, c0:c0 + MEM_HEAD_DIM], ones], axis=1).astype(BF16)


def _window_attention(q_lhs, k_win, v_win, bias, sink_col):
    s = _dot_nt(q_lhs, k_win) + bias
    m = jnp.maximum(jnp.max(s, axis=1, keepdims=True), sink_col)
    p = jnp.exp2(s - m)
    ov = _dot(p.astype(BF16), v_win)
    return ov[:, :LANES] / (ov[:, LANES:] + jnp.exp2(sink_col - m))


def _layer_kernel(sink_ref, x_ref, xp_ref, xn_ref, gpre_ref, win_ref, wconv_ref, cos_ref, sin_ref, bias_ref,
                  mkt_ref, mvx_ref, wua_ref, wub_ref, wum_ref, wout_ref, gpost_ref,
                  out_ref, h_s, k_s, ksw_s, v_s, vsw_s, cu_s, mrg_s):
    i = pl.program_id(1)
    row0 = pl.multiple_of(i * TS, TS)
    ext = TS + 2 * HALO

    g_pre = gpre_ref[...]
    h_s[0:HALO] = _rms(xp_ref[0], g_pre).astype(BF16)
    h_s[HALO:HALO + TS] = _rms(x_ref[0], g_pre).astype(BF16)
    h_s[HALO + TS:ext] = _rms(xn_ref[0], g_pre).astype(BF16)

    lane = lax.broadcasted_iota(jnp.int32, (1, LANES), 1)
    low_head = lane < HEAD_DIM
    low8 = (lane % HEAD_DIM) < (ROT_DIM // 2)

    kv_chunk = 256
    ones = jnp.ones((kv_chunk, LANES), BF16)
    for c in range(ext // kv_chunk):
        r0 = c * kv_chunk
        kv = _dot(h_s[r0:r0 + kv_chunk], win_ref[:, C_BK:C_BV + KV_WIDTH])
        cos = cos_ref[pl.ds(row0 + r0, kv_chunk), :]
        ssin = sin_ref[pl.ds(row0 + r0, kv_chunk), :]
        k = _rope(kv[:, :KV_WIDTH], cos, ssin, low8)
        v = kv[:, KV_WIDTH:]
        k_s[r0:r0 + kv_chunk] = k.astype(BF16)
        ksw_s[r0:r0 + kv_chunk] = pltpu.roll(k, HEAD_DIM, 1).astype(BF16)
        v_s[r0:r0 + kv_chunk, 0:LANES] = v.astype(BF16)
        v_s[r0:r0 + kv_chunk, LANES:2 * LANES] = ones
        vsw_s[r0:r0 + kv_chunk, 0:LANES] = pltpu.roll(v, HEAD_DIM, 1).astype(BF16)
        vsw_s[r0:r0 + kv_chunk, LANES:2 * LANES] = ones

    h = h_s[HALO:HALO + TS]

    scale = (HEAD_DIM ** -0.5) * LOG2E
    qcos = cos_ref[pl.ds(row0 + HALO, TS), :] * scale
    qsin = sin_ref[pl.ds(row0 + HALO, TS), :] * scale
    q = _dot(h, win_ref[:, C_BQ:C_BQ + ATTN_WIDTH])
    q_lo, q_hi = [], []
    for mblk in range(ATTN_WIDTH // LANES):
        qr = _rope(q[:, mblk * LANES:(mblk + 1) * LANES], qcos, qsin, low8)
        q_lo.append(jnp.where(low_head, qr, 0.0).astype(BF16))
        q_hi.append(jnp.where(low_head, 0.0, qr).astype(BF16))

    rows4 = lax.broadcasted_iota(jnp.int32, (4 * QB, 1), 0)
    def sink_rows(h0, h1, h2, h3):
        return LOG2E * jnp.where(rows4 < QB, sink_ref[h0],
                                 jnp.where(rows4 < 2 * QB, sink_ref[h1],
                                           jnp.where(rows4 < 3 * QB, sink_ref[h2], sink_ref[h3])))
    sink_a = sink_rows(0, 2, 5, 7)
    sink_b = sink_rows(1, 3, 4, 6)

    n_blocks = SEQ // QB
    yb_blocks = []
    for j in range(TS // QB):
        qs = slice(j * QB, (j + 1) * QB)
        blk = i * (TS // QB) + j
        variant = jnp.where(blk == 0, 1, jnp.where(blk == n_blocks - 1, 2, 0))
        bias1 = bias_ref[variant]
        bias = jnp.concatenate([bias1, bias1, bias1, bias1], axis=0)
        lhs_a = jnp.concatenate([q_lo[0][qs], q_lo[1][qs], q_hi[2][qs], q_hi[3][qs]], axis=0)
        lhs_b = jnp.concatenate([q_hi[0][qs], q_hi[1][qs], q_lo[2][qs], q_lo[3][qs]], axis=0)
        ws = slice(j * QB, j * QB + 3 * QB)
        o_a = _window_attention(lhs_a, k_s[ws], v_s[ws], bias, sink_a)
        o_b = _window_attention(lhs_b, ksw_s[ws], vsw_s[ws], bias, sink_b)
        cols = [jnp.where(low_head, o_a[0:QB], o_b[0:QB]),
                jnp.where(low_head, o_a[QB:2 * QB], o_b[QB:2 * QB]),
                jnp.where(low_head, o_b[2 * QB:3 * QB], o_a[2 * QB:3 * QB]),
                jnp.where(low_head, o_b[3 * QB:4 * QB], o_a[3 * QB:4 * QB])]
        yb_blocks.append(jnp.concatenate(cols, axis=1))
    yb = jnp.concatenate(yb_blocks, axis=0)
    yb = yb * _silu(_dot(h, win_ref[:, C_BZ:C_BZ + ATTN_WIDTH]))
    ub = _dot(yb.astype(BF16), wub_ref[...])
    mrg_s[...] = _sigmoid(_dot(h, win_ref[:, C_G1:C_G1 + D_MODEL])) * ub

    hx = h_s[HALO - CONV_HALO:HALO + TS + CONV_HALO]
    cx = _dot(hx, win_ref[:, C_AC:C_AX + A_WIDTH])
    cu_s[...] = cx[:, :A_WIDTH] * cx[:, A_WIDTH:]
    pos = row0 + lax.broadcasted_iota(jnp.int32, (TS, 1), 0)
    cu_prev = jnp.where(pos == 0, 0.0, cu_s[CONV_HALO - 1:CONV_HALO - 1 + TS])
    cu_next = jnp.where(pos == SEQ - 1, 0.0, cu_s[CONV_HALO + 1:CONV_HALO + 1 + TS])
    y = cu_prev * wconv_ref[0:1] + cu_s[CONV_HALO:CONV_HALO + TS] * wconv_ref[1:2] + cu_next * wconv_ref[2:3]
    ya = _dot(h, win_ref[:, C_AB:C_AB + A_WIDTH]) * y * _silu(_dot(h, win_ref[:, C_AZ:C_AZ + A_WIDTH]))
    ua = _dot(ya.astype(BF16), wua_ref[...])
    mrg_s[...] += _sigmoid(_dot(h, win_ref[:, C_G0:C_G0 + D_MODEL])) * ua

    mq = _dot(h, win_ref[:, C_MQ:C_MQ + MEM_WIDTH]) * ((MEM_HEAD_DIM ** -0.5) * LOG2E)
    ym_heads = []
    for hh in range(MEM_HEADS):
        hs = slice(hh * MEM_HEAD_DIM, (hh + 1) * MEM_HEAD_DIM)
        s = _dot(mq[:, hs].astype(BF16), mkt_ref[0, hs, :])
        p = jnp.exp2(s - jnp.max(s, axis=1, keepdims=True))
        ov = _dot(p.astype(BF16), mvx_ref[0, hh])
        ym_heads.append(ov[:, :MEM_HEAD_DIM] / ov[:, MEM_HEAD_DIM:])
    ym = jnp.concatenate(ym_heads, axis=1) * _silu(_dot(h, win_ref[:, C_MZ:C_MZ + MEM_WIDTH]))
    um = _dot(ym.astype(BF16), wum_ref[...])
    merged = mrg_s[...] + _sigmoid(_dot(h, win_ref[:, C_G2:C_G2 + D_MODEL])) * um

    o = _dot(merged.astype(BF16), wout_ref[...])
    out_ref[0] = x_ref[0] + _rms(o, gpost_ref[...])


def _resident(shape):
    return pl.BlockSpec(shape, lambda b, i: (0,) * len(shape), pipeline_mode=pl.Buffered(1))


def _mem_kv(mem, g_mem, w_mem_kv):
    bsz = mem.shape[0]
    return pl.pallas_call(
        _mem_kv_kernel,
        grid=(bsz,),
        in_specs=[pl.BlockSpec((1, MEM_LEN, D_MODEL), lambda b: (b, 0, 0)),
                  pl.BlockSpec((1, D_MODEL), lambda b: (0, 0)),
                  pl.BlockSpec((D_MODEL, 2 * MEM_WIDTH), lambda b: (0, 0))],
        out_specs=[pl.BlockSpec((1, MEM_WIDTH, MEM_LEN), lambda b: (b, 0, 0)),
                   pl.BlockSpec((1, MEM_HEADS, MEM_LEN, 2 * MEM_HEAD_DIM), lambda b: (b, 0, 0, 0))],
        out_shape=[jax.ShapeDtypeStruct((bsz, MEM_WIDTH, MEM_LEN), BF16),
                   jax.ShapeDtypeStruct((bsz, MEM_HEADS, MEM_LEN, 2 * MEM_HEAD_DIM), BF16)],
        compiler_params=pltpu.CompilerParams(dimension_semantics=("arbitrary",)),
        name="mem_kv",
    )(mem, g_mem.reshape(1, D_MODEL), w_mem_kv.astype(BF16))


def _layer(x, mkt, mvx, g_pre, w_in, w_conv, attn_sink, w_up_a, w_up_b, w_up_m, w_out, g_post, cos_t, sin_t, bias_t):
    bsz, s, d = x.shape
    assert (s, d) == (SEQ, D_MODEL) and s % TS == 0 and TS % QB == 0 and (TS + 2 * HALO) % 256 == 0
    nb = TS // HALO
    ext = TS + 2 * HALO
    in_specs = [
        pl.BlockSpec(memory_space=pltpu.SMEM),
        pl.BlockSpec((1, TS, d), lambda b, i: (b, i, 0)),
        pl.BlockSpec((1, HALO, d), lambda b, i: (b, jnp.maximum(i * nb - 1, 0), 0)),
        pl.BlockSpec((1, HALO, d), lambda b, i: (b, jnp.minimum((i + 1) * nb, s // HALO - 1), 0)),
        _resident((1, d)),
        _resident((d, IN_WIDTH)),
        _resident((CONV_WIDTH, A_WIDTH)),
        _resident((s + 2 * HALO, LANES)),
        _resident((s + 2 * HALO, LANES)),
        _resident((3, QB, 3 * QB)),
        pl.BlockSpec((1, MEM_WIDTH, MEM_LEN), lambda b, i: (b, 0, 0)),
        pl.BlockSpec((1, MEM_HEADS, MEM_LEN, 2 * MEM_HEAD_DIM), lambda b, i: (b, 0, 0, 0)),
        _resident((A_WIDTH, d)), _resident((ATTN_WIDTH, d)), _resident((MEM_WIDTH, d)),
        _resident((d, d)),
        _resident((1, d)),
    ]
    scratch = [
        pltpu.VMEM((ext, d), BF16),
        pltpu.VMEM((ext, KV_WIDTH), BF16),
        pltpu.VMEM((ext, KV_WIDTH), BF16),
        pltpu.VMEM((ext, 2 * KV_WIDTH), BF16),
        pltpu.VMEM((ext, 2 * KV_WIDTH), BF16),
        pltpu.VMEM((TS + 2 * CONV_HALO, A_WIDTH), F32),
        pltpu.VMEM((TS, d), F32),
    ]
    return pl.pallas_call(
        _layer_kernel,
        grid=(bsz, s // TS),
        in_specs=in_specs,
        out_specs=pl.BlockSpec((1, TS, d), lambda b, i: (b, i, 0)),
        out_shape=jax.ShapeDtypeStruct(x.shape, x.dtype),
        scratch_shapes=scratch,
        compiler_params=pltpu.CompilerParams(dimension_semantics=("arbitrary", "arbitrary"),
                                             vmem_limit_bytes=VMEM_LIMIT_BYTES),
        name="hybrid_layer",
    )(attn_sink, x, x, x, g_pre.reshape(1, d), w_in.astype(BF16), w_conv, cos_t, sin_t, bias_t, mkt, mvx,
      w_up_a.astype(BF16), w_up_b.astype(BF16), w_up_m.astype(BF16), w_out.astype(BF16), g_post.reshape(1, d))


def kernel(x, mem, g_pre, w_in, w_conv, attn_sink, g_mem, w_mem_kv, w_up_a, w_up_b, w_up_m, w_out, g_post):
    cos_np, sin_np = _rope_tables()
    cos_t, sin_t, bias_t = jnp.asarray(cos_np), jnp.asarray(sin_np), jnp.asarray(_band_bias())
    for l in range(g_pre.shape[0]):
        mkt, mvx = _mem_kv(mem, g_mem[l], w_mem_kv[l])
        x = _layer(x, mkt, mvx, g_pre[l], w_in[l], w_conv[l], attn_sink[l], w_up_a[l], w_up_b[l], w_up_m[l],
                   w_out[l], g_post[l], cos_t, sin_t, bias_t)
    return x
```

```python
import math

import numpy as np
import jax
import jax.numpy as jnp
from jax import lax
from jax.experimental import pallas as pl
from jax.experimental.pallas import tpu as pltpu

F32 = jnp.float32
BF16 = jnp.bfloat16

D_MODEL = 1024
SEQ = 4096
MEM_LEN = 256
EPS = 1e-6
CONV_WIDTH = 3
A_WIDTH = 512
HEAD_DIM = 64
ATTN_WIDTH = 512
N_Q_HEADS = 8
KV_WIDTH = 128
WINDOW = 128
ROPE_THETA = 500000.0
ROT_DIM = 16
MEM_HEADS = 4
MEM_HEAD_DIM = 128
MEM_WIDTH = 512

C_AB, C_AC, C_AX, C_AZ = 0, 512, 1024, 1536
C_BQ, C_BK, C_BV, C_BZ = 2048, 2560, 2688, 2816
C_MQ, C_MZ = 3328, 3840
C_G0, C_G1, C_G2 = 4352, 5376, 6400
IN_WIDTH = 7424

LANES = 128
BF16_ROWS = 16
VMEM_LIMIT_BYTES = 56 * 1024 * 1024

TS = 256
QB = WINDOW
HALO = WINDOW
CONV_HALO = BF16_ROWS

LOG2E = math.log2(math.e)


def _sigmoid(v):
    return 1.0 / (1.0 + jnp.exp(-v))


def _silu(v):
    return v * _sigmoid(v)


def _rms(v, g):
    ms = jnp.mean(v * v, axis=-1, keepdims=True)
    return v * lax.rsqrt(ms + EPS) * g


def _dot(a, b):
    return jnp.dot(a, b, preferred_element_type=F32)


def _dot_nt(a, b):
    return lax.dot_general(a, b, (((1,), (1,)), ((), ())), preferred_element_type=F32)


def _rope(t, cos, ssin, low8):
    partner = jnp.where(low8, pltpu.roll(t, LANES - ROT_DIM // 2, 1), pltpu.roll(t, ROT_DIM // 2, 1))
    return t * cos + partner * ssin


def _rope_tables():
    half = ROT_DIM // 2
    inv_freq = np.power(np.float32(ROPE_THETA), -np.arange(half, dtype=np.float32) * np.float32(2.0 / ROT_DIM))
    pos = (np.arange(SEQ + 2 * HALO) - HALO).astype(np.float32)
    ang = (pos[:, None] * inv_freq[None, :]).astype(np.float32)
    cos, sin = np.cos(ang).astype(np.float32), np.sin(ang).astype(np.float32)
    ct = np.ones((SEQ + 2 * HALO, HEAD_DIM), np.float32)
    st = np.zeros((SEQ + 2 * HALO, HEAD_DIM), np.float32)
    ct[:, :half], ct[:, half:ROT_DIM] = cos, cos
    st[:, :half], st[:, half:ROT_DIM] = -sin, sin
    return np.tile(ct, (1, LANES // HEAD_DIM)), np.tile(st, (1, LANES // HEAD_DIM))


def _band_bias():
    r = np.arange(QB)[:, None]
    c = np.arange(3 * QB)[None, :]
    band = (c >= r) & (c <= r + 2 * WINDOW)
    masks = [band, band & (c >= QB), band & (c < 2 * QB)]
    return np.stack([np.where(m, 0.0, -np.inf) for m in masks]).astype(np.float32)


def _mem_kv_kernel(mem_ref, g_ref, w_ref, mkt_ref, mvx_ref):
    mn = _rms(mem_ref[0], g_ref[...]).astype(BF16)
    kv = _dot(mn, w_ref[...])
    mkt_ref[0] = kv[:, :MEM_WIDTH].T.astype(BF16)
    ones = jnp.ones((MEM_LEN, MEM_HEAD_DIM), F32)
    for hh in range(MEM_HEADS):
        c0 = MEM_WIDTH + hh * MEM_HEAD_DIM
        mvx_ref[0, hh] = jnp.concatenate([kv[:, c0:c0 + MEM_HEAD_DIM], ones], axis=1).astype(BF16)


def _window_softmax_pv(s, v_win, sink_col):
    m = jnp.maximum(jnp.max(s, axis=1, keepdims=True), sink_col)
    p = jnp.exp2(s - m)
    ov = _dot(p.astype(BF16), v_win)
    return ov[:, :LANES] / (ov[:, LANES:] + jnp.exp2(sink_col - m))


def _layer_kernel(sink_ref, x_ref, xp_ref, xn_ref, gpre_ref, win_ref, wconv_ref, cos_ref, sin_ref, bias_ref,
                  mkt_ref, mvx_ref, wua_ref, wub_ref, wum_ref, wout_ref, gpost_ref,
                  out_ref, h_s, k_s, ksw_s, v_s, vsw_s, cu_s, mrg_s, gate_s):
    i = pl.program_id(1)
    row0 = pl.multiple_of(i * TS, TS)
    ext = TS + 2 * HALO

    g_pre = gpre_ref[...]
    h_s[0:HALO] = _rms(xp_ref[0], g_pre).astype(BF16)
    h_s[HALO:HALO + TS] = _rms(x_ref[0], g_pre).astype(BF16)
    h_s[HALO + TS:ext] = _rms(xn_ref[0], g_pre).astype(BF16)

    lane = lax.broadcasted_iota(jnp.int32, (1, LANES), 1)
    low_head = lane < HEAD_DIM
    low8 = (lane % HEAD_DIM) < (ROT_DIM // 2)

    kv_chunk = 256
    ones = jnp.ones((kv_chunk, LANES), BF16)
    for c in range(ext // kv_chunk):
        r0 = c * kv_chunk
        kv = _dot(h_s[r0:r0 + kv_chunk], win_ref[:, C_BK:C_BV + KV_WIDTH])
        cos = cos_ref[pl.ds(row0 + r0, kv_chunk), :]
        ssin = sin_ref[pl.ds(row0 + r0, kv_chunk), :]
        k = _rope(kv[:, :KV_WIDTH], cos, ssin, low8)
        v = kv[:, KV_WIDTH:]
        k_s[r0:r0 + kv_chunk] = k.astype(BF16)
        ksw_s[r0:r0 + kv_chunk] = pltpu.roll(k, HEAD_DIM, 1).astype(BF16)
        v_s[r0:r0 + kv_chunk, 0:LANES] = v.astype(BF16)
        v_s[r0:r0 + kv_chunk, LANES:2 * LANES] = ones
        vsw_s[r0:r0 + kv_chunk, 0:LANES] = pltpu.roll(v, HEAD_DIM, 1).astype(BF16)
        vsw_s[r0:r0 + kv_chunk, LANES:2 * LANES] = ones

    h = h_s[HALO:HALO + TS]

    scale = (HEAD_DIM ** -0.5) * LOG2E
    qcos = cos_ref[pl.ds(row0 + HALO, TS), :] * scale
    qsin = sin_ref[pl.ds(row0 + HALO, TS), :] * scale
    q = _dot(h, win_ref[:, C_BQ:C_BQ + ATTN_WIDTH])
    q_lo, q_hi = [], []
    for mblk in range(ATTN_WIDTH // LANES):
        qr = _rope(q[:, mblk * LANES:(mblk + 1) * LANES], qcos, qsin, low8)
        q_lo.append(jnp.where(low_head, qr, 0.0).astype(BF16))
        q_hi.append(jnp.where(low_head, 0.0, qr).astype(BF16))

    rows4 = lax.broadcasted_iota(jnp.int32, (4 * QB, 1), 0)
    def sink_rows(h0, h1, h2, h3):
        return LOG2E * jnp.where(rows4 < QB, sink_ref[h0],
                                 jnp.where(rows4 < 2 * QB, sink_ref[h1],
                                           jnp.where(rows4 < 3 * QB, sink_ref[h2], sink_ref[h3])))
    sink_a = sink_rows(0, 2, 5, 7)
    sink_b = sink_rows(1, 3, 4, 6)

    def gate(col, slot):
        gate_s[slot] = _sigmoid(_dot(h, win_ref[:, col:col + D_MODEL]))

    def conv_input():
        hx = h_s[HALO - CONV_HALO:HALO + TS + CONV_HALO]
        cx = _dot(hx, win_ref[:, C_AC:C_AX + A_WIDTH])
        cu_s[...] = cx[:, :A_WIDTH] * cx[:, A_WIDTH:]

    fillers = [lambda: gate(C_G1, 1), conv_input]

    n_blocks = SEQ // QB
    yb_blocks = []
    for j in range(TS // QB):
        qs = slice(j * QB, (j + 1) * QB)
        blk = i * (TS // QB) + j
        variant = jnp.where(blk == 0, 1, jnp.where(blk == n_blocks - 1, 2, 0))
        bias1 = bias_ref[variant]
        bias = jnp.concatenate([bias1, bias1, bias1, bias1], axis=0)
        lhs_a = jnp.concatenate([q_lo[0][qs], q_lo[1][qs], q_hi[2][qs], q_hi[3][qs]], axis=0)
        lhs_b = jnp.concatenate([q_hi[0][qs], q_hi[1][qs], q_lo[2][qs], q_lo[3][qs]], axis=0)
        ws = slice(j * QB, j * QB + 3 * QB)
        s_a = _dot_nt(lhs_a, k_s[ws]) + bias
        s_b = _dot_nt(lhs_b, ksw_s[ws]) + bias
        fillers[j % len(fillers)]()
        o_a = _window_softmax_pv(s_a, v_s[ws], sink_a)
        o_b = _window_softmax_pv(s_b, vsw_s[ws], sink_b)
        cols = [jnp.where(low_head, o_a[0:QB], o_b[0:QB]),
                jnp.where(low_head, o_a[QB:2 * QB], o_b[QB:2 * QB]),
                jnp.where(low_head, o_b[2 * QB:3 * QB], o_a[2 * QB:3 * QB]),
                jnp.where(low_head, o_b[3 * QB:4 * QB], o_a[3 * QB:4 * QB])]
        yb_blocks.append(jnp.concatenate(cols, axis=1))
    yb = jnp.concatenate(yb_blocks, axis=0)
    yb = yb * _silu(_dot(h, win_ref[:, C_BZ:C_BZ + ATTN_WIDTH]))
    ub = _dot(yb.astype(BF16), wub_ref[...])
    mrg_s[...] = gate_s[1] * ub

    a_b = _dot(h, win_ref[:, C_AB:C_AB + A_WIDTH])
    a_zs = _silu(_dot(h, win_ref[:, C_AZ:C_AZ + A_WIDTH]))
    gate(C_G0, 0)
    pos = row0 + lax.broadcasted_iota(jnp.int32, (TS, 1), 0)
    cu_prev = jnp.where(pos == 0, 0.0, cu_s[CONV_HALO - 1:CONV_HALO - 1 + TS])
    cu_next = jnp.where(pos == SEQ - 1, 0.0, cu_s[CONV_HALO + 1:CONV_HALO + 1 + TS])
    y = cu_prev * wconv_ref[0:1] + cu_s[CONV_HALO:CONV_HALO + TS] * wconv_ref[1:2] + cu_next * wconv_ref[2:3]
    ya = a_b * y * a_zs
    ua = _dot(ya.astype(BF16), wua_ref[...])
    mrg_s[...] += gate_s[0] * ua

    mq = _dot(h, win_ref[:, C_MQ:C_MQ + MEM_WIDTH]) * ((MEM_HEAD_DIM ** -0.5) * LOG2E)
    scores = []
    for hh in range(MEM_HEADS):
        hs = slice(hh * MEM_HEAD_DIM, (hh + 1) * MEM_HEAD_DIM)
        scores.append(_dot(mq[:, hs].astype(BF16), mkt_ref[0, hs, :]))
    m_zs = _silu(_dot(h, win_ref[:, C_MZ:C_MZ + MEM_WIDTH]))
    gate(C_G2, 2)
    ym_heads = []
    for hh in range(MEM_HEADS):
        s = scores[hh]
        p = jnp.exp2(s - jnp.max(s, axis=1, keepdims=True))
        ov = _dot(p.astype(BF16), mvx_ref[0, hh])
        ym_heads.append(ov[:, :MEM_HEAD_DIM] / ov[:, MEM_HEAD_DIM:])
    ym = jnp.concatenate(ym_heads, axis=1) * m_zs
    um = _dot(ym.astype(BF16), wum_ref[...])
    merged = mrg_s[...] + gate_s[2] * um

    o = _dot(merged.astype(BF16), wout_ref[...])
    out_ref[0] = x_ref[0] + _rms(o, gpost_ref[...])


def _resident(shape):
    return pl.BlockSpec(shape, lambda b, i: (0,) * len(shape), pipeline_mode=pl.Buffered(1))


def _mem_kv(mem, g_mem, w_mem_kv):
    bsz = mem.shape[0]
    return pl.pallas_call(
        _mem_kv_kernel,
        grid=(bsz,),
        in_specs=[pl.BlockSpec((1, MEM_LEN, D_MODEL), lambda b: (b, 0, 0)),
                  pl.BlockSpec((1, D_MODEL), lambda b: (0, 0)),
                  pl.BlockSpec((D_MODEL, 2 * MEM_WIDTH), lambda b: (0, 0))],
        out_specs=[pl.BlockSpec((1, MEM_WIDTH, MEM_LEN), lambda b: (b, 0, 0)),
                   pl.BlockSpec((1, MEM_HEADS, MEM_LEN, 2 * MEM_HEAD_DIM), lambda b: (b, 0, 0, 0))],
        out_shape=[jax.ShapeDtypeStruct((bsz, MEM_WIDTH, MEM_LEN), BF16),
                   jax.ShapeDtypeStruct((bsz, MEM_HEADS, MEM_LEN, 2 * MEM_HEAD_DIM), BF16)],
        compiler_params=pltpu.CompilerParams(dimension_semantics=("arbitrary",)),
        name="mem_kv",
    )(mem, g_mem.reshape(1, D_MODEL), w_mem_kv.astype(BF16))


def _layer(x, mkt, mvx, g_pre, w_in, w_conv, attn_sink, w_up_a, w_up_b, w_up_m, w_out, g_post, cos_t, sin_t, bias_t):
    bsz, s, d = x.shape
    assert (s, d) == (SEQ, D_MODEL) and s % TS == 0 and TS % QB == 0 and (TS + 2 * HALO) % 256 == 0
    nb = TS // HALO
    ext = TS + 2 * HALO
    in_specs = [
        pl.BlockSpec(memory_space=pltpu.SMEM),
        pl.BlockSpec((1, TS, d), lambda b, i: (b, i, 0)),
        pl.BlockSpec((1, HALO, d), lambda b, i: (b, jnp.maximum(i * nb - 1, 0), 0)),
        pl.BlockSpec((1, HALO, d), lambda b, i: (b, jnp.minimum((i + 1) * nb, s // HALO - 1), 0)),
        _resident((1, d)),
        _resident((d, IN_WIDTH)),
        _resident((CONV_WIDTH, A_WIDTH)),
        _resident((s + 2 * HALO, LANES)),
        _resident((s + 2 * HALO, LANES)),
        _resident((3, QB, 3 * QB)),
        pl.BlockSpec((1, MEM_WIDTH, MEM_LEN), lambda b, i: (b, 0, 0)),
        pl.BlockSpec((1, MEM_HEADS, MEM_LEN, 2 * MEM_HEAD_DIM), lambda b, i: (b, 0, 0, 0)),
        _resident((A_WIDTH, d)), _resident((ATTN_WIDTH, d)), _resident((MEM_WIDTH, d)),
        _resident((d, d)),
        _resident((1, d)),
    ]
    scratch = [
        pltpu.VMEM((ext, d), BF16),
        pltpu.VMEM((ext, KV_WIDTH), BF16),
        pltpu.VMEM((ext, KV_WIDTH), BF16),
        pltpu.VMEM((ext, 2 * KV_WIDTH), BF16),
        pltpu.VMEM((ext, 2 * KV_WIDTH), BF16),
        pltpu.VMEM((TS + 2 * CONV_HALO, A_WIDTH), F32),
        pltpu.VMEM((TS, d), F32),
        pltpu.VMEM((3, TS, d), F32),
    ]
    return pl.pallas_call(
        _layer_kernel,
        grid=(bsz, s // TS),
        in_specs=in_specs,
        out_specs=pl.BlockSpec((1, TS, d), lambda b, i: (b, i, 0)),
        out_shape=jax.ShapeDtypeStruct(x.shape, x.dtype),
        scratch_shapes=scratch,
        compiler_params=pltpu.CompilerParams(dimension_semantics=("arbitrary", "arbitrary"),
                                             vmem_limit_bytes=VMEM_LIMIT_BYTES),
        name="hybrid_layer",
    )(attn_sink, x, x, x, g_pre.reshape(1, d), w_in.astype(BF16), w_conv, cos_t, sin_t, bias_t, mkt, mvx,
      w_up_a.astype(BF16), w_up_b.astype(BF16), w_up_m.astype(BF16), w_out.astype(BF16), g_post.reshape(1, d))


def kernel(x, mem, g_pre, w_in, w_conv, attn_sink, g_mem, w_mem_kv, w_up_a, w_up_b, w_up_m, w_out, g_post):
    cos_np, sin_np = _rope_tables()
    cos_t, sin_t, bias_t = jnp.asarray(cos_np), jnp.asarray(sin_np), jnp.asarray(_band_bias())
    for l in range(g_pre.shape[0]):
        mkt, mvx = _mem_kv(mem, g_mem[l], w_mem_kv[l])
        x = _layer(x, mkt, mvx, g_pre[l], w_in[l], w_conv[l], attn_sink[l], w_up_a[l], w_up_b[l], w_up_m[l],
                   w_out[l], g_post[l], cos_t, sin_t, bias_t)
    return x
```

```python
import math

import numpy as np
import jax
import jax.numpy as jnp
from jax import lax
from jax.experimental import pallas as pl
from jax.experimental.pallas import tpu as pltpu

F32 = jnp.float32
BF16 = jnp.bfloat16

D_MODEL = 1024
SEQ = 4096
MEM_LEN = 256
EPS = 1e-6
CONV_WIDTH = 3
A_WIDTH = 512
HEAD_DIM = 64
ATTN_WIDTH = 512
N_Q_HEADS = 8
KV_WIDTH = 128
WINDOW = 128
ROPE_THETA = 500000.0
ROT_DIM = 16
MEM_HEADS = 4
MEM_HEAD_DIM = 128
MEM_WIDTH = 512

C_AB, C_AC, C_AX, C_AZ = 0, 512, 1024, 1536
C_BQ, C_BK, C_BV, C_BZ = 2048, 2560, 2688, 2816
C_MQ, C_MZ = 3328, 3840
C_G0, C_G1, C_G2 = 4352, 5376, 6400
IN_WIDTH = 7424

LANES = 128
BF16_ROWS = 16
VMEM_LIMIT_BYTES = 56 * 1024 * 1024

TS = 256
QB = WINDOW
HALO = WINDOW
CONV_HALO = BF16_ROWS
RMS_ROWS = 64

LOG2E = math.log2(math.e)


def _sigmoid(v):
    return 1.0 / (1.0 + jnp.exp(-v))


def _silu(v):
    return v * _sigmoid(v)


def _rms(v, g):
    ms = jnp.mean(v * v, axis=-1, keepdims=True)
    return v * lax.rsqrt(ms + EPS) * g


def _dot(a, b):
    return jnp.dot(a, b, preferred_element_type=F32)


def _dot_nt(a, b):
    return lax.dot_general(a, b, (((1,), (1,)), ((), ())), preferred_element_type=F32)


def _rope(t, cos, ssin, low8):
    partner = jnp.where(low8, pltpu.roll(t, LANES - ROT_DIM // 2, 1), pltpu.roll(t, ROT_DIM // 2, 1))
    return t * cos + partner * ssin


def _rope_tables():
    half = ROT_DIM // 2
    inv_freq = np.power(np.float32(ROPE_THETA), -np.arange(half, dtype=np.float32) * np.float32(2.0 / ROT_DIM))
    pos = (np.arange(SEQ + 2 * HALO) - HALO).astype(np.float32)
    ang = (pos[:, None] * inv_freq[None, :]).astype(np.float32)
    cos, sin = np.cos(ang).astype(np.float32), np.sin(ang).astype(np.float32)
    ct = np.ones((SEQ + 2 * HALO, HEAD_DIM), np.float32)
    st = np.zeros((SEQ + 2 * HALO, HEAD_DIM), np.float32)
    ct[:, :half], ct[:, half:ROT_DIM] = cos, cos
    st[:, :half], st[:, half:ROT_DIM] = -sin, sin
    return np.tile(ct, (1, LANES // HEAD_DIM)), np.tile(st, (1, LANES // HEAD_DIM))


def _band_bias():
    r = np.arange(QB)[:, None]
    c = np.arange(3 * QB)[None, :]
    band = (c >= r) & (c <= r + 2 * WINDOW)
    masks = [band, band & (c >= QB), band & (c < 2 * QB)]
    return np.stack([np.where(m, 0.0, -np.inf) for m in masks]).astype(np.float32)


def _mem_kv_kernel(mem_ref, g_ref, w_ref, mkt_ref, mvx_ref):
    mn = _rms(mem_ref[0], g_ref[...]).astype(BF16)
    kv = _dot(mn, w_ref[...])
    mkt_ref[0] = kv[:, :MEM_WIDTH].T.astype(BF16)
    ones = jnp.ones((MEM_LEN, MEM_HEAD_DIM), F32)
    for hh in range(MEM_HEADS):
        c0 = MEM_WIDTH + hh * MEM_HEAD_DIM
        mvx_ref[0, hh] = jnp.concatenate([kv[:, c0:c0 + MEM_HEAD_DIM], ones], axis=1).astype(BF16)


def _window_softmax_pv(s, v_win, sink_col):
    m = jnp.maximum(jnp.max(s, axis=1, keepdims=True), sink_col)
    p = jnp.exp2(s - m)
    ov = _dot(p.astype(BF16), v_win)
    return ov[:, :LANES] / (ov[:, LANES:] + jnp.exp2(sink_col - m))


def _layer_kernel(sink_ref, x_ref, xp_ref, xn_ref, gpre_ref, win_ref, wconv_ref, cos_ref, sin_ref, bias_ref,
                  mkt_ref, mvx_ref, wua_ref, wub_ref, wum_ref, wout_ref, gpost_ref,
                  out_ref, h_s, k_s, ksw_s, v_s, vsw_s, cu_s, mrg_s, gate_s):
    i = pl.program_id(1)
    row0 = pl.multiple_of(i * TS, TS)
    ext = TS + 2 * HALO

    g_pre = gpre_ref[...]
    for r0 in range(0, TS, RMS_ROWS):
        h_s[HALO + r0:HALO + r0 + RMS_ROWS] = _rms(x_ref[0, r0:r0 + RMS_ROWS], g_pre).astype(BF16)
    h = h_s[HALO:HALO + TS]

    lane = lax.broadcasted_iota(jnp.int32, (1, LANES), 1)
    low_head = lane < HEAD_DIM
    low8 = (lane % HEAD_DIM) < (ROT_DIM // 2)

    q = _dot(h, win_ref[:, C_BQ:C_BQ + ATTN_WIDTH])
    b_zs = _silu(_dot(h, win_ref[:, C_BZ:C_BZ + ATTN_WIDTH]))

    h_s[0:HALO] = _rms(xp_ref[0], g_pre).astype(BF16)
    h_s[HALO + TS:ext] = _rms(xn_ref[0], g_pre).astype(BF16)

    kv_chunk = 256
    ones = jnp.ones((kv_chunk, LANES), BF16)
    kvs = [_dot(h_s[c * kv_chunk:(c + 1) * kv_chunk], win_ref[:, C_BK:C_BV + KV_WIDTH])
           for c in range(ext // kv_chunk)]
    a_b = _dot(h, win_ref[:, C_AB:C_AB + A_WIDTH])
    a_zs = _silu(_dot(h, win_ref[:, C_AZ:C_AZ + A_WIDTH]))
    for c, kv in enumerate(kvs):
        r0 = c * kv_chunk
        cos = cos_ref[pl.ds(row0 + r0, kv_chunk), :]
        ssin = sin_ref[pl.ds(row0 + r0, kv_chunk), :]
        k = _rope(kv[:, :KV_WIDTH], cos, ssin, low8)
        v = kv[:, KV_WIDTH:]
        k_s[r0:r0 + kv_chunk] = k.astype(BF16)
        ksw_s[r0:r0 + kv_chunk] = pltpu.roll(k, HEAD_DIM, 1).astype(BF16)
        v_s[r0:r0 + kv_chunk, 0:LANES] = v.astype(BF16)
        v_s[r0:r0 + kv_chunk, LANES:2 * LANES] = ones
        vsw_s[r0:r0 + kv_chunk, 0:LANES] = pltpu.roll(v, HEAD_DIM, 1).astype(BF16)
        vsw_s[r0:r0 + kv_chunk, LANES:2 * LANES] = ones

    scale = (HEAD_DIM ** -0.5) * LOG2E
    qcos = cos_ref[pl.ds(row0 + HALO, TS), :] * scale
    qsin = sin_ref[pl.ds(row0 + HALO, TS), :] * scale
    q_lo, q_hi = [], []
    for mblk in range(ATTN_WIDTH // LANES):
        qr = _rope(q[:, mblk * LANES:(mblk + 1) * LANES], qcos, qsin, low8)
        q_lo.append(jnp.where(low_head, qr, 0.0).astype(BF16))
        q_hi.append(jnp.where(low_head, 0.0, qr).astype(BF16))

    rows4 = lax.broadcasted_iota(jnp.int32, (4 * QB, 1), 0)
    def sink_rows(h0, h1, h2, h3):
        return LOG2E * jnp.where(rows4 < QB, sink_ref[h0],
                                 jnp.where(rows4 < 2 * QB, sink_ref[h1],
                                           jnp.where(rows4 < 3 * QB, sink_ref[h2], sink_ref[h3])))
    sink_a = sink_rows(0, 2, 5, 7)
    sink_b = sink_rows(1, 3, 4, 6)

    def gate(col, slot):
        gate_s[slot] = _sigmoid(_dot(h, win_ref[:, col:col + D_MODEL]))

    def conv_input():
        hx = h_s[HALO - CONV_HALO:HALO + TS + CONV_HALO]
        cx = _dot(hx, win_ref[:, C_AC:C_AX + A_WIDTH])
        cu_s[...] = cx[:, :A_WIDTH] * cx[:, A_WIDTH:]

    fillers = [lambda: gate(C_G1, 1), conv_input]

    n_blocks = SEQ // QB
    yb_blocks = []
    for j in range(TS // QB):
        qs = slice(j * QB, (j + 1) * QB)
        blk = i * (TS // QB) + j
        variant = jnp.where(blk == 0, 1, jnp.where(blk == n_blocks - 1, 2, 0))
        bias1 = bias_ref[variant]
        bias = jnp.concatenate([bias1, bias1, bias1, bias1], axis=0)
        lhs_a = jnp.concatenate([q_lo[0][qs], q_lo[1][qs], q_hi[2][qs], q_hi[3][qs]], axis=0)
        lhs_b = jnp.concatenate([q_hi[0][qs], q_hi[1][qs], q_lo[2][qs], q_lo[3][qs]], axis=0)
        ws = slice(j * QB, j * QB + 3 * QB)
        s_a = _dot_nt(lhs_a, k_s[ws]) + bias
        s_b = _dot_nt(lhs_b, ksw_s[ws]) + bias
        fillers[j % len(fillers)]()
        o_a = _window_softmax_pv(s_a, v_s[ws], sink_a)
        o_b = _window_softmax_pv(s_b, vsw_s[ws], sink_b)
        cols = [jnp.where(low_head, o_a[0:QB], o_b[0:QB]),
                jnp.where(low_head, o_a[QB:2 * QB], o_b[QB:2 * QB]),
                jnp.where(low_head, o_b[2 * QB:3 * QB], o_a[2 * QB:3 * QB]),
                jnp.where(low_head, o_b[3 * QB:4 * QB], o_a[3 * QB:4 * QB])]
        yb_blocks.append(jnp.concatenate(cols, axis=1))
    yb = jnp.concatenate(yb_blocks, axis=0)
    yb = yb * b_zs
    ub = _dot(yb.astype(BF16), wub_ref[...])
    mrg_s[...] = gate_s[1] * ub

    gate(C_G0, 0)
    pos = row0 + lax.broadcasted_iota(jnp.int32, (TS, 1), 0)
    cu_prev = jnp.where(pos == 0, 0.0, cu_s[CONV_HALO - 1:CONV_HALO - 1 + TS])
    cu_next = jnp.where(pos == SEQ - 1, 0.0, cu_s[CONV_HALO + 1:CONV_HALO + 1 + TS])
    y = cu_prev * wconv_ref[0:1] + cu_s[CONV_HALO:CONV_HALO + TS] * wconv_ref[1:2] + cu_next * wconv_ref[2:3]
    ya = a_b * y * a_zs
    ua = _dot(ya.astype(BF16), wua_ref[...])
    mrg_s[...] += gate_s[0] * ua

    mq = _dot(h, win_ref[:, C_MQ:C_MQ + MEM_WIDTH]) * ((MEM_HEAD_DIM ** -0.5) * LOG2E)
    scores = []
    for hh in range(MEM_HEADS):
        hs = slice(hh * MEM_HEAD_DIM, (hh + 1) * MEM_HEAD_DIM)
        scores.append(_dot(mq[:, hs].astype(BF16), mkt_ref[0, hs, :]))
    m_zs = _silu(_dot(h, win_ref[:, C_MZ:C_MZ + MEM_WIDTH]))
    gate(C_G2, 2)
    ym_heads = []
    for hh in range(MEM_HEADS):
        s = scores[hh]
        p = jnp.exp2(s - jnp.max(s, axis=1, keepdims=True))
        ov = _dot(p.astype(BF16), mvx_ref[0, hh])
        ym_heads.append(ov[:, :MEM_HEAD_DIM] / ov[:, MEM_HEAD_DIM:])
    ym = (jnp.concatenate(ym_heads, axis=1) * m_zs).astype(BF16)

    g_post = gpost_ref[...]
    for r0 in range(0, TS, TS // 2):
        rs = slice(r0, r0 + TS // 2)
        um = _dot(ym[rs], wum_ref[...])
        merged = mrg_s[rs] + gate_s[2, rs] * um
        o = _dot(merged.astype(BF16), wout_ref[...])
        out_ref[0, rs] = x_ref[0, rs] + _rms(o, g_post)


def _resident(shape):
    return pl.BlockSpec(shape, lambda b, i: (0,) * len(shape), pipeline_mode=pl.Buffered(1))


def _mem_kv(mem, g_mem, w_mem_kv):
    bsz = mem.shape[0]
    return pl.pallas_call(
        _mem_kv_kernel,
        grid=(bsz,),
        in_specs=[pl.BlockSpec((1, MEM_LEN, D_MODEL), lambda b: (b, 0, 0)),
                  pl.BlockSpec((1, D_MODEL), lambda b: (0, 0)),
                  pl.BlockSpec((D_MODEL, 2 * MEM_WIDTH), lambda b: (0, 0))],
        out_specs=[pl.BlockSpec((1, MEM_WIDTH, MEM_LEN), lambda b: (b, 0, 0)),
                   pl.BlockSpec((1, MEM_HEADS, MEM_LEN, 2 * MEM_HEAD_DIM), lambda b: (b, 0, 0, 0))],
        out_shape=[jax.ShapeDtypeStruct((bsz, MEM_WIDTH, MEM_LEN), BF16),
                   jax.ShapeDtypeStruct((bsz, MEM_HEADS, MEM_LEN, 2 * MEM_HEAD_DIM), BF16)],
        compiler_params=pltpu.CompilerParams(dimension_semantics=("arbitrary",)),
        name="mem_kv",
    )(mem, g_mem.reshape(1, D_MODEL), w_mem_kv.astype(BF16))


def _layer(x, mkt, mvx, g_pre, w_in, w_conv, attn_sink, w_up_a, w_up_b, w_up_m, w_out, g_post, cos_t, sin_t, bias_t):
    bsz, s, d = x.shape
    assert (s, d) == (SEQ, D_MODEL) and s % TS == 0 and TS % QB == 0 and (TS + 2 * HALO) % 256 == 0
    nb = TS // HALO
    ext = TS + 2 * HALO
    in_specs = [
        pl.BlockSpec(memory_space=pltpu.SMEM),
        pl.BlockSpec((1, TS, d), lambda b, i: (b, i, 0)),
        pl.BlockSpec((1, HALO, d), lambda b, i: (b, jnp.maximum(i * nb - 1, 0), 0)),
        pl.BlockSpec((1, HALO, d), lambda b, i: (b, jnp.minimum((i + 1) * nb, s // HALO - 1), 0)),
        _resident((1, d)),
        _resident((d, IN_WIDTH)),
        _resident((CONV_WIDTH, A_WIDTH)),
        _resident((s + 2 * HALO, LANES)),
        _resident((s + 2 * HALO, LANES)),
        _resident((3, QB, 3 * QB)),
        pl.BlockSpec((1, MEM_WIDTH, MEM_LEN), lambda b, i: (b, 0, 0)),
        pl.BlockSpec((1, MEM_HEADS, MEM_LEN, 2 * MEM_HEAD_DIM), lambda b, i: (b, 0, 0, 0)),
        _resident((A_WIDTH, d)), _resident((ATTN_WIDTH, d)), _resident((MEM_WIDTH, d)),
        _resident((d, d)),
        _resident((1, d)),
    ]
    scratch = [
        pltpu.VMEM((ext, d), BF16),
        pltpu.VMEM((ext, KV_WIDTH), BF16),
        pltpu.VMEM((ext, KV_WIDTH), BF16),
        pltpu.VMEM((ext, 2 * KV_WIDTH), BF16),
        pltpu.VMEM((ext, 2 * KV_WIDTH), BF16),
        pltpu.VMEM((TS + 2 * CONV_HALO, A_WIDTH), F32),
        pltpu.VMEM((TS, d), F32),
        pltpu.VMEM((3, TS, d), F32),
    ]
    return pl.pallas_call(
        _layer_kernel,
        grid=(bsz, s // TS),
        in_specs=in_specs,
        out_specs=pl.BlockSpec((1, TS, d), lambda b, i: (b, i, 0)),
        out_shape=jax.ShapeDtypeStruct(x.shape, x.dtype),
        scratch_shapes=scratch,
        compiler_params=pltpu.CompilerParams(dimension_semantics=("arbitrary", "arbitrary"),
                                             vmem_limit_bytes=VMEM_LIMIT_BYTES),
        name="hybrid_layer",
    )(attn_sink, x, x, x, g_pre.reshape(1, d), w_in.astype(BF16), w_conv, cos_t, sin_t, bias_t, mkt, mvx,
      w_up_a.astype(BF16), w_up_b.astype(BF16), w_up_m.astype(BF16), w_out.astype(BF16), g_post.reshape(1, d))


def kernel(x, mem, g_pre, w_in, w_conv, attn_sink, g_mem, w_mem_kv, w_up_a, w_up_b, w_up_m, w_out, g_post):
    cos_np, sin_np = _rope_tables()
    cos_t, sin_t, bias_t = jnp.asarray(cos_np), jnp.asarray(sin_np), jnp.asarray(_band_bias())
    for l in range(g_pre.shape[0]):
        mkt, mvx = _mem_kv(mem, g_mem[l], w_mem_kv[l])
        x = _layer(x, mkt, mvx, g_pre[l], w_in[l], w_conv[l], attn_sink[l], w_up_a[l], w_up_b[l], w_up_m[l],
                   w_out[l], g_post[l], cos_t, sin_t, bias_t)
    return x
```

```python
import math

import numpy as np
import jax
import jax.numpy as jnp
from jax import lax
from jax.experimental import pallas as pl
from jax.experimental.pallas import tpu as pltpu

F32 = jnp.float32
BF16 = jnp.bfloat16

D_MODEL = 1024
SEQ = 4096
MEM_LEN = 256
EPS = 1e-6
CONV_WIDTH = 3
A_WIDTH = 512
HEAD_DIM = 64
ATTN_WIDTH = 512
N_Q_HEADS = 8
KV_WIDTH = 128
WINDOW = 128
ROPE_THETA = 500000.0
ROT_DIM = 16
MEM_HEADS = 4
MEM_HEAD_DIM = 128
MEM_WIDTH = 512

C_AB, C_AC, C_AX, C_AZ = 0, 512, 1024, 1536
C_BQ, C_BK, C_BV, C_BZ = 2048, 2560, 2688, 2816
C_MQ, C_MZ = 3328, 3840
C_G0, C_G1, C_G2 = 4352, 5376, 6400
IN_WIDTH = 7424

LANES = 128
BF16_ROWS = 16
VMEM_LIMIT_BYTES = 56 * 1024 * 1024

SB = 256
NSB = 2
TS = SB * NSB
QB = WINDOW
HALO = WINDOW
CONV_HALO = BF16_ROWS
EXT = TS + 2 * HALO

LOG2E = math.log2(math.e)


def _sigmoid(v):
    return 1.0 / (1.0 + jnp.exp(-v))


def _silu(v):
    return v * _sigmoid(v)


def _rms(v, g):
    ms = jnp.mean(v * v, axis=-1, keepdims=True)
    return v * lax.rsqrt(ms + EPS) * g


def _dot(a, b):
    return jnp.dot(a, b, preferred_element_type=F32)


def _dot_nt(a, b):
    return lax.dot_general(a, b, (((1,), (1,)), ((), ())), preferred_element_type=F32)


def _rope(t, cos, ssin, low8):
    partner = jnp.where(low8, pltpu.roll(t, LANES - ROT_DIM // 2, 1), pltpu.roll(t, ROT_DIM // 2, 1))
    return t * cos + partner * ssin


def _rope_tables():
    half = ROT_DIM // 2
    inv_freq = np.power(np.float32(ROPE_THETA), -np.arange(half, dtype=np.float32) * np.float32(2.0 / ROT_DIM))
    pos = (np.arange(SEQ + 2 * HALO) - HALO).astype(np.float32)
    ang = (pos[:, None] * inv_freq[None, :]).astype(np.float32)
    cos, sin = np.cos(ang).astype(np.float32), np.sin(ang).astype(np.float32)
    ct = np.ones((SEQ + 2 * HALO, HEAD_DIM), np.float32)
    st = np.zeros((SEQ + 2 * HALO, HEAD_DIM), np.float32)
    ct[:, :half], ct[:, half:ROT_DIM] = cos, cos
    st[:, :half], st[:, half:ROT_DIM] = -sin, sin
    return np.tile(ct, (1, LANES // HEAD_DIM)), np.tile(st, (1, LANES // HEAD_DIM))


def _band_bias():
    r = np.arange(QB)[:, None]
    c = np.arange(3 * QB)[None, :]
    band = (c >= r) & (c <= r + 2 * WINDOW)
    masks = [band, band & (c >= QB), band & (c < 2 * QB)]
    return np.stack([np.where(m, 0.0, -np.inf) for m in masks]).astype(np.float32)


def _mem_kv_kernel(mem_ref, g_ref, w_ref, mkt_ref, mvx_ref):
    mn = _rms(mem_ref[0], g_ref[...]).astype(BF16)
    kv = _dot(mn, w_ref[...])
    mkt_ref[0] = kv[:, :MEM_WIDTH].T.astype(BF16)
    ones = jnp.ones((MEM_LEN, MEM_HEAD_DIM), F32)
    for hh in range(MEM_HEADS):
        c0 = MEM_WIDTH + hh * MEM_HEAD_DIM
        mvx_ref[0, hh] = jnp.concatenate([kv[:, c0:c0 + MEM_HEAD_DIM], ones], axis=1).astype(BF16)


def _window_softmax_pv(s, v_win, sink_col):
    m = jnp.maximum(jnp.max(s, axis=1, keepdims=True), sink_col)
    p = jnp.exp2(s - m)
    ov = _dot(p.astype(BF16), v_win)
    return ov[:, :LANES] / (ov[:, LANES:] + jnp.exp2(sink_col - m))


def _layer_kernel(sink_ref, x_ref, xp_ref, xn_ref, gpre_ref, win_ref, wconv_ref, cos_ref, sin_ref, bias_ref,
                  mkt_ref, mvx_ref, wua_ref, wub_ref, wum_ref, wout_ref, gpost_ref,
                  out_ref, h_s, k_s, ksw_s, v_s, vsw_s, cu_s, mrg_s, gate_s):
    i = pl.program_id(1)
    row0 = pl.multiple_of(i * TS, TS)
    g_pre = gpre_ref[...]
    g_post = gpost_ref[...]

    lane = lax.broadcasted_iota(jnp.int32, (1, LANES), 1)
    low_head = lane < HEAD_DIM
    low8 = (lane % HEAD_DIM) < (ROT_DIM // 2)

    rows4 = lax.broadcasted_iota(jnp.int32, (4 * QB, 1), 0)
    def sink_rows(h0, h1, h2, h3):
        return LOG2E * jnp.where(rows4 < QB, sink_ref[h0],
                                 jnp.where(rows4 < 2 * QB, sink_ref[h1],
                                           jnp.where(rows4 < 3 * QB, sink_ref[h2], sink_ref[h3])))
    sink_a = sink_rows(0, 2, 5, 7)
    sink_b = sink_rows(1, 3, 4, 6)
    n_blocks = SEQ // QB
    ones = jnp.ones((SB, LANES), BF16)

    def norm_main(sb):
        r0 = HALO + sb * SB
        h_s[r0:r0 + SB] = _rms(x_ref[0, sb * SB:(sb + 1) * SB], g_pre).astype(BF16)

    def norm_prev_halo():
        h_s[0:HALO] = _rms(xp_ref[0], g_pre).astype(BF16)

    def norm_next_halo():
        h_s[HALO + TS:EXT] = _rms(xn_ref[0], g_pre).astype(BF16)

    def kv_chunk(c):
        r0 = c * SB
        kv = _dot(h_s[r0:r0 + SB], win_ref[:, C_BK:C_BV + KV_WIDTH])
        cos = cos_ref[pl.ds(row0 + r0, SB), :]
        ssin = sin_ref[pl.ds(row0 + r0, SB), :]
        k = _rope(kv[:, :KV_WIDTH], cos, ssin, low8)
        v = kv[:, KV_WIDTH:]
        k_s[r0:r0 + SB] = k.astype(BF16)
        ksw_s[r0:r0 + SB] = pltpu.roll(k, HEAD_DIM, 1).astype(BF16)
        v_s[r0:r0 + SB, 0:LANES] = v.astype(BF16)
        v_s[r0:r0 + SB, LANES:2 * LANES] = ones
        vsw_s[r0:r0 + SB, 0:LANES] = pltpu.roll(v, HEAD_DIM, 1).astype(BF16)
        vsw_s[r0:r0 + SB, LANES:2 * LANES] = ones

    def sub_block(sb):
        m0 = HALO + sb * SB
        rows = slice(sb * SB, (sb + 1) * SB)

        def gate(col, slot):
            gate_s[sb, slot] = _sigmoid(_dot(h, win_ref[:, col:col + D_MODEL]))

        def conv_input():
            hx = h_s[m0 - CONV_HALO:m0 + SB + CONV_HALO]
            cx = _dot(hx, win_ref[:, C_AC:C_AX + A_WIDTH])
            cu_s[sb] = cx[:, :A_WIDTH] * cx[:, A_WIDTH:]

        norm_main(sb)
        yield
        h = h_s[m0:m0 + SB]
        q = _dot(h, win_ref[:, C_BQ:C_BQ + ATTN_WIDTH])
        b_zs = _silu(_dot(h, win_ref[:, C_BZ:C_BZ + ATTN_WIDTH]))
        yield
        a_b = _dot(h, win_ref[:, C_AB:C_AB + A_WIDTH])
        a_zs = _silu(_dot(h, win_ref[:, C_AZ:C_AZ + A_WIDTH]))
        scale = (HEAD_DIM ** -0.5) * LOG2E
        qcos = cos_ref[pl.ds(row0 + m0, SB), :] * scale
        qsin = sin_ref[pl.ds(row0 + m0, SB), :] * scale
        q_lo, q_hi = [], []
        for mblk in range(ATTN_WIDTH // LANES):
            qr = _rope(q[:, mblk * LANES:(mblk + 1) * LANES], qcos, qsin, low8)
            q_lo.append(jnp.where(low_head, qr, 0.0).astype(BF16))
            q_hi.append(jnp.where(low_head, 0.0, qr).astype(BF16))
        yield
        fillers = [lambda: gate(C_G1, 1), conv_input]
        yb_blocks = []
        for j in range(SB // QB):
            qs = slice(j * QB, (j + 1) * QB)
            jb = sb * (SB // QB) + j
            blk = i * (TS // QB) + jb
            variant = jnp.where(blk == 0, 1, jnp.where(blk == n_blocks - 1, 2, 0))
            bias1 = bias_ref[variant]
            bias = jnp.concatenate([bias1, bias1, bias1, bias1], axis=0)
            lhs_a = jnp.concatenate([q_lo[0][qs], q_lo[1][qs], q_hi[2][qs], q_hi[3][qs]], axis=0)
            lhs_b = jnp.concatenate([q_hi[0][qs], q_hi[1][qs], q_lo[2][qs], q_lo[3][qs]], axis=0)
            ws = slice(jb * QB, jb * QB + 3 * QB)
            s_a = _dot_nt(lhs_a, k_s[ws]) + bias
            s_b = _dot_nt(lhs_b, ksw_s[ws]) + bias
            fillers[j % len(fillers)]()
            o_a = _window_softmax_pv(s_a, v_s[ws], sink_a)
            o_b = _window_softmax_pv(s_b, vsw_s[ws], sink_b)
            cols = [jnp.where(low_head, o_a[0:QB], o_b[0:QB]),
                    jnp.where(low_head, o_a[QB:2 * QB], o_b[QB:2 * QB]),
                    jnp.where(low_head, o_b[2 * QB:3 * QB], o_a[2 * QB:3 * QB]),
                    jnp.where(low_head, o_b[3 * QB:4 * QB], o_a[3 * QB:4 * QB])]
            yb_blocks.append(jnp.concatenate(cols, axis=1))
            yield
        yb = jnp.concatenate(yb_blocks, axis=0) * b_zs
        ub = _dot(yb.astype(BF16), wub_ref[...])
        mrg_s[sb] = gate_s[sb, 1] * ub
        gate(C_G0, 0)
        pos = row0 + sb * SB + lax.broadcasted_iota(jnp.int32, (SB, 1), 0)
        cu_prev = jnp.where(pos == 0, 0.0, cu_s[sb, CONV_HALO - 1:CONV_HALO - 1 + SB])
        cu_next = jnp.where(pos == SEQ - 1, 0.0, cu_s[sb, CONV_HALO + 1:CONV_HALO + 1 + SB])
        y = (cu_prev * wconv_ref[0:1] + cu_s[sb, CONV_HALO:CONV_HALO + SB] * wconv_ref[1:2]
             + cu_next * wconv_ref[2:3])
        ya = a_b * y * a_zs
        ua = _dot(ya.astype(BF16), wua_ref[...])
        mrg_s[sb] += gate_s[sb, 0] * ua
        yield
        mq = _dot(h, win_ref[:, C_MQ:C_MQ + MEM_WIDTH]) * ((MEM_HEAD_DIM ** -0.5) * LOG2E)
        scores = []
        for hh in range(MEM_HEADS):
            hs = slice(hh * MEM_HEAD_DIM, (hh + 1) * MEM_HEAD_DIM)
            scores.append(_dot(mq[:, hs].astype(BF16), mkt_ref[0, hs, :]))
        m_zs = _silu(_dot(h, win_ref[:, C_MZ:C_MZ + MEM_WIDTH]))
        gate(C_G2, 2)
        ym_heads = []
        for hh in range(MEM_HEADS):
            s = scores[hh]
            p = jnp.exp2(s - jnp.max(s, axis=1, keepdims=True))
            ov = _dot(p.astype(BF16), mvx_ref[0, hh])
            ym_heads.append(ov[:, :MEM_HEAD_DIM] / ov[:, MEM_HEAD_DIM:])
        ym = (jnp.concatenate(ym_heads, axis=1) * m_zs).astype(BF16)
        yield
        for r0 in range(0, SB, SB // 2):
            rs = slice(r0, r0 + SB // 2)
            um = _dot(ym[rs], wum_ref[...])
            merged = mrg_s[sb, rs] + gate_s[sb, 2, rs] * um
            o = _dot(merged.astype(BF16), wout_ref[...])
            t0 = sb * SB + r0
            out_ref[0, t0:t0 + SB // 2] = x_ref[0, t0:t0 + SB // 2] + _rms(o, g_post)
        yield

    n_phases = 8
    gens = [sub_block(sb) for sb in range(NSB)]
    def run(sb, count=1):
        for _ in range(count):
            next(gens[sb])

    run(0)
    run(0)
    norm_prev_halo()
    if NSB > 1:
        run(1)
    else:
        norm_next_halo()
    kv_chunk(0)
    kv_chunk(1)
    run(0, 5)
    for sb in range(1, NSB):
        run(sb)
        if sb + 1 < NSB:
            run(sb + 1)
        else:
            norm_next_halo()
        kv_chunk(sb + 1)
        run(sb)
        run(sb - 1)
        run(sb, 4)
    run(NSB - 1)
    assert n_phases == 8


def _resident(shape):
    return pl.BlockSpec(shape, lambda b, i: (0,) * len(shape), pipeline_mode=pl.Buffered(1))


def _mem_kv(mem, g_mem, w_mem_kv):
    bsz = mem.shape[0]
    return pl.pallas_call(
        _mem_kv_kernel,
        grid=(bsz,),
        in_specs=[pl.BlockSpec((1, MEM_LEN, D_MODEL), lambda b: (b, 0, 0)),
                  pl.BlockSpec((1, D_MODEL), lambda b: (0, 0)),
                  pl.BlockSpec((D_MODEL, 2 * MEM_WIDTH), lambda b: (0, 0))],
        out_specs=[pl.BlockSpec((1, MEM_WIDTH, MEM_LEN), lambda b: (b, 0, 0)),
                   pl.BlockSpec((1, MEM_HEADS, MEM_LEN, 2 * MEM_HEAD_DIM), lambda b: (b, 0, 0, 0))],
        out_shape=[jax.ShapeDtypeStruct((bsz, MEM_WIDTH, MEM_LEN), BF16),
                   jax.ShapeDtypeStruct((bsz, MEM_HEADS, MEM_LEN, 2 * MEM_HEAD_DIM), BF16)],
        compiler_params=pltpu.CompilerParams(dimension_semantics=("arbitrary",)),
        name="mem_kv",
    )(mem, g_mem.reshape(1, D_MODEL), w_mem_kv.astype(BF16))


def _layer(x, mkt, mvx, g_pre, w_in, w_conv, attn_sink, w_up_a, w_up_b, w_up_m, w_out, g_post, cos_t, sin_t, bias_t):
    bsz, s, d = x.shape
    assert (s, d) == (SEQ, D_MODEL) and s % TS == 0 and SB % QB == 0 and SB == 2 * HALO
    nb = TS // HALO
    in_specs = [
        pl.BlockSpec(memory_space=pltpu.SMEM),
        pl.BlockSpec((1, TS, d), lambda b, i: (b, i, 0)),
        pl.BlockSpec((1, HALO, d), lambda b, i: (b, jnp.maximum(i * nb - 1, 0), 0)),
        pl.BlockSpec((1, HALO, d), lambda b, i: (b, jnp.minimum((i + 1) * nb, s // HALO - 1), 0)),
        _resident((1, d)),
        _resident((d, IN_WIDTH)),
        _resident((CONV_WIDTH, A_WIDTH)),
        _resident((s + 2 * HALO, LANES)),
        _resident((s + 2 * HALO, LANES)),
        _resident((3, QB, 3 * QB)),
        pl.BlockSpec((1, MEM_WIDTH, MEM_LEN), lambda b, i: (b, 0, 0)),
        pl.BlockSpec((1, MEM_HEADS, MEM_LEN, 2 * MEM_HEAD_DIM), lambda b, i: (b, 0, 0, 0)),
        _resident((A_WIDTH, d)), _resident((ATTN_WIDTH, d)), _resident((MEM_WIDTH, d)),
        _resident((d, d)),
        _resident((1, d)),
    ]
    scratch = [
        pltpu.VMEM((EXT, d), BF16),
        pltpu.VMEM((EXT, KV_WIDTH), BF16),
        pltpu.VMEM((EXT, KV_WIDTH), BF16),
        pltpu.VMEM((EXT, 2 * KV_WIDTH), BF16),
        pltpu.VMEM((EXT, 2 * KV_WIDTH), BF16),
        pltpu.VMEM((NSB, SB + 2 * CONV_HALO, A_WIDTH), F32),
        pltpu.VMEM((NSB, SB, d), F32),
        pltpu.VMEM((NSB, 3, SB, d), F32),
    ]
    return pl.pallas_call(
        _layer_kernel,
        grid=(bsz, s // TS),
        in_specs=in_specs,
        out_specs=pl.BlockSpec((1, TS, d), lambda b, i: (b, i, 0)),
        out_shape=jax.ShapeDtypeStruct(x.shape, x.dtype),
        scratch_shapes=scratch,
        compiler_params=pltpu.CompilerParams(dimension_semantics=("arbitrary", "arbitrary"),
                                             vmem_limit_bytes=VMEM_LIMIT_BYTES),
        name="hybrid_layer",
    )(attn_sink, x, x, x, g_pre.reshape(1, d), w_in.astype(BF16), w_conv, cos_t, sin_t, bias_t, mkt, mvx,
      w_up_a.astype(BF16), w_up_b.astype(BF16), w_up_m.astype(BF16), w_out.astype(BF16), g_post.reshape(1, d))


def kernel(x, mem, g_pre, w_in, w_conv, attn_sink, g_mem, w_mem_kv, w_up_a, w_up_b, w_up_m, w_out, g_post):
    cos_np, sin_np = _rope_tables()
    cos_t, sin_t, bias_t = jnp.asarray(cos_np), jnp.asarray(sin_np), jnp.asarray(_band_bias())
    for l in range(g_pre.shape[0]):
        mkt, mvx = _mem_kv(mem, g_mem[l], w_mem_kv[l])
        x = _layer(x, mkt, mvx, g_pre[l], w_in[l], w_conv[l], attn_sink[l], w_up_a[l], w_up_b[l], w_up_m[l],
                   w_out[l], g_post[l], cos_t, sin_t, bias_t)
    return x
```

```python
import math

import numpy as np
import jax
import jax.numpy as jnp
from jax import lax
from jax.experimental import pallas as pl
from jax.experimental.pallas import tpu as pltpu

F32 = jnp.float32
BF16 = jnp.bfloat16

D_MODEL = 1024
SEQ = 4096
MEM_LEN = 256
EPS = 1e-6
CONV_WIDTH = 3
A_WIDTH = 512
HEAD_DIM = 64
ATTN_WIDTH = 512
N_Q_HEADS = 8
KV_WIDTH = 128
WINDOW = 128
ROPE_THETA = 500000.0
ROT_DIM = 16
MEM_HEADS = 4
MEM_HEAD_DIM = 128
MEM_WIDTH = 512

C_AB, C_AC, C_AX, C_AZ = 0, 512, 1024, 1536
C_BQ, C_BK, C_BV, C_BZ = 2048, 2560, 2688, 2816
C_MQ, C_MZ = 3328, 3840
C_G0, C_G1, C_G2 = 4352, 5376, 6400
IN_WIDTH = 7424

LANES = 128
BF16_ROWS = 16
VMEM_LIMIT_BYTES = 56 * 1024 * 1024

SB = 256
NSB = 2
TS = SB * NSB
QB = WINDOW
HALO = WINDOW
CONV_HALO = BF16_ROWS
EXT = TS + 2 * HALO
STAGE_SLOTS = 3
STAGE_IN_ROWS = 32

LOG2E = math.log2(math.e)


def _sigmoid(v):
    return 1.0 / (1.0 + jnp.exp(-v))


def _silu(v):
    return v * _sigmoid(v)


def _rms(v, g):
    ms = jnp.mean(v * v, axis=-1, keepdims=True)
    return v * lax.rsqrt(ms + EPS) * g


def _dot(a, b):
    return jnp.dot(a, b, preferred_element_type=F32)


def _dot_nt(a, b):
    return lax.dot_general(a, b, (((1,), (1,)), ((), ())), preferred_element_type=F32)


def _rope(t, cos, ssin, low8):
    partner = jnp.where(low8, pltpu.roll(t, LANES - ROT_DIM // 2, 1), pltpu.roll(t, ROT_DIM // 2, 1))
    return t * cos + partner * ssin


def _rope_tables():
    half = ROT_DIM // 2
    inv_freq = np.power(np.float32(ROPE_THETA), -np.arange(half, dtype=np.float32) * np.float32(2.0 / ROT_DIM))
    pos = (np.arange(SEQ + 2 * HALO) - HALO).astype(np.float32)
    ang = (pos[:, None] * inv_freq[None, :]).astype(np.float32)
    cos, sin = np.cos(ang).astype(np.float32), np.sin(ang).astype(np.float32)
    ct = np.ones((SEQ + 2 * HALO, HEAD_DIM), np.float32)
    st = np.zeros((SEQ + 2 * HALO, HEAD_DIM), np.float32)
    ct[:, :half], ct[:, half:ROT_DIM] = cos, cos
    st[:, :half], st[:, half:ROT_DIM] = -sin, sin
    return np.tile(ct, (1, LANES // HEAD_DIM)), np.tile(st, (1, LANES // HEAD_DIM))


def _band_bias():
    r = np.arange(QB)[:, None]
    c = np.arange(3 * QB)[None, :]
    band = (c >= r) & (c <= r + 2 * WINDOW)
    masks = [band, band & (c >= QB), band & (c < 2 * QB)]
    return np.stack([np.where(m, 0.0, -np.inf) for m in masks]).astype(np.float32)


def _stream_cast(jobs, stage, sem, rows):
    chunks = [(src, dst, r0) for src, dst in jobs for r0 in range(0, src.shape[0], rows)]

    def copy(n):
        src, _, r0 = chunks[n]
        slot = n % STAGE_SLOTS
        return pltpu.make_async_copy(src.at[pl.ds(r0, rows)], stage(slot), sem.at[slot])

    for n in range(min(STAGE_SLOTS, len(chunks))):
        copy(n).start()
    for n, (_, dst, r0) in enumerate(chunks):
        copy(n).wait()
        dst[r0:r0 + rows] = stage(n % STAGE_SLOTS)[...].astype(BF16)
        if n + STAGE_SLOTS < len(chunks):
            copy(n + STAGE_SLOTS).start()


def _window_softmax_pv(s, v_win, sink_col):
    m = jnp.maximum(jnp.max(s, axis=1, keepdims=True), sink_col)
    p = jnp.exp2(s - m)
    ov = _dot(p.astype(BF16), v_win)
    return ov[:, :LANES] / (ov[:, LANES:] + jnp.exp2(sink_col - m))


def _layer_kernel(sink_ref, x_ref, xp_ref, xn_ref, gpre_ref, win_hbm, wconv_ref, cos_ref, sin_ref, bias_ref,
                  mem_ref, gmem_ref, wmkv_hbm, wua_hbm, wub_hbm, wum_hbm, wout_hbm, gpost_ref,
                  out_ref, h_s, k_s, ksw_s, v_s, vsw_s, cu_s, mrg_s, gate_s,
                  win_ref, wmkv_ref, wua_ref, wub_ref, wum_ref, wout_ref, mkt_ref, mvx_ref,
                  stage_in, sem_in, sem_sq):
    b = pl.program_id(0)
    i = pl.program_id(1)
    row0 = pl.multiple_of(i * TS, TS)
    g_pre = gpre_ref[...]
    g_post = gpost_ref[...]

    @pl.when((b == 0) & (i == 0))
    def _load_weights():
        _stream_cast([(win_hbm, win_ref)], lambda s: stage_in.at[s], sem_in, STAGE_IN_ROWS)
        _stream_cast([(wmkv_hbm, wmkv_ref), (wua_hbm, wua_ref), (wub_hbm, wub_ref), (wum_hbm, wum_ref),
                      (wout_hbm, wout_ref)], lambda s: gate_s.at[s], sem_sq, SB)

    @pl.when(i == 0)
    def _memory_kv():
        mn = _rms(mem_ref[0], gmem_ref[...]).astype(BF16)
        kv = _dot(mn, wmkv_ref[...])
        mkt_ref[...] = kv[:, :MEM_WIDTH].T.astype(BF16)
        ones_m = jnp.ones((MEM_LEN, MEM_HEAD_DIM), F32)
        for hh in range(MEM_HEADS):
            c0 = MEM_WIDTH + hh * MEM_HEAD_DIM
            mvx_ref[hh] = jnp.concatenate([kv[:, c0:c0 + MEM_HEAD_DIM], ones_m], axis=1).astype(BF16)

    lane = lax.broadcasted_iota(jnp.int32, (1, LANES), 1)
    low_head = lane < HEAD_DIM
    low8 = (lane % HEAD_DIM) < (ROT_DIM // 2)

    rows4 = lax.broadcasted_iota(jnp.int32, (4 * QB, 1), 0)
    def sink_rows(h0, h1, h2, h3):
        return LOG2E * jnp.where(rows4 < QB, sink_ref[h0],
                                 jnp.where(rows4 < 2 * QB, sink_ref[h1],
                                           jnp.where(rows4 < 3 * QB, sink_ref[h2], sink_ref[h3])))
    sink_a = sink_rows(0, 2, 5, 7)
    sink_b = sink_rows(1, 3, 4, 6)
    n_blocks = SEQ // QB
    ones = jnp.ones((SB, LANES), BF16)

    def norm_main(sb):
        r0 = HALO + sb * SB
        h_s[r0:r0 + SB] = _rms(x_ref[0, sb * SB:(sb + 1) * SB], g_pre).astype(BF16)

    def norm_prev_halo():
        h_s[0:HALO] = _rms(xp_ref[0], g_pre).astype(BF16)

    def norm_next_halo():
        h_s[HALO + TS:EXT] = _rms(xn_ref[0], g_pre).astype(BF16)

    def kv_chunk(c):
        r0 = c * SB
        kv = _dot(h_s[r0:r0 + SB], win_ref[:, C_BK:C_BV + KV_WIDTH])
        cos = cos_ref[pl.ds(row0 + r0, SB), :]
        ssin = sin_ref[pl.ds(row0 + r0, SB), :]
        k = _rope(kv[:, :KV_WIDTH], cos, ssin, low8)
        v = kv[:, KV_WIDTH:]
        k_s[r0:r0 + SB] = k.astype(BF16)
        ksw_s[r0:r0 + SB] = pltpu.roll(k, HEAD_DIM, 1).astype(BF16)
        v_s[r0:r0 + SB, 0:LANES] = v.astype(BF16)
        v_s[r0:r0 + SB, LANES:2 * LANES] = ones
        vsw_s[r0:r0 + SB, 0:LANES] = pltpu.roll(v, HEAD_DIM, 1).astype(BF16)
        vsw_s[r0:r0 + SB, LANES:2 * LANES] = ones

    def sub_block(sb):
        m0 = HALO + sb * SB
        rows = slice(sb * SB, (sb + 1) * SB)

        def gate(col, slot):
            gate_s[slot] = _sigmoid(_dot(h, win_ref[:, col:col + D_MODEL]))

        def conv_input():
            hx = h_s[m0 - CONV_HALO:m0 + SB + CONV_HALO]
            cx = _dot(hx, win_ref[:, C_AC:C_AX + A_WIDTH])
            cu_s[...] = cx[:, :A_WIDTH] * cx[:, A_WIDTH:]

        norm_main(sb)
        yield
        h = h_s[m0:m0 + SB]
        q = _dot(h, win_ref[:, C_BQ:C_BQ + ATTN_WIDTH])
        b_zs = _silu(_dot(h, win_ref[:, C_BZ:C_BZ + ATTN_WIDTH]))
        yield
        a_b = _dot(h, win_ref[:, C_AB:C_AB + A_WIDTH])
        a_zs = _silu(_dot(h, win_ref[:, C_AZ:C_AZ + A_WIDTH]))
        scale = (HEAD_DIM ** -0.5) * LOG2E
        qcos = cos_ref[pl.ds(row0 + m0, SB), :] * scale
        qsin = sin_ref[pl.ds(row0 + m0, SB), :] * scale
        q_lo, q_hi = [], []
        for mblk in range(ATTN_WIDTH // LANES):
            qr = _rope(q[:, mblk * LANES:(mblk + 1) * LANES], qcos, qsin, low8)
            q_lo.append(jnp.where(low_head, qr, 0.0).astype(BF16))
            q_hi.append(jnp.where(low_head, 0.0, qr).astype(BF16))
        yield
        fillers = [lambda: gate(C_G1, 1), conv_input]
        yb_blocks = []
        for j in range(SB // QB):
            qs = slice(j * QB, (j + 1) * QB)
            jb = sb * (SB // QB) + j
            blk = i * (TS // QB) + jb
            variant = jnp.where(blk == 0, 1, jnp.where(blk == n_blocks - 1, 2, 0))
            bias1 = bias_ref[variant]
            bias = jnp.concatenate([bias1, bias1, bias1, bias1], axis=0)
            lhs_a = jnp.concatenate([q_lo[0][qs], q_lo[1][qs], q_hi[2][qs], q_hi[3][qs]], axis=0)
            lhs_b = jnp.concatenate([q_hi[0][qs], q_hi[1][qs], q_lo[2][qs], q_lo[3][qs]], axis=0)
            ws = slice(jb * QB, jb * QB + 3 * QB)
            s_a = _dot_nt(lhs_a, k_s[ws]) + bias
            s_b = _dot_nt(lhs_b, ksw_s[ws]) + bias
            fillers[j % len(fillers)]()
            o_a = _window_softmax_pv(s_a, v_s[ws], sink_a)
            o_b = _window_softmax_pv(s_b, vsw_s[ws], sink_b)
            cols = [jnp.where(low_head, o_a[0:QB], o_b[0:QB]),
                    jnp.where(low_head, o_a[QB:2 * QB], o_b[QB:2 * QB]),
                    jnp.where(low_head, o_b[2 * QB:3 * QB], o_a[2 * QB:3 * QB]),
                    jnp.where(low_head, o_b[3 * QB:4 * QB], o_a[3 * QB:4 * QB])]
            yb_blocks.append(jnp.concatenate(cols, axis=1))
            yield
        yb = jnp.concatenate(yb_blocks, axis=0) * b_zs
        ub = _dot(yb.astype(BF16), wub_ref[...])
        mrg_s[...] = gate_s[1] * ub
        gate(C_G0, 0)
        pos = row0 + sb * SB + lax.broadcasted_iota(jnp.int32, (SB, 1), 0)
        cu_prev = jnp.where(pos == 0, 0.0, cu_s[CONV_HALO - 1:CONV_HALO - 1 + SB])
        cu_next = jnp.where(pos == SEQ - 1, 0.0, cu_s[CONV_HALO + 1:CONV_HALO + 1 + SB])
        y = (cu_prev * wconv_ref[0:1] + cu_s[CONV_HALO:CONV_HALO + SB] * wconv_ref[1:2]
             + cu_next * wconv_ref[2:3])
        ya = a_b * y * a_zs
        ua = _dot(ya.astype(BF16), wua_ref[...])
        mrg_s[...] += gate_s[0] * ua
        yield
        mq = _dot(h, win_ref[:, C_MQ:C_MQ + MEM_WIDTH]) * ((MEM_HEAD_DIM ** -0.5) * LOG2E)
        scores = []
        for hh in range(MEM_HEADS):
            hs = slice(hh * MEM_HEAD_DIM, (hh + 1) * MEM_HEAD_DIM)
            scores.append(_dot(mq[:, hs].astype(BF16), mkt_ref[hs, :]))
        m_zs = _silu(_dot(h, win_ref[:, C_MZ:C_MZ + MEM_WIDTH]))
        gate(C_G2, 2)
        ym_heads = []
        for hh in range(MEM_HEADS):
            s = scores[hh]
            p = jnp.exp2(s - jnp.max(s, axis=1, keepdims=True))
            ov = _dot(p.astype(BF16), mvx_ref[hh])
            ym_heads.append(ov[:, :MEM_HEAD_DIM] / ov[:, MEM_HEAD_DIM:])
        ym = (jnp.concatenate(ym_heads, axis=1) * m_zs).astype(BF16)
        yield
        for r0 in range(0, SB, SB // 2):
            rs = slice(r0, r0 + SB // 2)
            um = _dot(ym[rs], wum_ref[...])
            merged = mrg_s[rs] + gate_s[2, rs] * um
            o = _dot(merged.astype(BF16), wout_ref[...])
            t0 = sb * SB + r0
            out_ref[0, t0:t0 + SB // 2] = x_ref[0, t0:t0 + SB // 2] + _rms(o, g_post)
        yield

    gens = [sub_block(sb) for sb in range(NSB)]
    def run(sb, count=1):
        for _ in range(count):
            next(gens[sb])

    run(0)
    run(0)
    norm_prev_halo()
    if NSB > 1:
        run(1)
    else:
        norm_next_halo()
    kv_chunk(0)
    kv_chunk(1)
    run(0, 5)
    for sb in range(1, NSB):
        run(sb)
        if sb + 1 < NSB:
            run(sb + 1)
        else:
            norm_next_halo()
        kv_chunk(sb + 1)
        run(sb)
        run(sb - 1)
        run(sb, 4)
    run(NSB - 1)


def _resident(shape):
    return pl.BlockSpec(shape, lambda b, i: (0,) * len(shape), pipeline_mode=pl.Buffered(1))


def _layer(x, mem, g_pre, w_in, w_conv, attn_sink, g_mem, w_mem_kv, w_up_a, w_up_b, w_up_m, w_out, g_post,
           cos_t, sin_t, bias_t):
    bsz, s, d = x.shape
    assert (s, d) == (SEQ, D_MODEL) and s % TS == 0 and SB % QB == 0 and SB == 2 * HALO
    assert d % STAGE_IN_ROWS == 0 and STAGE_SLOTS <= 3
    nb = TS // HALO
    hbm = pl.BlockSpec(memory_space=pl.ANY)
    in_specs = [
        pl.BlockSpec(memory_space=pltpu.SMEM),
        pl.BlockSpec((1, TS, d), lambda b, i: (b, i, 0)),
        pl.BlockSpec((1, HALO, d), lambda b, i: (b, jnp.maximum(i * nb - 1, 0), 0)),
        pl.BlockSpec((1, HALO, d), lambda b, i: (b, jnp.minimum((i + 1) * nb, s // HALO - 1), 0)),
        _resident((1, d)),
        hbm,
        _resident((CONV_WIDTH, A_WIDTH)),
        _resident((s + 2 * HALO, LANES)),
        _resident((s + 2 * HALO, LANES)),
        _resident((3, QB, 3 * QB)),
        pl.BlockSpec((1, MEM_LEN, d), lambda b, i: (b, 0, 0)),
        _resident((1, d)),
        hbm, hbm, hbm, hbm, hbm,
        _resident((1, d)),
    ]
    scratch = [
        pltpu.VMEM((EXT, d), BF16),
        pltpu.VMEM((EXT, KV_WIDTH), BF16),
        pltpu.VMEM((EXT, KV_WIDTH), BF16),
        pltpu.VMEM((EXT, 2 * KV_WIDTH), BF16),
        pltpu.VMEM((EXT, 2 * KV_WIDTH), BF16),
        pltpu.VMEM((SB + 2 * CONV_HALO, A_WIDTH), F32),
        pltpu.VMEM((SB, d), F32),
        pltpu.VMEM((3, SB, d), F32),
        pltpu.VMEM((d, IN_WIDTH), BF16),
        pltpu.VMEM((d, 2 * MEM_WIDTH), BF16),
        pltpu.VMEM((A_WIDTH, d), BF16),
        pltpu.VMEM((ATTN_WIDTH, d), BF16),
        pltpu.VMEM((MEM_WIDTH, d), BF16),
        pltpu.VMEM((d, d), BF16),
        pltpu.VMEM((MEM_WIDTH, MEM_LEN), BF16),
        pltpu.VMEM((MEM_HEADS, MEM_LEN, 2 * MEM_HEAD_DIM), BF16),
        pltpu.VMEM((STAGE_SLOTS, STAGE_IN_ROWS, IN_WIDTH), F32),
        pltpu.SemaphoreType.DMA((STAGE_SLOTS,)),
        pltpu.SemaphoreType.DMA((STAGE_SLOTS,)),
    ]
    return pl.pallas_call(
        _layer_kernel,
        grid=(bsz, s // TS),
        in_specs=in_specs,
        out_specs=pl.BlockSpec((1, TS, d), lambda b, i: (b, i, 0)),
        out_shape=jax.ShapeDtypeStruct(x.shape, x.dtype),
        scratch_shapes=scratch,
        compiler_params=pltpu.CompilerParams(dimension_semantics=("arbitrary", "arbitrary"),
                                             vmem_limit_bytes=VMEM_LIMIT_BYTES),
        name="hybrid_layer",
    )(attn_sink, x, x, x, g_pre.reshape(1, d), w_in, w_conv, cos_t, sin_t, bias_t, mem, g_mem.reshape(1, d),
      w_mem_kv, w_up_a, w_up_b, w_up_m, w_out, g_post.reshape(1, d))


def kernel(x, mem, g_pre, w_in, w_conv, attn_sink, g_mem, w_mem_kv, w_up_a, w_up_b, w_up_m, w_out, g_post):
    cos_np, sin_np = _rope_tables()
    cos_t, sin_t, bias_t = jnp.asarray(cos_np), jnp.asarray(sin_np), jnp.asarray(_band_bias())
    for l in range(g_pre.shape[0]):
        x = _layer(x, mem, g_pre[l], w_in[l], w_conv[l], attn_sink[l], g_mem[l], w_mem_kv[l], w_up_a[l], w_up_b[l],
                   w_up_m[l], w_out[l], g_post[l], cos_t, sin_t, bias_t)
    return x
```

```python
import math

import numpy as np
import jax
import jax.numpy as jnp
from jax import lax
from jax.experimental import pallas as pl
from jax.experimental.pallas import tpu as pltpu

F32 = jnp.float32
BF16 = jnp.bfloat16

D_MODEL = 1024
SEQ = 4096
MEM_LEN = 256
EPS = 1e-6
CONV_WIDTH = 3
A_WIDTH = 512
HEAD_DIM = 64
ATTN_WIDTH = 512
N_Q_HEADS = 8
KV_WIDTH = 128
WINDOW = 128
ROPE_THETA = 500000.0
ROT_DIM = 16
MEM_HEADS = 4
MEM_HEAD_DIM = 128
MEM_WIDTH = 512

C_AB, C_AC, C_AX, C_AZ = 0, 512, 1024, 1536
C_BQ, C_BK, C_BV, C_BZ = 2048, 2560, 2688, 2816
C_MQ, C_MZ = 3328, 3840
C_G0, C_G1, C_G2 = 4352, 5376, 6400
IN_WIDTH = 7424

LANES = 128
BF16_ROWS = 16
VMEM_LIMIT_BYTES = 60 * 1024 * 1024

SB = 256
NSB = 2
TS = SB * NSB
QB = WINDOW
HALO = WINDOW
CONV_HALO = BF16_ROWS
EXT = TS + 2 * HALO
STAGE_SLOTS = 4
STAGE_ROWS = SB
STAGE_COLS = D_MODEL

LOG2E = math.log2(math.e)


def _sigmoid(v):
    return 1.0 / (1.0 + jnp.exp(-v))


def _silu(v):
    return v * _sigmoid(v)


def _rms(v, g):
    ms = jnp.mean(v * v, axis=-1, keepdims=True)
    return v * lax.rsqrt(ms + EPS) * g


def _dot(a, b):
    return jnp.dot(a, b, preferred_element_type=F32)


def _dot_nt(a, b):
    return lax.dot_general(a, b, (((1,), (1,)), ((), ())), preferred_element_type=F32)


def _rope(t, cos, ssin, low8):
    partner = jnp.where(low8, pltpu.roll(t, LANES - ROT_DIM // 2, 1), pltpu.roll(t, ROT_DIM // 2, 1))
    return t * cos + partner * ssin


def _rope_tables():
    half = ROT_DIM // 2
    inv_freq = np.power(np.float32(ROPE_THETA), -np.arange(half, dtype=np.float32) * np.float32(2.0 / ROT_DIM))
    pos = (np.arange(SEQ + 2 * HALO) - HALO).astype(np.float32)
    ang = (pos[:, None] * inv_freq[None, :]).astype(np.float32)
    cos, sin = np.cos(ang).astype(np.float32), np.sin(ang).astype(np.float32)
    ct = np.ones((SEQ + 2 * HALO, HEAD_DIM), np.float32)
    st = np.zeros((SEQ + 2 * HALO, HEAD_DIM), np.float32)
    ct[:, :half], ct[:, half:ROT_DIM] = cos, cos
    st[:, :half], st[:, half:ROT_DIM] = -sin, sin
    return np.tile(ct, (1, LANES // HEAD_DIM)), np.tile(st, (1, LANES // HEAD_DIM))


def _band_bias():
    r = np.arange(QB)[:, None]
    c = np.arange(3 * QB)[None, :]
    band = (c >= r) & (c <= r + 2 * WINDOW)
    masks = [band, band & (c >= QB), band & (c < 2 * QB)]
    return np.stack([np.where(m, 0.0, -np.inf) for m in masks]).astype(np.float32)


def _stream_cast(jobs, slots, sem):
    chunks = [(src, dst, r0, c0, min(STAGE_COLS, src.shape[1] - c0))
              for src, dst in jobs
              for r0 in range(0, src.shape[0], STAGE_ROWS)
              for c0 in range(0, src.shape[1], STAGE_COLS)]
    n_slots = len(slots)

    def copy(n):
        src, _, r0, c0, cols = chunks[n]
        k = n % n_slots
        return pltpu.make_async_copy(src.at[pl.ds(r0, STAGE_ROWS), pl.ds(c0, cols)],
                                     slots[k].at[:, pl.ds(0, cols)], sem.at[k])

    for n in range(min(n_slots, len(chunks))):
        copy(n).start()
    for n, (_, dst, r0, c0, cols) in enumerate(chunks):
        copy(n).wait()
        dst[r0:r0 + STAGE_ROWS, c0:c0 + cols] = slots[n % n_slots][:, 0:cols].astype(BF16)
        if n + n_slots < len(chunks):
            copy(n + n_slots).start()


def _window_softmax_pv(s, v_win, sink_col):
    m = jnp.maximum(jnp.max(s, axis=1, keepdims=True), sink_col)
    p = jnp.exp2(s - m)
    ov = _dot(p.astype(BF16), v_win)
    return ov[:, :LANES] / (ov[:, LANES:] + jnp.exp2(sink_col - m))


def _layer_kernel(sink_ref, x_ref, xp_ref, xn_ref, gpre_ref, win_hbm, wconv_ref, cos_ref, sin_ref, bias_ref,
                  mem_ref, gmem_ref, wmkv_hbm, wua_hbm, wub_hbm, wum_hbm, wout_hbm, gpost_ref,
                  out_ref, h_s, k_s, ksw_s, v_s, vsw_s, cu_s, mrg_s, gate_s,
                  win_ref, wmkv_ref, wua_ref, wub_ref, wum_ref, wout_ref, mkt_ref, mvx_ref,
                  stage_s, stage_sem):
    b = pl.program_id(0)
    i = pl.program_id(1)
    row0 = pl.multiple_of(i * TS, TS)
    g_pre = gpre_ref[...]
    g_post = gpost_ref[...]

    @pl.when((b == 0) & (i == 0))
    def _load_weights():
        slots = ([stage_s.at[k] for k in range(STAGE_SLOTS)] + [gate_s.at[k] for k in range(3)] + [mrg_s])
        _stream_cast([(win_hbm, win_ref), (wmkv_hbm, wmkv_ref), (wua_hbm, wua_ref), (wub_hbm, wub_ref),
                      (wum_hbm, wum_ref), (wout_hbm, wout_ref)], slots, stage_sem)

    @pl.when(i == 0)
    def _memory_kv():
        mn = _rms(mem_ref[0], gmem_ref[...]).astype(BF16)
        kv = _dot(mn, wmkv_ref[...])
        mkt_ref[...] = kv[:, :MEM_WIDTH].T.astype(BF16)
        ones_m = jnp.ones((MEM_LEN, MEM_HEAD_DIM), F32)
        for hh in range(MEM_HEADS):
            c0 = MEM_WIDTH + hh * MEM_HEAD_DIM
            mvx_ref[hh] = jnp.concatenate([kv[:, c0:c0 + MEM_HEAD_DIM], ones_m], axis=1).astype(BF16)

    lane = lax.broadcasted_iota(jnp.int32, (1, LANES), 1)
    low_head = lane < HEAD_DIM
    low8 = (lane % HEAD_DIM) < (ROT_DIM // 2)

    rows4 = lax.broadcasted_iota(jnp.int32, (4 * QB, 1), 0)
    def sink_rows(h0, h1, h2, h3):
        return LOG2E * jnp.where(rows4 < QB, sink_ref[h0],
                                 jnp.where(rows4 < 2 * QB, sink_ref[h1],
                                           jnp.where(rows4 < 3 * QB, sink_ref[h2], sink_ref[h3])))
    sink_a = sink_rows(0, 2, 5, 7)
    sink_b = sink_rows(1, 3, 4, 6)
    n_blocks = SEQ // QB
    ones = jnp.ones((SB, LANES), BF16)

    def norm_main(sb):
        r0 = HALO + sb * SB
        h_s[r0:r0 + SB] = _rms(x_ref[0, sb * SB:(sb + 1) * SB], g_pre).astype(BF16)

    def norm_prev_halo():
        h_s[0:HALO] = _rms(xp_ref[0], g_pre).astype(BF16)

    def norm_next_halo():
        h_s[HALO + TS:EXT] = _rms(xn_ref[0], g_pre).astype(BF16)

    def kv_chunk(c):
        r0 = c * SB
        kv = _dot(h_s[r0:r0 + SB], win_ref[:, C_BK:C_BV + KV_WIDTH])
        cos = cos_ref[pl.ds(row0 + r0, SB), :]
        ssin = sin_ref[pl.ds(row0 + r0, SB), :]
        k = _rope(kv[:, :KV_WIDTH], cos, ssin, low8)
        v = kv[:, KV_WIDTH:]
        k_s[r0:r0 + SB] = k.astype(BF16)
        ksw_s[r0:r0 + SB] = pltpu.roll(k, HEAD_DIM, 1).astype(BF16)
        v_s[r0:r0 + SB, 0:LANES] = v.astype(BF16)
        v_s[r0:r0 + SB, LANES:2 * LANES] = ones
        vsw_s[r0:r0 + SB, 0:LANES] = pltpu.roll(v, HEAD_DIM, 1).astype(BF16)
        vsw_s[r0:r0 + SB, LANES:2 * LANES] = ones

    def sub_block(sb):
        m0 = HALO + sb * SB
        rows = slice(sb * SB, (sb + 1) * SB)

        def gate(col, slot):
            gate_s[slot] = _sigmoid(_dot(h, win_ref[:, col:col + D_MODEL]))

        def conv_input():
            hx = h_s[m0 - CONV_HALO:m0 + SB + CONV_HALO]
            cx = _dot(hx, win_ref[:, C_AC:C_AX + A_WIDTH])
            cu_s[...] = cx[:, :A_WIDTH] * cx[:, A_WIDTH:]

        norm_main(sb)
        yield
        h = h_s[m0:m0 + SB]
        q = _dot(h, win_ref[:, C_BQ:C_BQ + ATTN_WIDTH])
        b_zs = _silu(_dot(h, win_ref[:, C_BZ:C_BZ + ATTN_WIDTH]))
        yield
        a_b = _dot(h, win_ref[:, C_AB:C_AB + A_WIDTH])
        a_zs = _silu(_dot(h, win_ref[:, C_AZ:C_AZ + A_WIDTH]))
        scale = (HEAD_DIM ** -0.5) * LOG2E
        qcos = cos_ref[pl.ds(row0 + m0, SB), :] * scale
        qsin = sin_ref[pl.ds(row0 + m0, SB), :] * scale
        q_lo, q_hi = [], []
        for mblk in range(ATTN_WIDTH // LANES):
            qr = _rope(q[:, mblk * LANES:(mblk + 1) * LANES], qcos, qsin, low8)
            q_lo.append(jnp.where(low_head, qr, 0.0).astype(BF16))
            q_hi.append(jnp.where(low_head, 0.0, qr).astype(BF16))
        yield
        fillers = [lambda: gate(C_G1, 1), conv_input]
        yb_blocks = []
        for j in range(SB // QB):
            qs = slice(j * QB, (j + 1) * QB)
            jb = sb * (SB // QB) + j
            blk = i * (TS // QB) + jb
            variant = jnp.where(blk == 0, 1, jnp.where(blk == n_blocks - 1, 2, 0))
            bias1 = bias_ref[variant]
            bias = jnp.concatenate([bias1, bias1, bias1, bias1], axis=0)
            lhs_a = jnp.concatenate([q_lo[0][qs], q_lo[1][qs], q_hi[2][qs], q_hi[3][qs]], axis=0)
            lhs_b = jnp.concatenate([q_hi[0][qs], q_hi[1][qs], q_lo[2][qs], q_lo[3][qs]], axis=0)
            ws = slice(jb * QB, jb * QB + 3 * QB)
            s_a = _dot_nt(lhs_a, k_s[ws]) + bias
            s_b = _dot_nt(lhs_b, ksw_s[ws]) + bias
            fillers[j % len(fillers)]()
            o_a = _window_softmax_pv(s_a, v_s[ws], sink_a)
            o_b = _window_softmax_pv(s_b, vsw_s[ws], sink_b)
            cols = [jnp.where(low_head, o_a[0:QB], o_b[0:QB]),
                    jnp.where(low_head, o_a[QB:2 * QB], o_b[QB:2 * QB]),
                    jnp.where(low_head, o_b[2 * QB:3 * QB], o_a[2 * QB:3 * QB]),
                    jnp.where(low_head, o_b[3 * QB:4 * QB], o_a[3 * QB:4 * QB])]
            yb_blocks.append(jnp.concatenate(cols, axis=1))
            yield
        yb = jnp.concatenate(yb_blocks, axis=0) * b_zs
        ub = _dot(yb.astype(BF16), wub_ref[...])
        mrg_s[...] = gate_s[1] * ub
        gate(C_G0, 0)
        pos = row0 + sb * SB + lax.broadcasted_iota(jnp.int32, (SB, 1), 0)
        cu_prev = jnp.where(pos == 0, 0.0, cu_s[CONV_HALO - 1:CONV_HALO - 1 + SB])
        cu_next = jnp.where(pos == SEQ - 1, 0.0, cu_s[CONV_HALO + 1:CONV_HALO + 1 + SB])
        y = (cu_prev * wconv_ref[0:1] + cu_s[CONV_HALO:CONV_HALO + SB] * wconv_ref[1:2]
             + cu_next * wconv_ref[2:3])
        ya = a_b * y * a_zs
        ua = _dot(ya.astype(BF16), wua_ref[...])
        mrg_s[...] += gate_s[0] * ua
        yield
        mq = _dot(h, win_ref[:, C_MQ:C_MQ + MEM_WIDTH]) * ((MEM_HEAD_DIM ** -0.5) * LOG2E)
        scores = []
        for hh in range(MEM_HEADS):
            hs = slice(hh * MEM_HEAD_DIM, (hh + 1) * MEM_HEAD_DIM)
            scores.append(_dot(mq[:, hs].astype(BF16), mkt_ref[hs, :]))
        m_zs = _silu(_dot(h, win_ref[:, C_MZ:C_MZ + MEM_WIDTH]))
        gate(C_G2, 2)
        ym_heads = []
        for hh in range(MEM_HEADS):
            s = scores[hh]
            p = jnp.exp2(s - jnp.max(s, axis=1, keepdims=True))
            ov = _dot(p.astype(BF16), mvx_ref[hh])
            ym_heads.append(ov[:, :MEM_HEAD_DIM] / ov[:, MEM_HEAD_DIM:])
        ym = (jnp.concatenate(ym_heads, axis=1) * m_zs).astype(BF16)
        yield
        for r0 in range(0, SB, SB // 2):
            rs = slice(r0, r0 + SB // 2)
            um = _dot(ym[rs], wum_ref[...])
            merged = mrg_s[rs] + gate_s[2, rs] * um
            o = _dot(merged.astype(BF16), wout_ref[...])
            t0 = sb * SB + r0
            out_ref[0, t0:t0 + SB // 2] = x_ref[0, t0:t0 + SB // 2] + _rms(o, g_post)
        yield

    gens = [sub_block(sb) for sb in range(NSB)]
    def run(sb, count=1):
        for _ in range(count):
            next(gens[sb])

    run(0)
    run(0)
    norm_prev_halo()
    if NSB > 1:
        run(1)
    else:
        norm_next_halo()
    kv_chunk(0)
    kv_chunk(1)
    run(0, 5)
    for sb in range(1, NSB):
        run(sb)
        if sb + 1 < NSB:
            run(sb + 1)
        else:
            norm_next_halo()
        kv_chunk(sb + 1)
        run(sb)
        run(sb - 1)
        run(sb, 4)
    run(NSB - 1)


def _resident(shape):
    return pl.BlockSpec(shape, lambda b, i: (0,) * len(shape), pipeline_mode=pl.Buffered(1))


def _layer(x, mem, g_pre, w_in, w_conv, attn_sink, g_mem, w_mem_kv, w_up_a, w_up_b, w_up_m, w_out, g_post,
           cos_t, sin_t, bias_t):
    bsz, s, d = x.shape
    assert (s, d) == (SEQ, D_MODEL) and s % TS == 0 and SB % QB == 0 and SB == 2 * HALO
    assert (STAGE_ROWS, STAGE_COLS) == (SB, d)
    nb = TS // HALO
    hbm = pl.BlockSpec(memory_space=pl.ANY)
    in_specs = [
        pl.BlockSpec(memory_space=pltpu.SMEM),
        pl.BlockSpec((1, TS, d), lambda b, i: (b, i, 0)),
        pl.BlockSpec((1, HALO, d), lambda b, i: (b, jnp.maximum(i * nb - 1, 0), 0)),
        pl.BlockSpec((1, HALO, d), lambda b, i: (b, jnp.minimum((i + 1) * nb, s // HALO - 1), 0)),
        _resident((1, d)),
        hbm,
        _resident((CONV_WIDTH, A_WIDTH)),
        _resident((s + 2 * HALO, LANES)),
        _resident((s + 2 * HALO, LANES)),
        _resident((3, QB, 3 * QB)),
        pl.BlockSpec((1, MEM_LEN, d), lambda b, i: (b, 0, 0)),
        _resident((1, d)),
        hbm, hbm, hbm, hbm, hbm,
        _resident((1, d)),
    ]
    scratch = [
        pltpu.VMEM((EXT, d), BF16),
        pltpu.VMEM((EXT, KV_WIDTH), BF16),
        pltpu.VMEM((EXT, KV_WIDTH), BF16),
        pltpu.VMEM((EXT, 2 * KV_WIDTH), BF16),
        pltpu.VMEM((EXT, 2 * KV_WIDTH), BF16),
        pltpu.VMEM((SB + 2 * CONV_HALO, A_WIDTH), F32),
        pltpu.VMEM((SB, d), F32),
        pltpu.VMEM((3, SB, d), F32),
        pltpu.VMEM((d, IN_WIDTH), BF16),
        pltpu.VMEM((d, 2 * MEM_WIDTH), BF16),
        pltpu.VMEM((A_WIDTH, d), BF16),
        pltpu.VMEM((ATTN_WIDTH, d), BF16),
        pltpu.VMEM((MEM_WIDTH, d), BF16),
        pltpu.VMEM((d, d), BF16),
        pltpu.VMEM((MEM_WIDTH, MEM_LEN), BF16),
        pltpu.VMEM((MEM_HEADS, MEM_LEN, 2 * MEM_HEAD_DIM), BF16),
        pltpu.VMEM((STAGE_SLOTS, STAGE_ROWS, STAGE_COLS), F32),
        pltpu.SemaphoreType.DMA((STAGE_SLOTS + 4,)),
    ]
    return pl.pallas_call(
        _layer_kernel,
        grid=(bsz, s // TS),
        in_specs=in_specs,
        out_specs=pl.BlockSpec((1, TS, d), lambda b, i: (b, i, 0)),
        out_shape=jax.ShapeDtypeStruct(x.shape, x.dtype),
        scratch_shapes=scratch,
        compiler_params=pltpu.CompilerParams(dimension_semantics=("arbitrary", "arbitrary"),
                                             vmem_limit_bytes=VMEM_LIMIT_BYTES),
        name="hybrid_layer",
    )(attn_sink, x, x, x, g_pre.reshape(1, d), w_in, w_conv, cos_t, sin_t, bias_t, mem, g_mem.reshape(1, d),
      w_mem_kv, w_up_a, w_up_b, w_up_m, w_out, g_post.reshape(1, d))


def kernel(x, mem, g_pre, w_in, w_conv, attn_sink, g_mem, w_mem_kv, w_up_a, w_up_b, w_up_m, w_out, g_post):
    cos_np, sin_np = _rope_tables()
    cos_t, sin_t, bias_t = jnp.asarray(cos_np), jnp.asarray(sin_np), jnp.asarray(_band_bias())
    for l in range(g_pre.shape[0]):
        x = _layer(x, mem, g_pre[l], w_in[l], w_conv[l], attn_sink[l], g_mem[l], w_mem_kv[l], w_up_a[l], w_up_b[l],
                   w_up_m[l], w_out[l], g_post[l], cos_t, sin_t, bias_t)
    return x
```

```python
import math

import numpy as np
import jax
import jax.numpy as jnp
from jax import lax
from jax.experimental import pallas as pl
from jax.experimental.pallas import tpu as pltpu

F32 = jnp.float32
BF16 = jnp.bfloat16

D_MODEL = 1024
SEQ = 4096
MEM_LEN = 256
EPS = 1e-6
CONV_WIDTH = 3
A_WIDTH = 512
HEAD_DIM = 64
ATTN_WIDTH = 512
N_Q_HEADS = 8
KV_WIDTH = 128
WINDOW = 128
ROPE_THETA = 500000.0
ROT_DIM = 16
MEM_HEADS = 4
MEM_HEAD_DIM = 128
MEM_WIDTH = 512

C_AB, C_AC, C_AX, C_AZ = 0, 512, 1024, 1536
C_BQ, C_BK, C_BV, C_BZ = 2048, 2560, 2688, 2816
C_MQ, C_MZ = 3328, 3840
C_G0, C_G1, C_G2 = 4352, 5376, 6400
IN_WIDTH = 7424

LANES = 128
BF16_ROWS = 16
VMEM_LIMIT_BYTES = 60 * 1024 * 1024

SB = 256
NSB = 2
TS = SB * NSB
QB = WINDOW
HALO = WINDOW
CONV_HALO = BF16_ROWS
EXT = TS + 2 * HALO
STAGE_SLOTS = 4
STAGE_ROWS = SB
STAGE_COLS = D_MODEL
OUT_ROWS = SB // 2
OUT_SLOTS = 4

LOG2E = math.log2(math.e)


def _sigmoid(v):
    return 1.0 / (1.0 + jnp.exp(-v))


def _silu(v):
    return v * _sigmoid(v)


def _rms(v, g):
    ms = jnp.mean(v * v, axis=-1, keepdims=True)
    return v * lax.rsqrt(ms + EPS) * g


def _dot(a, b):
    return jnp.dot(a, b, preferred_element_type=F32)


def _dot_nt(a, b):
    return lax.dot_general(a, b, (((1,), (1,)), ((), ())), preferred_element_type=F32)


def _rope(t, cos, ssin, low8):
    partner = jnp.where(low8, pltpu.roll(t, LANES - ROT_DIM // 2, 1), pltpu.roll(t, ROT_DIM // 2, 1))
    return t * cos + partner * ssin


def _rope_tables():
    half = ROT_DIM // 2
    inv_freq = np.power(np.float32(ROPE_THETA), -np.arange(half, dtype=np.float32) * np.float32(2.0 / ROT_DIM))
    pos = (np.arange(SEQ + 2 * HALO) - HALO).astype(np.float32)
    ang = (pos[:, None] * inv_freq[None, :]).astype(np.float32)
    cos, sin = np.cos(ang).astype(np.float32), np.sin(ang).astype(np.float32)
    ct = np.ones((SEQ + 2 * HALO, HEAD_DIM), np.float32)
    st = np.zeros((SEQ + 2 * HALO, HEAD_DIM), np.float32)
    ct[:, :half], ct[:, half:ROT_DIM] = cos, cos
    st[:, :half], st[:, half:ROT_DIM] = -sin, sin
    return np.tile(ct, (1, LANES // HEAD_DIM)), np.tile(st, (1, LANES // HEAD_DIM))


def _band_bias():
    r = np.arange(QB)[:, None]
    c = np.arange(3 * QB)[None, :]
    band = (c >= r) & (c <= r + 2 * WINDOW)
    masks = [band, band & (c >= QB), band & (c < 2 * QB)]
    return np.stack([np.where(m, 0.0, -np.inf) for m in masks]).astype(np.float32)


def _stream_cast(jobs, slots, sem):
    chunks = [(src, dst, r0, c0, min(STAGE_COLS, src.shape[1] - c0))
              for src, dst in jobs
              for r0 in range(0, src.shape[0], STAGE_ROWS)
              for c0 in range(0, src.shape[1], STAGE_COLS)]
    n_slots = len(slots)

    def copy(n):
        src, _, r0, c0, cols = chunks[n]
        k = n % n_slots
        return pltpu.make_async_copy(src.at[pl.ds(r0, STAGE_ROWS), pl.ds(c0, cols)],
                                     slots[k].at[:, pl.ds(0, cols)], sem.at[k])

    for n in range(min(n_slots, len(chunks))):
        copy(n).start()
    for n, (_, dst, r0, c0, cols) in enumerate(chunks):
        copy(n).wait()
        dst[r0:r0 + STAGE_ROWS, c0:c0 + cols] = slots[n % n_slots][:, 0:cols].astype(BF16)
        if n + n_slots < len(chunks):
            copy(n + n_slots).start()


def _window_softmax_pv(s, v_win, sink_col):
    m = jnp.maximum(jnp.max(s, axis=1, keepdims=True), sink_col)
    p = jnp.exp2(s - m)
    ov = _dot(p.astype(BF16), v_win)
    return ov[:, :LANES] / (ov[:, LANES:] + jnp.exp2(sink_col - m))


def _layer_kernel(sink_ref, x_ref, xp_ref, xn_ref, gpre_ref, win_hbm, wconv_ref, cos_ref, sin_ref, bias_ref,
                  mem_ref, gmem_ref, wmkv_hbm, wua_hbm, wub_hbm, wum_hbm, wout_hbm, gpost_ref,
                  out_hbm, h_s, k_s, ksw_s, v_s, vsw_s, cu_s, mrg_s, gate_s,
                  win_ref, wmkv_ref, wua_ref, wub_ref, wum_ref, wout_ref, mkt_ref, mvx_ref,
                  stage_s, stage_sem, hprev_s, xprev_s, out_s, out_sem):
    b = pl.program_id(0)
    i = pl.program_id(1)
    n_tiles = pl.num_programs(1)
    step = b * n_tiles + i
    last_step = pl.num_programs(0) * n_tiles - 1
    row0 = pl.multiple_of(i * TS, TS)
    tile_row = step * TS
    g_pre = gpre_ref[...]
    g_post = gpost_ref[...]

    def out_copy(slot, dst_row):
        return pltpu.make_async_copy(out_s.at[slot], out_hbm.at[pl.ds(dst_row, OUT_ROWS)], out_sem.at[slot])

    pending_out = []

    def write_out(slot, value, dst_row):
        out_s[slot] = value
        pending_out.append((slot, dst_row))

    def flush_out():
        for slot, dst_row in pending_out:
            out_copy(slot, dst_row).start()
        pending_out.clear()

    def wait_out_slots(slots):
        for slot in slots:
            out_copy(slot, 0).wait()

    @pl.when(step == 0)
    def _first_step():
        slots = ([stage_s.at[k] for k in range(STAGE_SLOTS)] + [gate_s.at[k] for k in range(3)] + [mrg_s])
        _stream_cast([(win_hbm, win_ref), (wmkv_hbm, wmkv_ref), (wua_hbm, wua_ref), (wub_hbm, wub_ref),
                      (wum_hbm, wum_ref), (wout_hbm, wout_ref)], slots, stage_sem)
        mrg_s[...] = jnp.zeros(mrg_s.shape, F32)
        hprev_s[...] = jnp.zeros(hprev_s.shape, BF16)
        xprev_s[...] = jnp.zeros(xprev_s.shape, F32)
        out_s[...] = jnp.zeros(out_s.shape, F32)
        for slot in range(OUT_SLOTS):
            out_copy(slot, _first_rows(slot)).start()

    par = b % 2
    @pl.when(i == 0)
    def _memory_kv():
        mn = _rms(mem_ref[0], gmem_ref[...]).astype(BF16)
        kv = _dot(mn, wmkv_ref[...])
        mkt_ref[par] = kv[:, :MEM_WIDTH].T.astype(BF16)
        ones_m = jnp.ones((MEM_LEN, MEM_HEAD_DIM), F32)
        for hh in range(MEM_HEADS):
            c0 = MEM_WIDTH + hh * MEM_HEAD_DIM
            mvx_ref[par, hh] = jnp.concatenate([kv[:, c0:c0 + MEM_HEAD_DIM], ones_m], axis=1).astype(BF16)

    lane = lax.broadcasted_iota(jnp.int32, (1, LANES), 1)
    low_head = lane < HEAD_DIM
    low8 = (lane % HEAD_DIM) < (ROT_DIM // 2)

    rows4 = lax.broadcasted_iota(jnp.int32, (4 * QB, 1), 0)
    def sink_rows(h0, h1, h2, h3):
        return LOG2E * jnp.where(rows4 < QB, sink_ref[h0],
                                 jnp.where(rows4 < 2 * QB, sink_ref[h1],
                                           jnp.where(rows4 < 3 * QB, sink_ref[h2], sink_ref[h3])))
    sink_a = sink_rows(0, 2, 5, 7)
    sink_b = sink_rows(1, 3, 4, 6)
    n_blocks = SEQ // QB
    ones = jnp.ones((SB, LANES), BF16)

    def norm_main(sb):
        r0 = HALO + sb * SB
        h_s[r0:r0 + SB] = _rms(x_ref[0, sb * SB:(sb + 1) * SB], g_pre).astype(BF16)

    def norm_prev_halo():
        h_s[0:HALO] = _rms(xp_ref[0], g_pre).astype(BF16)

    def norm_next_halo():
        h_s[HALO + TS:EXT] = _rms(xn_ref[0], g_pre).astype(BF16)

    def kv_chunk(c):
        r0 = c * SB
        kv = _dot(h_s[r0:r0 + SB], win_ref[:, C_BK:C_BV + KV_WIDTH])
        cos = cos_ref[pl.ds(row0 + r0, SB), :]
        ssin = sin_ref[pl.ds(row0 + r0, SB), :]
        k = _rope(kv[:, :KV_WIDTH], cos, ssin, low8)
        v = kv[:, KV_WIDTH:]
        k_s[r0:r0 + SB] = k.astype(BF16)
        ksw_s[r0:r0 + SB] = pltpu.roll(k, HEAD_DIM, 1).astype(BF16)
        v_s[r0:r0 + SB, 0:LANES] = v.astype(BF16)
        v_s[r0:r0 + SB, LANES:2 * LANES] = ones
        vsw_s[r0:r0 + SB, 0:LANES] = pltpu.roll(v, HEAD_DIM, 1).astype(BF16)
        vsw_s[r0:r0 + SB, LANES:2 * LANES] = ones

    def memory_and_tail(h, x_rows, mem_par, dst_row, slots):
        mq = _dot(h, win_ref[:, C_MQ:C_MQ + MEM_WIDTH]) * ((MEM_HEAD_DIM ** -0.5) * LOG2E)
        m_zs = _silu(_dot(h, win_ref[:, C_MZ:C_MZ + MEM_WIDTH]))
        scores = []
        for hh in range(MEM_HEADS):
            hs = slice(hh * MEM_HEAD_DIM, (hh + 1) * MEM_HEAD_DIM)
            scores.append(_dot(mq[:, hs].astype(BF16), mkt_ref[mem_par, hs, :]))
        gate_s[2] = _sigmoid(_dot(h, win_ref[:, C_G2:C_G2 + D_MODEL]))
        ym_heads = []
        for hh in range(MEM_HEADS):
            s = scores[hh]
            p = jnp.exp2(s - jnp.max(s, axis=1, keepdims=True))
            ov = _dot(p.astype(BF16), mvx_ref[mem_par, hh])
            ym_heads.append(ov[:, :MEM_HEAD_DIM] / ov[:, MEM_HEAD_DIM:])
        ym = (jnp.concatenate(ym_heads, axis=1) * m_zs).astype(BF16)
        yield
        for half in range(SB // OUT_ROWS):
            rs = slice(half * OUT_ROWS, (half + 1) * OUT_ROWS)
            um = _dot(ym[rs], wum_ref[...])
            merged = mrg_s[rs] + gate_s[2, rs] * um
            o = _dot(merged.astype(BF16), wout_ref[...])
            write_out(slots[half], x_rows(rs) + _rms(o, g_post), dst_row + half * OUT_ROWS)
        yield

    def sub_block(sb):
        m0 = HALO + sb * SB

        def gate(col, slot):
            gate_s[slot] = _sigmoid(_dot(h, win_ref[:, col:col + D_MODEL]))

        def conv_input():
            hx = h_s[m0 - CONV_HALO:m0 + SB + CONV_HALO]
            cx = _dot(hx, win_ref[:, C_AC:C_AX + A_WIDTH])
            cu = cx[:, :A_WIDTH] * cx[:, A_WIDTH:]
            first = row0 + sb * SB == 0
            last = row0 + (sb + 1) * SB == SEQ
            cu_s[0:CONV_HALO] = jnp.where(first, 0.0, cu[0:CONV_HALO])
            cu_s[CONV_HALO:CONV_HALO + SB] = cu[CONV_HALO:CONV_HALO + SB]
            cu_s[CONV_HALO + SB:] = jnp.where(last, 0.0, cu[CONV_HALO + SB:])

        norm_main(sb)
        yield
        h = h_s[m0:m0 + SB]
        q = _dot(h, win_ref[:, C_BQ:C_BQ + ATTN_WIDTH])
        b_zs = _silu(_dot(h, win_ref[:, C_BZ:C_BZ + ATTN_WIDTH]))
        yield
        a_b = _dot(h, win_ref[:, C_AB:C_AB + A_WIDTH])
        a_zs = _silu(_dot(h, win_ref[:, C_AZ:C_AZ + A_WIDTH]))
        scale = (HEAD_DIM ** -0.5) * LOG2E
        qcos = cos_ref[pl.ds(row0 + m0, SB), :] * scale
        qsin = sin_ref[pl.ds(row0 + m0, SB), :] * scale
        q_lo, q_hi = [], []
        for mblk in range(ATTN_WIDTH // LANES):
            qr = _rope(q[:, mblk * LANES:(mblk + 1) * LANES], qcos, qsin, low8)
            q_lo.append(jnp.where(low_head, qr, 0.0).astype(BF16))
            q_hi.append(jnp.where(low_head, 0.0, qr).astype(BF16))
        yield
        fillers = [lambda: gate(C_G1, 1), conv_input]
        yb_blocks = []
        for j in range(SB // QB):
            qs = slice(j * QB, (j + 1) * QB)
            jb = sb * (SB // QB) + j
            blk = i * (TS // QB) + jb
            variant = jnp.where(blk == 0, 1, jnp.where(blk == n_blocks - 1, 2, 0))
            bias1 = bias_ref[variant]
            bias = jnp.concatenate([bias1, bias1, bias1, bias1], axis=0)
            lhs_a = jnp.concatenate([q_lo[0][qs], q_lo[1][qs], q_hi[2][qs], q_hi[3][qs]], axis=0)
            lhs_b = jnp.concatenate([q_hi[0][qs], q_hi[1][qs], q_lo[2][qs], q_lo[3][qs]], axis=0)
            ws = slice(jb * QB, jb * QB + 3 * QB)
            s_a = _dot_nt(lhs_a, k_s[ws]) + bias
            s_b = _dot_nt(lhs_b, ksw_s[ws]) + bias
            fillers[j % len(fillers)]()
            o_a = _window_softmax_pv(s_a, v_s[ws], sink_a)
            o_b = _window_softmax_pv(s_b, vsw_s[ws], sink_b)
            cols = [jnp.where(low_head, o_a[0:QB], o_b[0:QB]),
                    jnp.where(low_head, o_a[QB:2 * QB], o_b[QB:2 * QB]),
                    jnp.where(low_head, o_b[2 * QB:3 * QB], o_a[2 * QB:3 * QB]),
                    jnp.where(low_head, o_b[3 * QB:4 * QB], o_a[3 * QB:4 * QB])]
            yb_blocks.append(jnp.concatenate(cols, axis=1))
            yield
        yb = jnp.concatenate(yb_blocks, axis=0) * b_zs
        ub = _dot(yb.astype(BF16), wub_ref[...])
        mrg_s[...] = gate_s[1] * ub
        gate(C_G0, 0)
        y = (cu_s[CONV_HALO - 1:CONV_HALO - 1 + SB] * wconv_ref[0:1]
             + cu_s[CONV_HALO:CONV_HALO + SB] * wconv_ref[1:2]
             + cu_s[CONV_HALO + 1:CONV_HALO + 1 + SB] * wconv_ref[2:3])
        ya = a_b * y * a_zs
        ua = _dot(ya.astype(BF16), wua_ref[...])
        mrg_s[...] += gate_s[0] * ua
        yield
        if sb + 1 < NSB:
            yield from memory_and_tail(h, lambda rs: x_ref[0, sb * SB + rs.start:sb * SB + rs.stop], par,
                                       tile_row + sb * SB, TAIL_SLOTS)
        else:
            hprev_s[...] = h
            xprev_s[...] = x_ref[0, sb * SB:(sb + 1) * SB]
            yield

    def deferred_tail(mem_par, dst_row):
        return memory_and_tail(hprev_s[...], lambda rs: xprev_s[rs], mem_par, dst_row, DEFERRED_SLOTS)

    gens = [sub_block(sb) for sb in range(NSB)]
    def run(sb, count=1):
        for _ in range(count):
            next(gens[sb])

    def norm_for_chunk(c):
        if c < NSB:
            run(c)
        else:
            norm_next_halo()

    wait_out_slots(range(OUT_SLOTS))
    prev_step = jnp.maximum(step - 1, 0)
    prev_par = (prev_step // n_tiles) % 2
    prev = deferred_tail(prev_par, prev_step * TS + (NSB - 1) * SB)
    next(prev)
    run(0)
    next(prev)
    run(0)
    norm_prev_halo()
    norm_for_chunk(1)
    kv_chunk(0)
    kv_chunk(1)
    for sb in range(NSB):
        run(sb)
        if sb > 0:
            run(sb - 1)
        run(sb)
        if sb == 0:
            flush_out()
        run(sb)
        if sb + 1 < NSB:
            norm_for_chunk(sb + 2)
        run(sb)
        if sb > 0:
            flush_out()
        run(sb)
        if sb + 1 < NSB:
            run(sb + 1)
            kv_chunk(sb + 2)
    assert not pending_out

    @pl.when(step == last_step)
    def _last_step():
        wait_out_slots(DEFERRED_SLOTS)
        final = deferred_tail(par, tile_row + (NSB - 1) * SB)
        next(final)
        next(final)
        flush_out()
        wait_out_slots(range(OUT_SLOTS))


DEFERRED_SLOTS = (0, 1)
TAIL_SLOTS = (2, 3)


def _first_rows(slot):
    if slot in DEFERRED_SLOTS:
        return (NSB - 1) * SB + DEFERRED_SLOTS.index(slot) * OUT_ROWS
    return TAIL_SLOTS.index(slot) * OUT_ROWS


def _resident(shape):
    return pl.BlockSpec(shape, lambda b, i: (0,) * len(shape), pipeline_mode=pl.Buffered(1))


def _layer(x, mem, g_pre, w_in, w_conv, attn_sink, g_mem, w_mem_kv, w_up_a, w_up_b, w_up_m, w_out, g_post,
           cos_t, sin_t, bias_t):
    bsz, s, d = x.shape
    assert (s, d) == (SEQ, D_MODEL) and s % TS == 0 and SB % QB == 0 and SB == 2 * HALO
    assert (STAGE_ROWS, STAGE_COLS) == (SB, d)
    assert NSB == 2 and OUT_SLOTS == len(DEFERRED_SLOTS) + len(TAIL_SLOTS) and SB // OUT_ROWS == 2
    nb = TS // HALO
    hbm = pl.BlockSpec(memory_space=pl.ANY)
    in_specs = [
        pl.BlockSpec(memory_space=pltpu.SMEM),
        pl.BlockSpec((1, TS, d), lambda b, i: (b, i, 0)),
        pl.BlockSpec((1, HALO, d), lambda b, i: (b, jnp.maximum(i * nb - 1, 0), 0)),
        pl.BlockSpec((1, HALO, d), lambda b, i: (b, jnp.minimum((i + 1) * nb, s // HALO - 1), 0)),
        _resident((1, d)),
        hbm,
        _resident((CONV_WIDTH, A_WIDTH)),
        _resident((s + 2 * HALO, LANES)),
        _resident((s + 2 * HALO, LANES)),
        _resident((3, QB, 3 * QB)),
        pl.BlockSpec((1, MEM_LEN, d), lambda b, i: (b, 0, 0)),
        _resident((1, d)),
        hbm, hbm, hbm, hbm, hbm,
        _resident((1, d)),
    ]
    scratch = [
        pltpu.VMEM((EXT, d), BF16),
        pltpu.VMEM((EXT, KV_WIDTH), BF16),
        pltpu.VMEM((EXT, KV_WIDTH), BF16),
        pltpu.VMEM((EXT, 2 * KV_WIDTH), BF16),
        pltpu.VMEM((EXT, 2 * KV_WIDTH), BF16),
        pltpu.VMEM((SB + 2 * CONV_HALO, A_WIDTH), F32),
        pltpu.VMEM((SB, d), F32),
        pltpu.VMEM((3, SB, d), F32),
        pltpu.VMEM((d, IN_WIDTH), BF16),
        pltpu.VMEM((d, 2 * MEM_WIDTH), BF16),
        pltpu.VMEM((A_WIDTH, d), BF16),
        pltpu.VMEM((ATTN_WIDTH, d), BF16),
        pltpu.VMEM((MEM_WIDTH, d), BF16),
        pltpu.VMEM((d, d), BF16),
        pltpu.VMEM((2, MEM_WIDTH, MEM_LEN), BF16),
        pltpu.VMEM((2, MEM_HEADS, MEM_LEN, 2 * MEM_HEAD_DIM), BF16),
        pltpu.VMEM((STAGE_SLOTS, STAGE_ROWS, STAGE_COLS), F32),
        pltpu.SemaphoreType.DMA((STAGE_SLOTS + 4,)),
        pltpu.VMEM((SB, d), BF16),
        pltpu.VMEM((SB, d), F32),
        pltpu.VMEM((OUT_SLOTS, OUT_ROWS, d), F32),
        pltpu.SemaphoreType.DMA((OUT_SLOTS,)),
    ]
    out = pl.pallas_call(
        _layer_kernel,
        grid=(bsz, s // TS),
        in_specs=in_specs,
        out_specs=hbm,
        out_shape=jax.ShapeDtypeStruct((bsz * s, d), x.dtype),
        scratch_shapes=scratch,
        compiler_params=pltpu.CompilerParams(dimension_semantics=("arbitrary", "arbitrary"),
                                             vmem_limit_bytes=VMEM_LIMIT_BYTES),
        name="hybrid_layer",
    )(attn_sink, x, x, x, g_pre.reshape(1, d), w_in, w_conv, cos_t, sin_t, bias_t, mem, g_mem.reshape(1, d),
      w_mem_kv, w_up_a, w_up_b, w_up_m, w_out, g_post.reshape(1, d))
    return out.reshape(bsz, s, d)


def kernel(x, mem, g_pre, w_in, w_conv, attn_sink, g_mem, w_mem_kv, w_up_a, w_up_b, w_up_m, w_out, g_post):
    cos_np, sin_np = _rope_tables()
    cos_t, sin_t, bias_t = jnp.asarray(cos_np), jnp.asarray(sin_np), jnp.asarray(_band_bias())
    for l in range(g_pre.shape[0]):
        x = _layer(x, mem, g_pre[l], w_in[l], w_conv[l], attn_sink[l], g_mem[l], w_mem_kv[l], w_up_a[l], w_up_b[l],
                   w_up_m[l], w_out[l], g_post[l], cos_t, sin_t, bias_t)
    return x
```

```python
import math

import numpy as np
import jax
import jax.numpy as jnp
from jax import lax
from jax.experimental import pallas as pl
from jax.experimental.pallas import tpu as pltpu

F32 = jnp.float32
BF16 = jnp.bfloat16

D_MODEL = 1024
SEQ = 4096
MEM_LEN = 256
EPS = 1e-6
CONV_WIDTH = 3
A_WIDTH = 512
HEAD_DIM = 64
ATTN_WIDTH = 512
N_Q_HEADS = 8
KV_WIDTH = 128
WINDOW = 128
ROPE_THETA = 500000.0
ROT_DIM = 16
MEM_HEADS = 4
MEM_HEAD_DIM = 128
MEM_WIDTH = 512

C_AB, C_AC, C_AX, C_AZ = 0, 512, 1024, 1536
C_BQ, C_BK, C_BV, C_BZ = 2048, 2560, 2688, 2816
C_MQ, C_MZ = 3328, 3840
C_G0, C_G1, C_G2 = 4352, 5376, 6400
IN_WIDTH = 7424

LANES = 128
BF16_ROWS = 16
VMEM_LIMIT_BYTES = 60 * 1024 * 1024

SB = 256
NSB = 2
TS = SB * NSB
QB = WINDOW
HALO = WINDOW
CONV_HALO = BF16_ROWS
EXT = TS + 2 * HALO
STAGE_SLOTS = 4
STAGE_ROWS = SB
STAGE_COLS = D_MODEL
OUT_ROWS = SB // 2
OUT_SLOTS = 4

LOG2E = math.log2(math.e)


def _sigmoid(v):
    return 1.0 / (1.0 + jnp.exp(-v))


def _silu(v):
    return v * _sigmoid(v)


def _rms(v, g):
    ms = jnp.mean(v * v, axis=-1, keepdims=True)
    return v * lax.rsqrt(ms + EPS) * g


def _dot(a, b):
    return jnp.dot(a, b, preferred_element_type=F32)


def _dot_nt(a, b):
    return lax.dot_general(a, b, (((1,), (1,)), ((), ())), preferred_element_type=F32)


def _rope(t, cos, ssin, low8):
    partner = jnp.where(low8, pltpu.roll(t, LANES - ROT_DIM // 2, 1), pltpu.roll(t, ROT_DIM // 2, 1))
    return t * cos + partner * ssin


def _rope_tables():
    half = ROT_DIM // 2
    inv_freq = np.power(np.float32(ROPE_THETA), -np.arange(half, dtype=np.float32) * np.float32(2.0 / ROT_DIM))
    pos = (np.arange(SEQ + 2 * HALO) - HALO).astype(np.float32)
    ang = (pos[:, None] * inv_freq[None, :]).astype(np.float32)
    cos, sin = np.cos(ang).astype(np.float32), np.sin(ang).astype(np.float32)
    ct = np.ones((SEQ + 2 * HALO, HEAD_DIM), np.float32)
    st = np.zeros((SEQ + 2 * HALO, HEAD_DIM), np.float32)
    ct[:, :half], ct[:, half:ROT_DIM] = cos, cos
    st[:, :half], st[:, half:ROT_DIM] = -sin, sin
    return np.tile(ct, (1, LANES // HEAD_DIM)), np.tile(st, (1, LANES // HEAD_DIM))


def _band_bias():
    r = np.arange(QB)[:, None]
    c = np.arange(3 * QB)[None, :]
    band = (c >= r) & (c <= r + 2 * WINDOW)
    masks = [band, band & (c >= QB), band & (c < 2 * QB)]
    return np.stack([np.where(m, 0.0, -np.inf) for m in masks]).astype(np.float32)


def _stream_cast(jobs, slots, sem):
    chunks = [(src, dst, r0, c0, min(STAGE_COLS, src.shape[1] - c0))
              for src, dst in jobs
              for r0 in range(0, src.shape[0], STAGE_ROWS)
              for c0 in range(0, src.shape[1], STAGE_COLS)]
    n_slots = len(slots)

    def copy(n):
        src, _, r0, c0, cols = chunks[n]
        k = n % n_slots
        return pltpu.make_async_copy(src.at[pl.ds(r0, STAGE_ROWS), pl.ds(c0, cols)],
                                     slots[k].at[:, pl.ds(0, cols)], sem.at[k])

    for n in range(min(n_slots, len(chunks))):
        copy(n).start()
    for n, (_, dst, r0, c0, cols) in enumerate(chunks):
        copy(n).wait()
        dst[r0:r0 + STAGE_ROWS, c0:c0 + cols] = slots[n % n_slots][:, 0:cols].astype(BF16)
        if n + n_slots < len(chunks):
            copy(n + n_slots).start()


def _window_softmax_pv(s, v_win, sink_col):
    m = jnp.maximum(jnp.max(s, axis=1, keepdims=True), sink_col)
    p = jnp.exp2(s - m)
    ov = _dot(p.astype(BF16), v_win)
    return ov[:, :LANES] / (ov[:, LANES:] + jnp.exp2(sink_col - m))


def _layer_kernel(sink_ref, x_ref, xp_ref, xn_ref, gpre_ref, win_hbm, wconv_ref, cos_ref, sin_ref, bias_ref,
                  mem_ref, gmem_ref, wmkv_hbm, wua_hbm, wub_hbm, wum_hbm, wout_hbm, gpost_ref,
                  out_hbm, h_s, k_s, ksw_s, v_s, vsw_s, cu_s, mrg_s, gate_s,
                  win_ref, wmkv_ref, wua_ref, wub_ref, wum_ref, wout_ref, mkt_ref, mvx_ref,
                  stage_s, stage_sem, hprev_s, xprev_s, out_s, out_sem):
    b = pl.program_id(0)
    i = pl.program_id(1)
    n_tiles = pl.num_programs(1)
    step = b * n_tiles + i
    last_step = pl.num_programs(0) * n_tiles - 1
    row0 = pl.multiple_of(i * TS, TS)
    tile_row = step * TS
    g_pre = gpre_ref[...]
    g_post = gpost_ref[...]

    def out_copy(slot, dst_row):
        return pltpu.make_async_copy(out_s.at[slot], out_hbm.at[pl.ds(dst_row, OUT_ROWS)], out_sem.at[0])

    pending_out = []

    def write_out(slot, value, dst_row):
        out_s[slot] = value
        pending_out.append((slot, dst_row))

    def flush_out():
        for slot, dst_row in pending_out:
            out_copy(slot, dst_row).start()
        pending_out.clear()

    def wait_out_slots(slots):
        for slot in slots:
            out_copy(slot, 0).wait()

    @pl.when(step == 0)
    def _first_step():
        slots = ([stage_s.at[k] for k in range(STAGE_SLOTS)] + [gate_s.at[k] for k in range(3)] + [mrg_s])
        _stream_cast([(win_hbm, win_ref), (wmkv_hbm, wmkv_ref), (wua_hbm, wua_ref), (wub_hbm, wub_ref),
                      (wum_hbm, wum_ref), (wout_hbm, wout_ref)], slots, stage_sem)
        mrg_s[...] = jnp.zeros(mrg_s.shape, F32)
        hprev_s[...] = jnp.zeros(hprev_s.shape, BF16)
        xprev_s[...] = jnp.zeros(xprev_s.shape, F32)
        out_s[...] = jnp.zeros(out_s.shape, F32)
        for slot in range(OUT_SLOTS):
            out_copy(slot, _first_rows(slot)).start()

    par = b % 2
    @pl.when(i == 0)
    def _memory_kv():
        mn = _rms(mem_ref[0], gmem_ref[...]).astype(BF16)
        kv = _dot(mn, wmkv_ref[...])
        mkt_ref[par] = kv[:, :MEM_WIDTH].T.astype(BF16)
        ones_m = jnp.ones((MEM_LEN, MEM_HEAD_DIM), F32)
        for hh in range(MEM_HEADS):
            c0 = MEM_WIDTH + hh * MEM_HEAD_DIM
            mvx_ref[par, hh] = jnp.concatenate([kv[:, c0:c0 + MEM_HEAD_DIM], ones_m], axis=1).astype(BF16)

    lane = lax.broadcasted_iota(jnp.int32, (1, LANES), 1)
    low_head = lane < HEAD_DIM
    low8 = (lane % HEAD_DIM) < (ROT_DIM // 2)

    rows4 = lax.broadcasted_iota(jnp.int32, (4 * QB, 1), 0)
    def sink_rows(h0, h1, h2, h3):
        return LOG2E * jnp.where(rows4 < QB, sink_ref[h0],
                                 jnp.where(rows4 < 2 * QB, sink_ref[h1],
                                           jnp.where(rows4 < 3 * QB, sink_ref[h2], sink_ref[h3])))
    sink_a = sink_rows(0, 2, 5, 7)
    sink_b = sink_rows(1, 3, 4, 6)
    n_blocks = SEQ // QB
    ones = jnp.ones((SB, LANES), BF16)

    def norm_main(sb):
        r0 = HALO + sb * SB
        h_s[r0:r0 + SB] = _rms(x_ref[0, sb * SB:(sb + 1) * SB], g_pre).astype(BF16)

    def norm_prev_halo():
        h_s[0:HALO] = _rms(xp_ref[0], g_pre).astype(BF16)

    def norm_next_halo():
        h_s[HALO + TS:EXT] = _rms(xn_ref[0], g_pre).astype(BF16)

    def kv_chunk(c):
        r0 = c * SB
        kv = _dot(h_s[r0:r0 + SB], win_ref[:, C_BK:C_BV + KV_WIDTH])
        cos = cos_ref[pl.ds(row0 + r0, SB), :]
        ssin = sin_ref[pl.ds(row0 + r0, SB), :]
        k = _rope(kv[:, :KV_WIDTH], cos, ssin, low8)
        v = kv[:, KV_WIDTH:]
        k_s[r0:r0 + SB] = k.astype(BF16)
        ksw_s[r0:r0 + SB] = pltpu.roll(k, HEAD_DIM, 1).astype(BF16)
        v_s[r0:r0 + SB, 0:LANES] = v.astype(BF16)
        v_s[r0:r0 + SB, LANES:2 * LANES] = ones
        vsw_s[r0:r0 + SB, 0:LANES] = pltpu.roll(v, HEAD_DIM, 1).astype(BF16)
        vsw_s[r0:r0 + SB, LANES:2 * LANES] = ones

    def memory_and_tail(h, x_rows, mem_par, dst_row, slots):
        mq = _dot(h, win_ref[:, C_MQ:C_MQ + MEM_WIDTH]) * ((MEM_HEAD_DIM ** -0.5) * LOG2E)
        m_zs = _silu(_dot(h, win_ref[:, C_MZ:C_MZ + MEM_WIDTH]))
        scores = []
        for hh in range(MEM_HEADS):
            hs = slice(hh * MEM_HEAD_DIM, (hh + 1) * MEM_HEAD_DIM)
            scores.append(_dot(mq[:, hs].astype(BF16), mkt_ref[mem_par, hs, :]))
        gate_s[2] = _sigmoid(_dot(h, win_ref[:, C_G2:C_G2 + D_MODEL]))
        ym_heads = []
        for hh in range(MEM_HEADS):
            s = scores[hh]
            p = jnp.exp2(s - jnp.max(s, axis=1, keepdims=True))
            ov = _dot(p.astype(BF16), mvx_ref[mem_par, hh])
            ym_heads.append(ov[:, :MEM_HEAD_DIM] / ov[:, MEM_HEAD_DIM:])
        ym = (jnp.concatenate(ym_heads, axis=1) * m_zs).astype(BF16)
        yield
        for half in range(SB // OUT_ROWS):
            rs = slice(half * OUT_ROWS, (half + 1) * OUT_ROWS)
            um = _dot(ym[rs], wum_ref[...])
            merged = mrg_s[rs] + gate_s[2, rs] * um
            o = _dot(merged.astype(BF16), wout_ref[...])
            write_out(slots[half], x_rows(rs) + _rms(o, g_post), dst_row + half * OUT_ROWS)
        yield

    def sub_block(sb):
        m0 = HALO + sb * SB

        def gate(col, slot):
            gate_s[slot] = _sigmoid(_dot(h, win_ref[:, col:col + D_MODEL]))

        def conv_input():
            hx = h_s[m0 - CONV_HALO:m0 + SB + CONV_HALO]
            cx = _dot(hx, win_ref[:, C_AC:C_AX + A_WIDTH])
            cu = cx[:, :A_WIDTH] * cx[:, A_WIDTH:]
            first = row0 + sb * SB == 0
            last = row0 + (sb + 1) * SB == SEQ
            cu_s[0:CONV_HALO] = jnp.where(first, 0.0, cu[0:CONV_HALO])
            cu_s[CONV_HALO:CONV_HALO + SB] = cu[CONV_HALO:CONV_HALO + SB]
            cu_s[CONV_HALO + SB:] = jnp.where(last, 0.0, cu[CONV_HALO + SB:])

        norm_main(sb)
        yield
        h = h_s[m0:m0 + SB]
        q = _dot(h, win_ref[:, C_BQ:C_BQ + ATTN_WIDTH])
        b_zs = _silu(_dot(h, win_ref[:, C_BZ:C_BZ + ATTN_WIDTH]))
        yield
        a_b = _dot(h, win_ref[:, C_AB:C_AB + A_WIDTH])
        a_zs = _silu(_dot(h, win_ref[:, C_AZ:C_AZ + A_WIDTH]))
        scale = (HEAD_DIM ** -0.5) * LOG2E
        qcos = cos_ref[pl.ds(row0 + m0, SB), :] * scale
        qsin = sin_ref[pl.ds(row0 + m0, SB), :] * scale
        q_lo, q_hi = [], []
        for mblk in range(ATTN_WIDTH // LANES):
            qr = _rope(q[:, mblk * LANES:(mblk + 1) * LANES], qcos, qsin, low8)
            q_lo.append(jnp.where(low_head, qr, 0.0).astype(BF16))
            q_hi.append(jnp.where(low_head, 0.0, qr).astype(BF16))
        yield
        fillers = [lambda: gate(C_G1, 1), conv_input]
        yb_blocks = []
        for j in range(SB // QB):
            qs = slice(j * QB, (j + 1) * QB)
            jb = sb * (SB // QB) + j
            blk = i * (TS // QB) + jb
            variant = jnp.where(blk == 0, 1, jnp.where(blk == n_blocks - 1, 2, 0))
            bias1 = bias_ref[variant]
            bias = jnp.concatenate([bias1, bias1, bias1, bias1], axis=0)
            lhs_a = jnp.concatenate([q_lo[0][qs], q_lo[1][qs], q_hi[2][qs], q_hi[3][qs]], axis=0)
            lhs_b = jnp.concatenate([q_hi[0][qs], q_hi[1][qs], q_lo[2][qs], q_lo[3][qs]], axis=0)
            ws = slice(jb * QB, jb * QB + 3 * QB)
            s_a = _dot_nt(lhs_a, k_s[ws]) + bias
            s_b = _dot_nt(lhs_b, ksw_s[ws]) + bias
            fillers[j % len(fillers)]()
            o_a = _window_softmax_pv(s_a, v_s[ws], sink_a)
            o_b = _window_softmax_pv(s_b, vsw_s[ws], sink_b)
            cols = [jnp.where(low_head, o_a[0:QB], o_b[0:QB]),
                    jnp.where(low_head, o_a[QB:2 * QB], o_b[QB:2 * QB]),
                    jnp.where(low_head, o_b[2 * QB:3 * QB], o_a[2 * QB:3 * QB]),
                    jnp.where(low_head, o_b[3 * QB:4 * QB], o_a[3 * QB:4 * QB])]
            yb_blocks.append(jnp.concatenate(cols, axis=1))
            yield
        yb = jnp.concatenate(yb_blocks, axis=0) * b_zs
        ub = _dot(yb.astype(BF16), wub_ref[...])
        mrg_s[...] = gate_s[1] * ub
        gate(C_G0, 0)
        y = (cu_s[CONV_HALO - 1:CONV_HALO - 1 + SB] * wconv_ref[0:1]
             + cu_s[CONV_HALO:CONV_HALO + SB] * wconv_ref[1:2]
             + cu_s[CONV_HALO + 1:CONV_HALO + 1 + SB] * wconv_ref[2:3])
        ya = a_b * y * a_zs
        ua = _dot(ya.astype(BF16), wua_ref[...])
        mrg_s[...] += gate_s[0] * ua
        yield
        if sb + 1 < NSB:
            yield from memory_and_tail(h, lambda rs: x_ref[0, sb * SB + rs.start:sb * SB + rs.stop], par,
                                       tile_row + sb * SB, TAIL_SLOTS)
        else:
            hprev_s[...] = h
            xprev_s[...] = x_ref[0, sb * SB:(sb + 1) * SB]
            yield

    def deferred_tail(mem_par, dst_row):
        return memory_and_tail(hprev_s[...], lambda rs: xprev_s[rs], mem_par, dst_row, DEFERRED_SLOTS)

    gens = [sub_block(sb) for sb in range(NSB)]
    def run(sb, count=1):
        for _ in range(count):
            next(gens[sb])

    def norm_for_chunk(c):
        if c < NSB:
            run(c)
        else:
            norm_next_halo()

    wait_out_slots(range(OUT_SLOTS))
    prev_step = jnp.maximum(step - 1, 0)
    prev_par = (prev_step // n_tiles) % 2
    prev = deferred_tail(prev_par, prev_step * TS + (NSB - 1) * SB)
    next(prev)
    run(0)
    next(prev)
    run(0)
    norm_prev_halo()
    norm_for_chunk(1)
    kv_chunk(0)
    kv_chunk(1)
    for sb in range(NSB):
        run(sb)
        if sb > 0:
            run(sb - 1)
        run(sb)
        if sb == 0:
            flush_out()
        run(sb)
        if sb + 1 < NSB:
            norm_for_chunk(sb + 2)
        run(sb)
        if sb > 0:
            flush_out()
        run(sb)
        if sb + 1 < NSB:
            run(sb + 1)
            kv_chunk(sb + 2)
    assert not pending_out

    @pl.when(step == last_step)
    def _last_step():
        wait_out_slots(range(OUT_SLOTS))
        final = deferred_tail(par, tile_row + (NSB - 1) * SB)
        next(final)
        next(final)
        flush_out()
        wait_out_slots(DEFERRED_SLOTS)


DEFERRED_SLOTS = (0, 1)
TAIL_SLOTS = (2, 3)


def _first_rows(slot):
    if slot in DEFERRED_SLOTS:
        return (NSB - 1) * SB + DEFERRED_SLOTS.index(slot) * OUT_ROWS
    return TAIL_SLOTS.index(slot) * OUT_ROWS


def _resident(shape):
    return pl.BlockSpec(shape, lambda b, i: (0,) * len(shape), pipeline_mode=pl.Buffered(1))


def _layer(x, mem, g_pre, w_in, w_conv, attn_sink, g_mem, w_mem_kv, w_up_a, w_up_b, w_up_m, w_out, g_post,
           cos_t, sin_t, bias_t):
    bsz, s, d = x.shape
    assert (s, d) == (SEQ, D_MODEL) and s % TS == 0 and SB % QB == 0 and SB == 2 * HALO
    assert (STAGE_ROWS, STAGE_COLS) == (SB, d)
    assert NSB == 2 and OUT_SLOTS == len(DEFERRED_SLOTS) + len(TAIL_SLOTS) and SB // OUT_ROWS == 2
    nb = TS // HALO
    hbm = pl.BlockSpec(memory_space=pl.ANY)
    in_specs = [
        pl.BlockSpec(memory_space=pltpu.SMEM),
        pl.BlockSpec((1, TS, d), lambda b, i: (b, i, 0)),
        pl.BlockSpec((1, HALO, d), lambda b, i: (b, jnp.maximum(i * nb - 1, 0), 0)),
        pl.BlockSpec((1, HALO, d), lambda b, i: (b, jnp.minimum((i + 1) * nb, s // HALO - 1), 0)),
        _resident((1, d)),
        hbm,
        _resident((CONV_WIDTH, A_WIDTH)),
        _resident((s + 2 * HALO, LANES)),
        _resident((s + 2 * HALO, LANES)),
        _resident((3, QB, 3 * QB)),
        pl.BlockSpec((1, MEM_LEN, d), lambda b, i: (b, 0, 0)),
        _resident((1, d)),
        hbm, hbm, hbm, hbm, hbm,
        _resident((1, d)),
    ]
    scratch = [
        pltpu.VMEM((EXT, d), BF16),
        pltpu.VMEM((EXT, KV_WIDTH), BF16),
        pltpu.VMEM((EXT, KV_WIDTH), BF16),
        pltpu.VMEM((EXT, 2 * KV_WIDTH), BF16),
        pltpu.VMEM((EXT, 2 * KV_WIDTH), BF16),
        pltpu.VMEM((SB + 2 * CONV_HALO, A_WIDTH), F32),
        pltpu.VMEM((SB, d), F32),
        pltpu.VMEM((3, SB, d), F32),
        pltpu.VMEM((d, IN_WIDTH), BF16),
        pltpu.VMEM((d, 2 * MEM_WIDTH), BF16),
        pltpu.VMEM((A_WIDTH, d), BF16),
        pltpu.VMEM((ATTN_WIDTH, d), BF16),
        pltpu.VMEM((MEM_WIDTH, d), BF16),
        pltpu.VMEM((d, d), BF16),
        pltpu.VMEM((2, MEM_WIDTH, MEM_LEN), BF16),
        pltpu.VMEM((2, MEM_HEADS, MEM_LEN, 2 * MEM_HEAD_DIM), BF16),
        pltpu.VMEM((STAGE_SLOTS, STAGE_ROWS, STAGE_COLS), F32),
        pltpu.SemaphoreType.DMA((STAGE_SLOTS + 4,)),
        pltpu.VMEM((SB, d), BF16),
        pltpu.VMEM((SB, d), F32),
        pltpu.VMEM((OUT_SLOTS, OUT_ROWS, d), F32),
        pltpu.SemaphoreType.DMA((1,)),
    ]
    out = pl.pallas_call(
        _layer_kernel,
        grid=(bsz, s // TS),
        in_specs=in_specs,
        out_specs=hbm,
        out_shape=jax.ShapeDtypeStruct((bsz * s, d), x.dtype),
        scratch_shapes=scratch,
        compiler_params=pltpu.CompilerParams(dimension_semantics=("arbitrary", "arbitrary"),
                                             vmem_limit_bytes=VMEM_LIMIT_BYTES),
        name="hybrid_layer",
    )(attn_sink, x, x, x, g_pre.reshape(1, d), w_in, w_conv, cos_t, sin_t, bias_t, mem, g_mem.reshape(1, d),
      w_mem_kv, w_up_a, w_up_b, w_up_m, w_out, g_post.reshape(1, d))
    return out.reshape(bsz, s, d)


def kernel(x, mem, g_pre, w_in, w_conv, attn_sink, g_mem, w_mem_kv, w_up_a, w_up_b, w_up_m, w_out, g_post):
    cos_np, sin_np = _rope_tables()
    cos_t, sin_t, bias_t = jnp.asarray(cos_np), jnp.asarray(sin_np), jnp.asarray(_band_bias())
    for l in range(g_pre.shape[0]):
        x = _layer(x, mem, g_pre[l], w_in[l], w_conv[l], attn_sink[l], g_mem[l], w_mem_kv[l], w_up_a[l], w_up_b[l],
                   w_up_m[l], w_out[l], g_post[l], cos_t, sin_t, bias_t)
    return x
```

```python
import math

import numpy as np
import jax
import jax.numpy as jnp
from jax import lax
from jax.experimental import pallas as pl
from jax.experimental.pallas import tpu as pltpu

F32 = jnp.float32
BF16 = jnp.bfloat16

D_MODEL = 1024
SEQ = 4096
MEM_LEN = 256
EPS = 1e-6
CONV_WIDTH = 3
A_WIDTH = 512
HEAD_DIM = 64
ATTN_WIDTH = 512
N_Q_HEADS = 8
KV_WIDTH = 128
WINDOW = 128
ROPE_THETA = 500000.0
ROT_DIM = 16
MEM_HEADS = 4
MEM_HEAD_DIM = 128
MEM_WIDTH = 512

C_AB, C_AC, C_AX, C_AZ = 0, 512, 1024, 1536
C_BQ, C_BK, C_BV, C_BZ = 2048, 2560, 2688, 2816
C_MQ, C_MZ = 3328, 3840
C_G0, C_G1, C_G2 = 4352, 5376, 6400
IN_WIDTH = 7424

LANES = 128
BF16_ROWS = 16
VMEM_LIMIT_BYTES = 60 * 1024 * 1024

SB = 256
NSB = 2
TS = SB * NSB
QB = WINDOW
HALO = WINDOW
CONV_HALO = BF16_ROWS
EXT = TS + 2 * HALO
STAGE_SLOTS = 2
STAGE_ROWS = SB
STAGE_COLS = D_MODEL
OUT_ROWS = SB // 2
OUT_SLOTS = 4

LOG2E = math.log2(math.e)


def _sigmoid(v):
    return 1.0 / (1.0 + jnp.exp(-v))


def _silu(v):
    return v * _sigmoid(v)


def _rms(v, g):
    ms = jnp.mean(v * v, axis=-1, keepdims=True)
    return v * lax.rsqrt(ms + EPS) * g


def _dot(a, b):
    return jnp.dot(a, b, preferred_element_type=F32)


def _dot_nt(a, b):
    return lax.dot_general(a, b, (((1,), (1,)), ((), ())), preferred_element_type=F32)


def _rope(t, cos, ssin, low8):
    partner = jnp.where(low8, pltpu.roll(t, LANES - ROT_DIM // 2, 1), pltpu.roll(t, ROT_DIM // 2, 1))
    return t * cos + partner * ssin


def _rope_tables():
    half = ROT_DIM // 2
    inv_freq = np.power(np.float32(ROPE_THETA), -np.arange(half, dtype=np.float32) * np.float32(2.0 / ROT_DIM))
    pos = (np.arange(SEQ + 2 * HALO) - HALO).astype(np.float32)
    ang = (pos[:, None] * inv_freq[None, :]).astype(np.float32)
    cos, sin = np.cos(ang).astype(np.float32), np.sin(ang).astype(np.float32)
    ct = np.ones((SEQ + 2 * HALO, HEAD_DIM), np.float32)
    st = np.zeros((SEQ + 2 * HALO, HEAD_DIM), np.float32)
    ct[:, :half], ct[:, half:ROT_DIM] = cos, cos
    st[:, :half], st[:, half:ROT_DIM] = -sin, sin
    return np.tile(ct, (1, LANES // HEAD_DIM)), np.tile(st, (1, LANES // HEAD_DIM))


def _band_bias():
    r = np.arange(QB)[:, None]
    c = np.arange(3 * QB)[None, :]
    band = (c >= r) & (c <= r + 2 * WINDOW)
    masks = [band, band & (c >= QB), band & (c < 2 * QB)]
    return np.stack([np.where(m, 0.0, -np.inf) for m in masks]).astype(np.float32)


def _stream_cast(jobs, slots, sem):
    chunks = [(src, dst, r0, c0, min(STAGE_COLS, src.shape[1] - c0))
              for src, dst in jobs
              for r0 in range(0, src.shape[0], STAGE_ROWS)
              for c0 in range(0, src.shape[1], STAGE_COLS)]
    n_slots = len(slots)

    def copy(n):
        src, _, r0, c0, cols = chunks[n]
        k = n % n_slots
        return pltpu.make_async_copy(src.at[pl.ds(r0, STAGE_ROWS), pl.ds(c0, cols)],
                                     slots[k].at[:, pl.ds(0, cols)], sem.at[k])

    for n in range(min(n_slots, len(chunks))):
        copy(n).start()
    for n, (_, dst, r0, c0, cols) in enumerate(chunks):
        copy(n).wait()
        dst[r0:r0 + STAGE_ROWS, c0:c0 + cols] = slots[n % n_slots][:, 0:cols].astype(BF16)
        if n + n_slots < len(chunks):
            copy(n + n_slots).start()


def _window_softmax_pv(s, v_win, sink_col):
    m = jnp.maximum(jnp.max(s, axis=1, keepdims=True), sink_col)
    p = jnp.exp2(s - m)
    ov = _dot(p.astype(BF16), v_win)
    return ov[:, :LANES] / (ov[:, LANES:] + jnp.exp2(sink_col - m))


def _layer_kernel(sink_ref, x_ref, xp_ref, xn_ref, gpre_ref, win_hbm, wconv_ref, cos_ref, sin_ref, bias_ref,
                  mem_ref, gmem_ref, wmkv_hbm, wua_hbm, wub_hbm, wum_hbm, wout_hbm, gpost_ref,
                  out_hbm, h_s, k_s, ksw_s, v_s, vsw_s, cu_s, mrg_s, gate_s,
                  win_ref, wmkv_ref, wua_ref, wub_ref, wum_ref, wout_ref, mkt_ref, mvx_ref,
                  stage_s, stage_sem, hprev_s, xprev_s, out_s, out_sem, gate_ab_s):
    b = pl.program_id(0)
    i = pl.program_id(1)
    n_tiles = pl.num_programs(1)
    step = b * n_tiles + i
    last_step = pl.num_programs(0) * n_tiles - 1
    row0 = pl.multiple_of(i * TS, TS)
    tile_row = step * TS
    g_pre = gpre_ref[...]
    g_post = gpost_ref[...]

    def out_copy(slot, dst_row):
        return pltpu.make_async_copy(out_s.at[slot], out_hbm.at[pl.ds(dst_row, OUT_ROWS)], out_sem.at[0])

    pending_out = []

    def write_out(slot, value, dst_row):
        out_s[slot] = value
        pending_out.append((slot, dst_row))

    def flush_out():
        for slot, dst_row in pending_out:
            out_copy(slot, dst_row).start()
        pending_out.clear()

    def wait_out_slots(slots):
        for slot in slots:
            out_copy(slot, 0).wait()

    @pl.when(step == 0)
    def _first_step():
        slots = ([stage_s.at[k] for k in range(STAGE_SLOTS)] + [gate_s.at[k] for k in range(3)] + [mrg_s])
        _stream_cast([(win_hbm, win_ref), (wmkv_hbm, wmkv_ref), (wua_hbm, wua_ref), (wub_hbm, wub_ref),
                      (wum_hbm, wum_ref), (wout_hbm, wout_ref)], slots, stage_sem)
        mrg_s[...] = jnp.zeros(mrg_s.shape, F32)
        hprev_s[...] = jnp.zeros(hprev_s.shape, BF16)
        xprev_s[...] = jnp.zeros(xprev_s.shape, F32)
        out_s[...] = jnp.zeros(out_s.shape, F32)
        for slot in range(OUT_SLOTS):
            out_copy(slot, _first_rows(slot)).start()

    par = b % 2
    @pl.when(i == 0)
    def _memory_kv():
        mn = _rms(mem_ref[0], gmem_ref[...]).astype(BF16)
        kv = _dot(mn, wmkv_ref[...])
        mkt_ref[par] = kv[:, :MEM_WIDTH].T.astype(BF16)
        ones_m = jnp.ones((MEM_LEN, MEM_HEAD_DIM), F32)
        for hh in range(MEM_HEADS):
            c0 = MEM_WIDTH + hh * MEM_HEAD_DIM
            mvx_ref[par, hh] = jnp.concatenate([kv[:, c0:c0 + MEM_HEAD_DIM], ones_m], axis=1).astype(BF16)

    lane = lax.broadcasted_iota(jnp.int32, (1, LANES), 1)
    low_head = lane < HEAD_DIM
    low8 = (lane % HEAD_DIM) < (ROT_DIM // 2)

    rows4 = lax.broadcasted_iota(jnp.int32, (4 * QB, 1), 0)
    def sink_rows(h0, h1, h2, h3):
        return LOG2E * jnp.where(rows4 < QB, sink_ref[h0],
                                 jnp.where(rows4 < 2 * QB, sink_ref[h1],
                                           jnp.where(rows4 < 3 * QB, sink_ref[h2], sink_ref[h3])))
    sink_a = sink_rows(0, 2, 5, 7)
    sink_b = sink_rows(1, 3, 4, 6)
    n_blocks = SEQ // QB
    ones = jnp.ones((SB, LANES), BF16)

    def norm_main(sb):
        r0 = HALO + sb * SB
        h_s[r0:r0 + SB] = _rms(x_ref[0, sb * SB:(sb + 1) * SB], g_pre).astype(BF16)

    def norm_prev_halo():
        h_s[0:HALO] = _rms(xp_ref[0], g_pre).astype(BF16)

    def norm_next_halo():
        h_s[HALO + TS:EXT] = _rms(xn_ref[0], g_pre).astype(BF16)

    def kv_chunk(c):
        r0 = c * SB
        kv = _dot(h_s[r0:r0 + SB], win_ref[:, C_BK:C_BV + KV_WIDTH])
        cos = cos_ref[pl.ds(row0 + r0, SB), :]
        ssin = sin_ref[pl.ds(row0 + r0, SB), :]
        k = _rope(kv[:, :KV_WIDTH], cos, ssin, low8)
        v = kv[:, KV_WIDTH:]
        k_s[r0:r0 + SB] = k.astype(BF16)
        ksw_s[r0:r0 + SB] = pltpu.roll(k, HEAD_DIM, 1).astype(BF16)
        v_s[r0:r0 + SB, 0:LANES] = v.astype(BF16)
        v_s[r0:r0 + SB, LANES:2 * LANES] = ones
        vsw_s[r0:r0 + SB, 0:LANES] = pltpu.roll(v, HEAD_DIM, 1).astype(BF16)
        vsw_s[r0:r0 + SB, LANES:2 * LANES] = ones

    def memory_and_tail(h, x_rows, mem_par, dst_row, slots):
        mq = _dot(h, win_ref[:, C_MQ:C_MQ + MEM_WIDTH]) * ((MEM_HEAD_DIM ** -0.5) * LOG2E)
        m_zs = _silu(_dot(h, win_ref[:, C_MZ:C_MZ + MEM_WIDTH]))
        scores = []
        for hh in range(MEM_HEADS):
            hs = slice(hh * MEM_HEAD_DIM, (hh + 1) * MEM_HEAD_DIM)
            scores.append(_dot(mq[:, hs].astype(BF16), mkt_ref[mem_par, hs, :]))
        gate_s[2] = _sigmoid(_dot(h, win_ref[:, C_G2:C_G2 + D_MODEL]))
        ym_heads = []
        for hh in range(MEM_HEADS):
            s = scores[hh]
            p = jnp.exp2(s - jnp.max(s, axis=1, keepdims=True))
            ov = _dot(p.astype(BF16), mvx_ref[mem_par, hh])
            ym_heads.append(ov[:, :MEM_HEAD_DIM] / ov[:, MEM_HEAD_DIM:])
        ym = (jnp.concatenate(ym_heads, axis=1) * m_zs).astype(BF16)
        yield
        for half in range(SB // OUT_ROWS):
            rs = slice(half * OUT_ROWS, (half + 1) * OUT_ROWS)
            um = _dot(ym[rs], wum_ref[...])
            merged = mrg_s[rs] + gate_s[2, rs] * um
            o = _dot(merged.astype(BF16), wout_ref[...])
            write_out(slots[half], x_rows(rs) + _rms(o, g_post), dst_row + half * OUT_ROWS)
        yield

    def sub_block(sb):
        m0 = HALO + sb * SB

        tile_rows = slice(sb * SB, (sb + 1) * SB)

        def gate(col, slot):
            if sb == 0:
                gate_ab_s[slot] = _sigmoid(_dot(h_s[HALO:HALO + TS], win_ref[:, col:col + D_MODEL]))

        def conv_input():
            hx = h_s[m0 - CONV_HALO:m0 + SB + CONV_HALO]
            cx = _dot(hx, win_ref[:, C_AC:C_AX + A_WIDTH])
            cu = cx[:, :A_WIDTH] * cx[:, A_WIDTH:]
            first = row0 + sb * SB == 0
            last = row0 + (sb + 1) * SB == SEQ
            cu_s[0:CONV_HALO] = jnp.where(first, 0.0, cu[0:CONV_HALO])
            cu_s[CONV_HALO:CONV_HALO + SB] = cu[CONV_HALO:CONV_HALO + SB]
            cu_s[CONV_HALO + SB:] = jnp.where(last, 0.0, cu[CONV_HALO + SB:])

        norm_main(sb)
        yield
        h = h_s[m0:m0 + SB]
        q = _dot(h, win_ref[:, C_BQ:C_BQ + ATTN_WIDTH])
        b_zs = _silu(_dot(h, win_ref[:, C_BZ:C_BZ + ATTN_WIDTH]))
        yield
        a_b = _dot(h, win_ref[:, C_AB:C_AB + A_WIDTH])
        a_zs = _silu(_dot(h, win_ref[:, C_AZ:C_AZ + A_WIDTH]))
        scale = (HEAD_DIM ** -0.5) * LOG2E
        qcos = cos_ref[pl.ds(row0 + m0, SB), :] * scale
        qsin = sin_ref[pl.ds(row0 + m0, SB), :] * scale
        q_lo, q_hi = [], []
        for mblk in range(ATTN_WIDTH // LANES):
            qr = _rope(q[:, mblk * LANES:(mblk + 1) * LANES], qcos, qsin, low8)
            q_lo.append(jnp.where(low_head, qr, 0.0).astype(BF16))
            q_hi.append(jnp.where(low_head, 0.0, qr).astype(BF16))
        yield
        fillers = [lambda: gate(C_G1, 1), conv_input]
        yb_blocks = []
        for j in range(SB // QB):
            qs = slice(j * QB, (j + 1) * QB)
            jb = sb * (SB // QB) + j
            blk = i * (TS // QB) + jb
            variant = jnp.where(blk == 0, 1, jnp.where(blk == n_blocks - 1, 2, 0))
            bias1 = bias_ref[variant]
            bias = jnp.concatenate([bias1, bias1, bias1, bias1], axis=0)
            lhs_a = jnp.concatenate([q_lo[0][qs], q_lo[1][qs], q_hi[2][qs], q_hi[3][qs]], axis=0)
            lhs_b = jnp.concatenate([q_hi[0][qs], q_hi[1][qs], q_lo[2][qs], q_lo[3][qs]], axis=0)
            ws = slice(jb * QB, jb * QB + 3 * QB)
            s_a = _dot_nt(lhs_a, k_s[ws]) + bias
            s_b = _dot_nt(lhs_b, ksw_s[ws]) + bias
            fillers[j % len(fillers)]()
            o_a = _window_softmax_pv(s_a, v_s[ws], sink_a)
            o_b = _window_softmax_pv(s_b, vsw_s[ws], sink_b)
            cols = [jnp.where(low_head, o_a[0:QB], o_b[0:QB]),
                    jnp.where(low_head, o_a[QB:2 * QB], o_b[QB:2 * QB]),
                    jnp.where(low_head, o_b[2 * QB:3 * QB], o_a[2 * QB:3 * QB]),
                    jnp.where(low_head, o_b[3 * QB:4 * QB], o_a[3 * QB:4 * QB])]
            yb_blocks.append(jnp.concatenate(cols, axis=1))
            yield
        yb = jnp.concatenate(yb_blocks, axis=0) * b_zs
        ub = _dot(yb.astype(BF16), wub_ref[...])
        mrg_s[...] = gate_ab_s[1, tile_rows] * ub
        gate(C_G0, 0)
        y = (cu_s[CONV_HALO - 1:CONV_HALO - 1 + SB] * wconv_ref[0:1]
             + cu_s[CONV_HALO:CONV_HALO + SB] * wconv_ref[1:2]
             + cu_s[CONV_HALO + 1:CONV_HALO + 1 + SB] * wconv_ref[2:3])
        ya = a_b * y * a_zs
        ua = _dot(ya.astype(BF16), wua_ref[...])
        mrg_s[...] += gate_ab_s[0, tile_rows] * ua
        yield
        if sb + 1 < NSB:
            yield from memory_and_tail(h, lambda rs: x_ref[0, sb * SB + rs.start:sb * SB + rs.stop], par,
                                       tile_row + sb * SB, TAIL_SLOTS)
        else:
            hprev_s[...] = h
            xprev_s[...] = x_ref[0, sb * SB:(sb + 1) * SB]
            yield

    def deferred_tail(mem_par, dst_row):
        return memory_and_tail(hprev_s[...], lambda rs: xprev_s[rs], mem_par, dst_row, DEFERRED_SLOTS)

    gens = [sub_block(sb) for sb in range(NSB)]
    def run(sb, count=1):
        for _ in range(count):
            next(gens[sb])

    def norm_for_chunk(c):
        if c < NSB:
            run(c)
        else:
            norm_next_halo()

    wait_out_slots(range(OUT_SLOTS))
    prev_step = jnp.maximum(step - 1, 0)
    prev_par = (prev_step // n_tiles) % 2
    prev = deferred_tail(prev_par, prev_step * TS + (NSB - 1) * SB)
    next(prev)
    run(0)
    next(prev)
    run(0)
    norm_prev_halo()
    norm_for_chunk(1)
    kv_chunk(0)
    kv_chunk(1)
    for sb in range(NSB):
        run(sb)
        if sb > 0:
            run(sb - 1)
        run(sb)
        if sb == 0:
            flush_out()
        run(sb)
        if sb + 1 < NSB:
            norm_for_chunk(sb + 2)
        run(sb)
        if sb > 0:
            flush_out()
        run(sb)
        if sb + 1 < NSB:
            run(sb + 1)
            kv_chunk(sb + 2)
    assert not pending_out

    @pl.when(step == last_step)
    def _last_step():
        wait_out_slots(range(OUT_SLOTS))
        final = deferred_tail(par, tile_row + (NSB - 1) * SB)
        next(final)
        next(final)
        flush_out()
        wait_out_slots(DEFERRED_SLOTS)


DEFERRED_SLOTS = (0, 1)
TAIL_SLOTS = (2, 3)


def _first_rows(slot):
    if slot in DEFERRED_SLOTS:
        return (NSB - 1) * SB + DEFERRED_SLOTS.index(slot) * OUT_ROWS
    return TAIL_SLOTS.index(slot) * OUT_ROWS


def _resident(shape):
    return pl.BlockSpec(shape, lambda b, i: (0,) * len(shape), pipeline_mode=pl.Buffered(1))


def _layer(x, mem, g_pre, w_in, w_conv, attn_sink, g_mem, w_mem_kv, w_up_a, w_up_b, w_up_m, w_out, g_post,
           cos_t, sin_t, bias_t):
    bsz, s, d = x.shape
    assert (s, d) == (SEQ, D_MODEL) and s % TS == 0 and SB % QB == 0 and SB == 2 * HALO
    assert (STAGE_ROWS, STAGE_COLS) == (SB, d)
    assert NSB == 2 and OUT_SLOTS == len(DEFERRED_SLOTS) + len(TAIL_SLOTS) and SB // OUT_ROWS == 2
    nb = TS // HALO
    hbm = pl.BlockSpec(memory_space=pl.ANY)
    in_specs = [
        pl.BlockSpec(memory_space=pltpu.SMEM),
        pl.BlockSpec((1, TS, d), lambda b, i: (b, i, 0)),
        pl.BlockSpec((1, HALO, d), lambda b, i: (b, jnp.maximum(i * nb - 1, 0), 0)),
        pl.BlockSpec((1, HALO, d), lambda b, i: (b, jnp.minimum((i + 1) * nb, s // HALO - 1), 0)),
        _resident((1, d)),
        hbm,
        _resident((CONV_WIDTH, A_WIDTH)),
        _resident((s + 2 * HALO, LANES)),
        _resident((s + 2 * HALO, LANES)),
        _resident((3, QB, 3 * QB)),
        pl.BlockSpec((1, MEM_LEN, d), lambda b, i: (b, 0, 0)),
        _resident((1, d)),
        hbm, hbm, hbm, hbm, hbm,
        _resident((1, d)),
    ]
    scratch = [
        pltpu.VMEM((EXT, d), BF16),
        pltpu.VMEM((EXT, KV_WIDTH), BF16),
        pltpu.VMEM((EXT, KV_WIDTH), BF16),
        pltpu.VMEM((EXT, 2 * KV_WIDTH), BF16),
        pltpu.VMEM((EXT, 2 * KV_WIDTH), BF16),
        pltpu.VMEM((SB + 2 * CONV_HALO, A_WIDTH), F32),
        pltpu.VMEM((SB, d), F32),
        pltpu.VMEM((3, SB, d), F32),
        pltpu.VMEM((d, IN_WIDTH), BF16),
        pltpu.VMEM((d, 2 * MEM_WIDTH), BF16),
        pltpu.VMEM((A_WIDTH, d), BF16),
        pltpu.VMEM((ATTN_WIDTH, d), BF16),
        pltpu.VMEM((MEM_WIDTH, d), BF16),
        pltpu.VMEM((d, d), BF16),
        pltpu.VMEM((2, MEM_WIDTH, MEM_LEN), BF16),
        pltpu.VMEM((2, MEM_HEADS, MEM_LEN, 2 * MEM_HEAD_DIM), BF16),
        pltpu.VMEM((STAGE_SLOTS, STAGE_ROWS, STAGE_COLS), F32),
        pltpu.SemaphoreType.DMA((STAGE_SLOTS + 4,)),
        pltpu.VMEM((SB, d), BF16),
        pltpu.VMEM((SB, d), F32),
        pltpu.VMEM((OUT_SLOTS, OUT_ROWS, d), F32),
        pltpu.SemaphoreType.DMA((1,)),
        pltpu.VMEM((2, TS, d), F32),
    ]
    out = pl.pallas_call(
        _layer_kernel,
        grid=(bsz, s // TS),
        in_specs=in_specs,
        out_specs=hbm,
        out_shape=jax.ShapeDtypeStruct((bsz * s, d), x.dtype),
        scratch_shapes=scratch,
        compiler_params=pltpu.CompilerParams(dimension_semantics=("arbitrary", "arbitrary"),
                                             vmem_limit_bytes=VMEM_LIMIT_BYTES),
        name="hybrid_layer",
    )(attn_sink, x, x, x, g_pre.reshape(1, d), w_in, w_conv, cos_t, sin_t, bias_t, mem, g_mem.reshape(1, d),
      w_mem_kv, w_up_a, w_up_b, w_up_m, w_out, g_post.reshape(1, d))
    return out.reshape(bsz, s, d)


def kernel(x, mem, g_pre, w_in, w_conv, attn_sink, g_mem, w_mem_kv, w_up_a, w_up_b, w_up_m, w_out, g_post):
    cos_np, sin_np = _rope_tables()
    cos_t, sin_t, bias_t = jnp.asarray(cos_np), jnp.asarray(sin_np), jnp.asarray(_band_bias())
    for l in range(g_pre.shape[0]):
        x = _layer(x, mem, g_pre[l], w_in[l], w_conv[l], attn_sink[l], g_mem[l], w_mem_kv[l], w_up_a[l], w_up_b[l],
                   w_up_m[l], w_out[l], g_post[l], cos_t, sin_t, bias_t)
    return x
```

```python
import math

import numpy as np
import jax
import jax.numpy as jnp
from jax import lax
from jax.experimental import pallas as pl
from jax.experimental.pallas import tpu as pltpu

F32 = jnp.float32
BF16 = jnp.bfloat16

D_MODEL = 1024
SEQ = 4096
MEM_LEN = 256
EPS = 1e-6
CONV_WIDTH = 3
A_WIDTH = 512
HEAD_DIM = 64
ATTN_WIDTH = 512
N_Q_HEADS = 8
KV_WIDTH = 128
WINDOW = 128
ROPE_THETA = 500000.0
ROT_DIM = 16
MEM_HEADS = 4
MEM_HEAD_DIM = 128
MEM_WIDTH = 512

C_AB, C_AC, C_AX, C_AZ = 0, 512, 1024, 1536
C_BQ, C_BK, C_BV, C_BZ = 2048, 2560, 2688, 2816
C_MQ, C_MZ = 3328, 3840
C_G0, C_G1, C_G2 = 4352, 5376, 6400
IN_WIDTH = 7424

LANES = 128
BF16_ROWS = 16
VMEM_LIMIT_BYTES = 62 * 1024 * 1024

SB = 256
NSB = 4
TS = SB * NSB
QB = WINDOW
HALO = WINDOW
CONV_HALO = BF16_ROWS
EXT = TS + 2 * HALO
STAGE_SLOTS = 2
STAGE_ROWS = SB
STAGE_COLS = D_MODEL
OUT_ROWS = SB // 2
OUT_SLOTS = TS // OUT_ROWS

LOG2E = math.log2(math.e)


def _sigmoid(v):
    return 1.0 / (1.0 + jnp.exp(-v))


def _silu(v):
    return v * _sigmoid(v)


def _rms(v, g):
    ms = jnp.mean(v * v, axis=-1, keepdims=True)
    return v * lax.rsqrt(ms + EPS) * g


def _dot(a, b):
    return jnp.dot(a, b, preferred_element_type=F32)


def _dot_nt(a, b):
    return lax.dot_general(a, b, (((1,), (1,)), ((), ())), preferred_element_type=F32)


def _rope(t, cos, ssin, low8):
    partner = jnp.where(low8, pltpu.roll(t, LANES - ROT_DIM // 2, 1), pltpu.roll(t, ROT_DIM // 2, 1))
    return t * cos + partner * ssin


def _rope_tables():
    half = ROT_DIM // 2
    inv_freq = np.power(np.float32(ROPE_THETA), -np.arange(half, dtype=np.float32) * np.float32(2.0 / ROT_DIM))
    pos = (np.arange(SEQ + 2 * HALO) - HALO).astype(np.float32)
    ang = (pos[:, None] * inv_freq[None, :]).astype(np.float32)
    cos, sin = np.cos(ang).astype(np.float32), np.sin(ang).astype(np.float32)
    ct = np.ones((SEQ + 2 * HALO, HEAD_DIM), np.float32)
    st = np.zeros((SEQ + 2 * HALO, HEAD_DIM), np.float32)
    ct[:, :half], ct[:, half:ROT_DIM] = cos, cos
    st[:, :half], st[:, half:ROT_DIM] = -sin, sin
    return np.tile(ct, (1, LANES // HEAD_DIM)), np.tile(st, (1, LANES // HEAD_DIM))


def _band_bias():
    r = np.arange(QB)[:, None]
    c = np.arange(3 * QB)[None, :]
    band = (c >= r) & (c <= r + 2 * WINDOW)
    masks = [band, band & (c >= QB), band & (c < 2 * QB)]
    return np.stack([np.where(m, 0.0, -np.inf) for m in masks]).astype(np.float32)


def _stream_cast(jobs, slots, sem):
    chunks = [(src, dst, r0, c0, min(STAGE_COLS, src.shape[1] - c0))
              for src, dst in jobs
              for r0 in range(0, src.shape[0], STAGE_ROWS)
              for c0 in range(0, src.shape[1], STAGE_COLS)]
    n_slots = len(slots)

    def copy(n):
        src, _, r0, c0, cols = chunks[n]
        k = n % n_slots
        return pltpu.make_async_copy(src.at[pl.ds(r0, STAGE_ROWS), pl.ds(c0, cols)],
                                     slots[k].at[:, pl.ds(0, cols)], sem.at[k])

    for n in range(min(n_slots, len(chunks))):
        copy(n).start()
    for n, (_, dst, r0, c0, cols) in enumerate(chunks):
        copy(n).wait()
        dst[r0:r0 + STAGE_ROWS, c0:c0 + cols] = slots[n % n_slots][:, 0:cols].astype(BF16)
        if n + n_slots < len(chunks):
            copy(n + n_slots).start()


def _window_softmax_pv(s, v_win, sink_col):
    m = jnp.maximum(jnp.max(s, axis=1, keepdims=True), sink_col)
    p = jnp.exp2(s - m)
    ov = _dot(p.astype(BF16), v_win)
    return ov[:, :LANES] / (ov[:, LANES:] + jnp.exp2(sink_col - m))


def _layer_kernel(sink_ref, x_ref, xp_ref, xn_ref, gpre_ref, win_hbm, wconv_ref, cos_ref, sin_ref, bias_ref,
                  mem_ref, gmem_ref, wmkv_hbm, wua_hbm, wub_hbm, wum_hbm, wout_hbm, gpost_ref,
                  out_hbm, h_s, k_s, ksw_s, v_s, vsw_s, cu_s, mrg_s, gate_s,
                  win_ref, wmkv_ref, wua_ref, wub_ref, wum_ref, wout_ref, mkt_ref, mvx_ref,
                  stage_s, stage_sem, out_s, out_sem):
    b = pl.program_id(0)
    i = pl.program_id(1)
    n_tiles = pl.num_programs(1)
    step = b * n_tiles + i
    last_step = pl.num_programs(0) * n_tiles - 1
    row0 = pl.multiple_of(i * TS, TS)
    tile_row = step * TS
    g_pre = gpre_ref[...]
    g_post = gpost_ref[...]

    def out_copy(slot, dst_row):
        return pltpu.make_async_copy(out_s.at[slot], out_hbm.at[pl.ds(dst_row, OUT_ROWS)], out_sem.at[0])

    pending_out = []

    def write_out(slot, value, dst_row):
        out_s[slot] = value
        pending_out.append((slot, dst_row))

    def flush_out():
        for slot, dst_row in pending_out:
            out_copy(slot, dst_row).start()
        pending_out.clear()

    def wait_out_slots():
        for slot in range(OUT_SLOTS):
            out_copy(slot, 0).wait()

    @pl.when(step == 0)
    def _first_step():
        slots = ([stage_s.at[k] for k in range(STAGE_SLOTS)] + [gate_s.at[k] for k in range(3)] + [mrg_s])
        _stream_cast([(win_hbm, win_ref), (wmkv_hbm, wmkv_ref), (wua_hbm, wua_ref), (wub_hbm, wub_ref),
                      (wum_hbm, wum_ref), (wout_hbm, wout_ref)], slots, stage_sem)
        out_s[...] = jnp.zeros(out_s.shape, F32)
        for slot in range(OUT_SLOTS):
            out_copy(slot, slot * OUT_ROWS).start()

    @pl.when(i == 0)
    def _memory_kv():
        mn = _rms(mem_ref[0], gmem_ref[...]).astype(BF16)
        kv = _dot(mn, wmkv_ref[...])
        mkt_ref[...] = kv[:, :MEM_WIDTH].T.astype(BF16)
        ones_m = jnp.ones((MEM_LEN, MEM_HEAD_DIM), F32)
        for hh in range(MEM_HEADS):
            c0 = MEM_WIDTH + hh * MEM_HEAD_DIM
            mvx_ref[hh] = jnp.concatenate([kv[:, c0:c0 + MEM_HEAD_DIM], ones_m], axis=1).astype(BF16)

    lane = lax.broadcasted_iota(jnp.int32, (1, LANES), 1)
    low_head = lane < HEAD_DIM
    low8 = (lane % HEAD_DIM) < (ROT_DIM // 2)

    rows4 = lax.broadcasted_iota(jnp.int32, (4 * QB, 1), 0)
    def sink_rows(h0, h1, h2, h3):
        return LOG2E * jnp.where(rows4 < QB, sink_ref[h0],
                                 jnp.where(rows4 < 2 * QB, sink_ref[h1],
                                           jnp.where(rows4 < 3 * QB, sink_ref[h2], sink_ref[h3])))
    sink_a = sink_rows(0, 2, 5, 7)
    sink_b = sink_rows(1, 3, 4, 6)
    n_blocks = SEQ // QB
    ones = jnp.ones((SB, LANES), BF16)

    def norm_main(sb):
        r0 = HALO + sb * SB
        h_s[r0:r0 + SB] = _rms(x_ref[0, sb * SB:(sb + 1) * SB], g_pre).astype(BF16)

    def norm_prev_halo():
        h_s[0:HALO] = _rms(xp_ref[0], g_pre).astype(BF16)

    def norm_next_halo():
        h_s[HALO + TS:EXT] = _rms(xn_ref[0], g_pre).astype(BF16)

    def kv_chunk(c):
        r0 = c * SB
        kv = _dot(h_s[r0:r0 + SB], win_ref[:, C_BK:C_BV + KV_WIDTH])
        cos = cos_ref[pl.ds(row0 + r0, SB), :]
        ssin = sin_ref[pl.ds(row0 + r0, SB), :]
        k = _rope(kv[:, :KV_WIDTH], cos, ssin, low8)
        v = kv[:, KV_WIDTH:]
        k_s[r0:r0 + SB] = k.astype(BF16)
        ksw_s[r0:r0 + SB] = pltpu.roll(k, HEAD_DIM, 1).astype(BF16)
        v_s[r0:r0 + SB, 0:LANES] = v.astype(BF16)
        v_s[r0:r0 + SB, LANES:2 * LANES] = ones
        vsw_s[r0:r0 + SB, 0:LANES] = pltpu.roll(v, HEAD_DIM, 1).astype(BF16)
        vsw_s[r0:r0 + SB, LANES:2 * LANES] = ones

    def sub_block(sb):
        m0 = HALO + sb * SB

        def gate(col, slot):
            gate_s[slot] = _sigmoid(_dot(h, win_ref[:, col:col + D_MODEL]))

        def conv_input():
            hx = h_s[m0 - CONV_HALO:m0 + SB + CONV_HALO]
            cx = _dot(hx, win_ref[:, C_AC:C_AX + A_WIDTH])
            cu = cx[:, :A_WIDTH] * cx[:, A_WIDTH:]
            first = row0 + sb * SB == 0
            last = row0 + (sb + 1) * SB == SEQ
            cu_s[0:CONV_HALO] = jnp.where(first, 0.0, cu[0:CONV_HALO])
            cu_s[CONV_HALO:CONV_HALO + SB] = cu[CONV_HALO:CONV_HALO + SB]
            cu_s[CONV_HALO + SB:] = jnp.where(last, 0.0, cu[CONV_HALO + SB:])

        norm_main(sb)
        yield
        h = h_s[m0:m0 + SB]
        q = _dot(h, win_ref[:, C_BQ:C_BQ + ATTN_WIDTH])
        b_zs = _silu(_dot(h, win_ref[:, C_BZ:C_BZ + ATTN_WIDTH]))
        yield
        a_b = _dot(h, win_ref[:, C_AB:C_AB + A_WIDTH])
        a_zs = _silu(_dot(h, win_ref[:, C_AZ:C_AZ + A_WIDTH]))
        scale = (HEAD_DIM ** -0.5) * LOG2E
        qcos = cos_ref[pl.ds(row0 + m0, SB), :] * scale
        qsin = sin_ref[pl.ds(row0 + m0, SB), :] * scale
        q_lo, q_hi = [], []
        for mblk in range(ATTN_WIDTH // LANES):
            qr = _rope(q[:, mblk * LANES:(mblk + 1) * LANES], qcos, qsin, low8)
            q_lo.append(jnp.where(low_head, qr, 0.0).astype(BF16))
            q_hi.append(jnp.where(low_head, 0.0, qr).astype(BF16))
        yield
        fillers = [lambda: gate(C_G1, 1), conv_input]
        yb_blocks = []
        for j in range(SB // QB):
            qs = slice(j * QB, (j + 1) * QB)
            jb = sb * (SB // QB) + j
            blk = i * (TS // QB) + jb
            variant = jnp.where(blk == 0, 1, jnp.where(blk == n_blocks - 1, 2, 0))
            bias1 = bias_ref[variant]
            bias = jnp.concatenate([bias1, bias1, bias1, bias1], axis=0)
            lhs_a = jnp.concatenate([q_lo[0][qs], q_lo[1][qs], q_hi[2][qs], q_hi[3][qs]], axis=0)
            lhs_b = jnp.concatenate([q_hi[0][qs], q_hi[1][qs], q_lo[2][qs], q_lo[3][qs]], axis=0)
            ws = slice(jb * QB, jb * QB + 3 * QB)
            s_a = _dot_nt(lhs_a, k_s[ws]) + bias
            s_b = _dot_nt(lhs_b, ksw_s[ws]) + bias
            fillers[j % len(fillers)]()
            o_a = _window_softmax_pv(s_a, v_s[ws], sink_a)
            o_b = _window_softmax_pv(s_b, vsw_s[ws], sink_b)
            cols = [jnp.where(low_head, o_a[0:QB], o_b[0:QB]),
                    jnp.where(low_head, o_a[QB:2 * QB], o_b[QB:2 * QB]),
                    jnp.where(low_head, o_b[2 * QB:3 * QB], o_a[2 * QB:3 * QB]),
                    jnp.where(low_head, o_b[3 * QB:4 * QB], o_a[3 * QB:4 * QB])]
            yb_blocks.append(jnp.concatenate(cols, axis=1))
            yield
        yb = jnp.concatenate(yb_blocks, axis=0) * b_zs
        ub = _dot(yb.astype(BF16), wub_ref[...])
        mrg_s[...] = gate_s[1] * ub
        gate(C_G0, 0)
        y = (cu_s[CONV_HALO - 1:CONV_HALO - 1 + SB] * wconv_ref[0:1]
             + cu_s[CONV_HALO:CONV_HALO + SB] * wconv_ref[1:2]
             + cu_s[CONV_HALO + 1:CONV_HALO + 1 + SB] * wconv_ref[2:3])
        ya = a_b * y * a_zs
        ua = _dot(ya.astype(BF16), wua_ref[...])
        mrg_s[...] += gate_s[0] * ua
        yield
        mq = _dot(h, win_ref[:, C_MQ:C_MQ + MEM_WIDTH]) * ((MEM_HEAD_DIM ** -0.5) * LOG2E)
        m_zs = _silu(_dot(h, win_ref[:, C_MZ:C_MZ + MEM_WIDTH]))
        scores = []
        for hh in range(MEM_HEADS):
            hs = slice(hh * MEM_HEAD_DIM, (hh + 1) * MEM_HEAD_DIM)
            scores.append(_dot(mq[:, hs].astype(BF16), mkt_ref[hs, :]))
        gate(C_G2, 2)
        ym_heads = []
        for hh in range(MEM_HEADS):
            s = scores[hh]
            p = jnp.exp2(s - jnp.max(s, axis=1, keepdims=True))
            ov = _dot(p.astype(BF16), mvx_ref[hh])
            ym_heads.append(ov[:, :MEM_HEAD_DIM] / ov[:, MEM_HEAD_DIM:])
        ym = (jnp.concatenate(ym_heads, axis=1) * m_zs).astype(BF16)
        yield
        for half in range(SB // OUT_ROWS):
            rs = slice(half * OUT_ROWS, (half + 1) * OUT_ROWS)
            t0 = sb * SB + half * OUT_ROWS
            um = _dot(ym[rs], wum_ref[...])
            merged = mrg_s[rs] + gate_s[2, rs] * um
            o = _dot(merged.astype(BF16), wout_ref[...])
            write_out(t0 // OUT_ROWS, x_ref[0, t0:t0 + OUT_ROWS] + _rms(o, g_post), tile_row + t0)
        yield

    gens = [sub_block(sb) for sb in range(NSB)]
    def run(sb, count=1):
        for _ in range(count):
            next(gens[sb])

    def norm_for_chunk(c):
        if c < NSB:
            run(c)
        else:
            norm_next_halo()

    wait_out_slots()
    run(0, 2)
    norm_prev_halo()
    norm_for_chunk(1)
    kv_chunk(0)
    kv_chunk(1)
    for sb in range(NSB):
        run(sb)
        if sb > 0:
            run(sb - 1)
        run(sb, 2)
        if sb + 1 < NSB:
            norm_for_chunk(sb + 2)
        run(sb)
        flush_out()
        run(sb)
        if sb + 1 < NSB:
            run(sb + 1)
            kv_chunk(sb + 2)
    run(NSB - 1)
    flush_out()

    @pl.when(step == last_step)
    def _last_step():
        wait_out_slots()


def _resident(shape):
    return pl.BlockSpec(shape, lambda b, i: (0,) * len(shape), pipeline_mode=pl.Buffered(1))


def _layer(x, mem, g_pre, w_in, w_conv, attn_sink, g_mem, w_mem_kv, w_up_a, w_up_b, w_up_m, w_out, g_post,
           cos_t, sin_t, bias_t):
    bsz, s, d = x.shape
    assert (s, d) == (SEQ, D_MODEL) and s % TS == 0 and SB % QB == 0 and SB == 2 * HALO
    assert (STAGE_ROWS, STAGE_COLS) == (SB, d)
    nb = TS // HALO
    hbm = pl.BlockSpec(memory_space=pl.ANY)
    in_specs = [
        pl.BlockSpec(memory_space=pltpu.SMEM),
        pl.BlockSpec((1, TS, d), lambda b, i: (b, i, 0)),
        pl.BlockSpec((1, HALO, d), lambda b, i: (b, jnp.maximum(i * nb - 1, 0), 0)),
        pl.BlockSpec((1, HALO, d), lambda b, i: (b, jnp.minimum((i + 1) * nb, s // HALO - 1), 0)),
        _resident((1, d)),
        hbm,
        _resident((CONV_WIDTH, A_WIDTH)),
        _resident((s + 2 * HALO, LANES)),
        _resident((s + 2 * HALO, LANES)),
        _resident((3, QB, 3 * QB)),
        pl.BlockSpec((1, MEM_LEN, d), lambda b, i: (b, 0, 0)),
        _resident((1, d)),
        hbm, hbm, hbm, hbm, hbm,
        _resident((1, d)),
    ]
    scratch = [
        pltpu.VMEM((EXT, d), BF16),
        pltpu.VMEM((EXT, KV_WIDTH), BF16),
        pltpu.VMEM((EXT, KV_WIDTH), BF16),
        pltpu.VMEM((EXT, 2 * KV_WIDTH), BF16),
        pltpu.VMEM((EXT, 2 * KV_WIDTH), BF16),
        pltpu.VMEM((SB + 2 * CONV_HALO, A_WIDTH), F32),
        pltpu.VMEM((SB, d), F32),
        pltpu.VMEM((3, SB, d), F32),
        pltpu.VMEM((d, IN_WIDTH), BF16),
        pltpu.VMEM((d, 2 * MEM_WIDTH), BF16),
        pltpu.VMEM((A_WIDTH, d), BF16),
        pltpu.VMEM((ATTN_WIDTH, d), BF16),
        pltpu.VMEM((MEM_WIDTH, d), BF16),
        pltpu.VMEM((d, d), BF16),
        pltpu.VMEM((MEM_WIDTH, MEM_LEN), BF16),
        pltpu.VMEM((MEM_HEADS, MEM_LEN, 2 * MEM_HEAD_DIM), BF16),
        pltpu.VMEM((STAGE_SLOTS, STAGE_ROWS, STAGE_COLS), F32),
        pltpu.SemaphoreType.DMA((STAGE_SLOTS + 4,)),
        pltpu.VMEM((OUT_SLOTS, OUT_ROWS, d), F32),
        pltpu.SemaphoreType.DMA((1,)),
    ]
    out = pl.pallas_call(
        _layer_kernel,
        grid=(bsz, s // TS),
        in_specs=in_specs,
        out_specs=hbm,
        out_shape=jax.ShapeDtypeStruct((bsz * s, d), x.dtype),
        scratch_shapes=scratch,
        compiler_params=pltpu.CompilerParams(dimension_semantics=("arbitrary", "arbitrary"),
                                             vmem_limit_bytes=VMEM_LIMIT_BYTES),
        name="hybrid_layer",
    )(attn_sink, x, x, x, g_pre.reshape(1, d), w_in, w_conv, cos_t, sin_t, bias_t, mem, g_mem.reshape(1, d),
      w_mem_kv, w_up_a, w_up_b, w_up_m, w_out, g_post.reshape(1, d))
    return out.reshape(bsz, s, d)


def kernel(x, mem, g_pre, w_in, w_conv, attn_sink, g_mem, w_mem_kv, w_up_a, w_up_b, w_up_m, w_out, g_post):
    cos_np, sin_np = _rope_tables()
    cos_t, sin_t, bias_t = jnp.asarray(cos_np), jnp.asarray(sin_np), jnp.asarray(_band_bias())
    for l in range(g_pre.shape[0]):
        x = _layer(x, mem, g_pre[l], w_in[l], w_conv[l], attn_sink[l], g_mem[l], w_mem_kv[l], w_up_a[l], w_up_b[l],
                   w_up_m[l], w_out[l], g_post[l], cos_t, sin_t, bias_t)
    return x
```

```python
import math

import numpy as np
import jax
import jax.numpy as jnp
from jax import lax
from jax.experimental import pallas as pl
from jax.experimental.pallas import tpu as pltpu

F32 = jnp.float32
BF16 = jnp.bfloat16

D_MODEL = 1024
SEQ = 4096
MEM_LEN = 256
EPS = 1e-6
CONV_WIDTH = 3
A_WIDTH = 512
HEAD_DIM = 64
ATTN_WIDTH = 512
N_Q_HEADS = 8
KV_WIDTH = 128
WINDOW = 128
ROPE_THETA = 500000.0
ROT_DIM = 16
MEM_HEADS = 4
MEM_HEAD_DIM = 128
MEM_WIDTH = 512

C_AB, C_AC, C_AX, C_AZ = 0, 512, 1024, 1536
C_BQ, C_BK, C_BV, C_BZ = 2048, 2560, 2688, 2816
C_MQ, C_MZ = 3328, 3840
C_G0, C_G1, C_G2 = 4352, 5376, 6400
IN_WIDTH = 7424

LANES = 128
BF16_ROWS = 16
VMEM_LIMIT_BYTES = 62 * 1024 * 1024

SB = 256
NSB = 4
TS = SB * NSB
QB = WINDOW
HALO = WINDOW
CONV_HALO = BF16_ROWS
EXT = TS + 2 * HALO
STAGE_SLOTS = 2
STAGE_ROWS = SB
STAGE_COLS = D_MODEL
OUT_ROWS = SB // 2
OUT_SLOTS = TS // OUT_ROWS

LOG2E = math.log2(math.e)


def _sigmoid(v):
    return 1.0 / (1.0 + jnp.exp(-v))


def _silu(v):
    return v * _sigmoid(v)


def _rms(v, g):
    ms = jnp.mean(v * v, axis=-1, keepdims=True)
    return v * lax.rsqrt(ms + EPS) * g


def _dot(a, b):
    return jnp.dot(a, b, preferred_element_type=F32)


def _dot_nt(a, b):
    return lax.dot_general(a, b, (((1,), (1,)), ((), ())), preferred_element_type=F32)


def _rope(t, cos, ssin, low8):
    partner = jnp.where(low8, pltpu.roll(t, LANES - ROT_DIM // 2, 1), pltpu.roll(t, ROT_DIM // 2, 1))
    return t * cos + partner * ssin


def _rope_tables():
    half = ROT_DIM // 2
    inv_freq = np.power(np.float32(ROPE_THETA), -np.arange(half, dtype=np.float32) * np.float32(2.0 / ROT_DIM))
    pos = (np.arange(SEQ + 2 * HALO) - HALO).astype(np.float32)
    ang = (pos[:, None] * inv_freq[None, :]).astype(np.float32)
    cos, sin = np.cos(ang).astype(np.float32), np.sin(ang).astype(np.float32)
    ct = np.ones((SEQ + 2 * HALO, HEAD_DIM), np.float32)
    st = np.zeros((SEQ + 2 * HALO, HEAD_DIM), np.float32)
    ct[:, :half], ct[:, half:ROT_DIM] = cos, cos
    st[:, :half], st[:, half:ROT_DIM] = -sin, sin
    return np.tile(ct, (1, LANES // HEAD_DIM)), np.tile(st, (1, LANES // HEAD_DIM))


def _band_bias():
    r = np.arange(QB)[:, None]
    c = np.arange(3 * QB)[None, :]
    band = (c >= r) & (c <= r + 2 * WINDOW)
    masks = [band, band & (c >= QB), band & (c < 2 * QB)]
    return np.stack([np.where(m, 0.0, -np.inf) for m in masks]).astype(np.float32)


def _stream_cast(jobs, slots, sem):
    chunks = [(src, dst, r0, c0, min(STAGE_COLS, src.shape[1] - c0))
              for src, dst in jobs
              for r0 in range(0, src.shape[0], STAGE_ROWS)
              for c0 in range(0, src.shape[1], STAGE_COLS)]
    n_slots = len(slots)

    def copy(n):
        src, _, r0, c0, cols = chunks[n]
        k = n % n_slots
        return pltpu.make_async_copy(src.at[pl.ds(r0, STAGE_ROWS), pl.ds(c0, cols)],
                                     slots[k].at[:, pl.ds(0, cols)], sem.at[k])

    for n in range(min(n_slots, len(chunks))):
        copy(n).start()
    for n, (_, dst, r0, c0, cols) in enumerate(chunks):
        copy(n).wait()
        dst[r0:r0 + STAGE_ROWS, c0:c0 + cols] = slots[n % n_slots][:, 0:cols].astype(BF16)
        if n + n_slots < len(chunks):
            copy(n + n_slots).start()


def _window_softmax_pv(s, v_win, sink_col):
    m = jnp.maximum(jnp.max(s, axis=1, keepdims=True), sink_col)
    p = jnp.exp2(s - m)
    ov = _dot(p.astype(BF16), v_win)
    return ov[:, :LANES] / (ov[:, LANES:] + jnp.exp2(sink_col - m))


def _layer_kernel(sink_ref, x_ref, xn_ref, gpre_ref, win_hbm, wconv_ref, cos_ref, sin_ref, bias_ref,
                  mem_ref, gmem_ref, wmkv_hbm, wua_hbm, wub_hbm, wum_hbm, wout_hbm, gpost_ref,
                  out_hbm, h_s, k_s, ksw_s, v_s, vsw_s, cu_s, mrg_s, gate_s,
                  win_ref, wmkv_ref, wua_ref, wub_ref, wum_ref, wout_ref, mkt_ref, mvx_ref,
                  stage_s, stage_sem, out_s, out_sem):
    b = pl.program_id(0)
    i = pl.program_id(1)
    n_tiles = pl.num_programs(1)
    step = b * n_tiles + i
    last_step = pl.num_programs(0) * n_tiles - 1
    row0 = pl.multiple_of(i * TS, TS)
    tile_row = step * TS
    g_pre = gpre_ref[...]
    g_post = gpost_ref[...]

    def out_copy(slot, dst_row):
        return pltpu.make_async_copy(out_s.at[slot], out_hbm.at[pl.ds(dst_row, OUT_ROWS)], out_sem.at[0])

    pending_out = []

    def write_out(slot, value, dst_row):
        out_s[slot] = value
        pending_out.append((slot, dst_row))

    def flush_out():
        for slot, dst_row in pending_out:
            out_copy(slot, dst_row).start()
        pending_out.clear()

    def wait_out_slots():
        for slot in range(OUT_SLOTS):
            out_copy(slot, 0).wait()

    @pl.when(step == 0)
    def _first_step():
        slots = ([stage_s.at[k] for k in range(STAGE_SLOTS)] + [gate_s.at[k] for k in range(3)] + [mrg_s])
        _stream_cast([(win_hbm, win_ref), (wmkv_hbm, wmkv_ref), (wua_hbm, wua_ref), (wub_hbm, wub_ref),
                      (wum_hbm, wum_ref), (wout_hbm, wout_ref)], slots, stage_sem)
        for buf in (k_s, ksw_s, v_s, vsw_s):
            buf[TS:TS + HALO] = jnp.zeros((HALO, buf.shape[1]), BF16)
        cu_s[...] = jnp.zeros(cu_s.shape, F32)
        out_s[...] = jnp.zeros(out_s.shape, F32)
        for slot in range(OUT_SLOTS):
            out_copy(slot, slot * OUT_ROWS).start()

    @pl.when(i == 0)
    def _memory_kv():
        mn = _rms(mem_ref[0], gmem_ref[...]).astype(BF16)
        kv = _dot(mn, wmkv_ref[...])
        mkt_ref[...] = kv[:, :MEM_WIDTH].T.astype(BF16)
        ones_m = jnp.ones((MEM_LEN, MEM_HEAD_DIM), F32)
        for hh in range(MEM_HEADS):
            c0 = MEM_WIDTH + hh * MEM_HEAD_DIM
            mvx_ref[hh] = jnp.concatenate([kv[:, c0:c0 + MEM_HEAD_DIM], ones_m], axis=1).astype(BF16)

    lane = lax.broadcasted_iota(jnp.int32, (1, LANES), 1)
    low_head = lane < HEAD_DIM
    low8 = (lane % HEAD_DIM) < (ROT_DIM // 2)

    rows4 = lax.broadcasted_iota(jnp.int32, (4 * QB, 1), 0)
    def sink_rows(h0, h1, h2, h3):
        return LOG2E * jnp.where(rows4 < QB, sink_ref[h0],
                                 jnp.where(rows4 < 2 * QB, sink_ref[h1],
                                           jnp.where(rows4 < 3 * QB, sink_ref[h2], sink_ref[h3])))
    sink_a = sink_rows(0, 2, 5, 7)
    sink_b = sink_rows(1, 3, 4, 6)
    n_blocks = SEQ // QB
    ones = jnp.ones((SB, LANES), BF16)

    def norm_main(sb):
        r0 = HALO + sb * SB
        h_s[r0:r0 + SB] = _rms(x_ref[0, sb * SB:(sb + 1) * SB], g_pre).astype(BF16)

    def norm_next_halo():
        h_s[HALO + TS:EXT] = _rms(xn_ref[0], g_pre).astype(BF16)

    def carry_prev_halo():
        for buf in (k_s, ksw_s, v_s, vsw_s):
            buf[0:HALO] = buf[TS:TS + HALO]

    def kv_chunk(c):
        r0 = HALO if c == 0 else c * SB
        r1 = (c + 1) * SB
        n = r1 - r0
        kv = _dot(h_s[r0:r1], win_ref[:, C_BK:C_BV + KV_WIDTH])
        cos = cos_ref[pl.ds(row0 + r0, n), :]
        ssin = sin_ref[pl.ds(row0 + r0, n), :]
        k = _rope(kv[:, :KV_WIDTH], cos, ssin, low8)
        v = kv[:, KV_WIDTH:]
        k_s[r0:r1] = k.astype(BF16)
        ksw_s[r0:r1] = pltpu.roll(k, HEAD_DIM, 1).astype(BF16)
        v_s[r0:r1, 0:LANES] = v.astype(BF16)
        v_s[r0:r1, LANES:2 * LANES] = ones[0:n]
        vsw_s[r0:r1, 0:LANES] = pltpu.roll(v, HEAD_DIM, 1).astype(BF16)
        vsw_s[r0:r1, LANES:2 * LANES] = ones[0:n]

    def sub_block(sb):
        m0 = HALO + sb * SB

        def gate(col, slot):
            gate_s[slot] = _sigmoid(_dot(h, win_ref[:, col:col + D_MODEL]))

        def conv_input():
            before = cu_s[SB:SB + CONV_HALO]
            hx = h_s[m0:m0 + SB + CONV_HALO]
            cx = _dot(hx, win_ref[:, C_AC:C_AX + A_WIDTH])
            cu = cx[:, :A_WIDTH] * cx[:, A_WIDTH:]
            first = row0 + sb * SB == 0
            last = row0 + (sb + 1) * SB == SEQ
            cu_s[0:CONV_HALO] = jnp.where(first, 0.0, before)
            cu_s[CONV_HALO:CONV_HALO + SB] = cu[0:SB]
            cu_s[CONV_HALO + SB:] = jnp.where(last, 0.0, cu[SB:])

        norm_main(sb)
        yield
        h = h_s[m0:m0 + SB]
        q = _dot(h, win_ref[:, C_BQ:C_BQ + ATTN_WIDTH])
        b_zs = _silu(_dot(h, win_ref[:, C_BZ:C_BZ + ATTN_WIDTH]))
        yield
        a_b = _dot(h, win_ref[:, C_AB:C_AB + A_WIDTH])
        a_zs = _silu(_dot(h, win_ref[:, C_AZ:C_AZ + A_WIDTH]))
        scale = (HEAD_DIM ** -0.5) * LOG2E
        qcos = cos_ref[pl.ds(row0 + m0, SB), :] * scale
        qsin = sin_ref[pl.ds(row0 + m0, SB), :] * scale
        q_lo, q_hi = [], []
        for mblk in range(ATTN_WIDTH // LANES):
            qr = _rope(q[:, mblk * LANES:(mblk + 1) * LANES], qcos, qsin, low8)
            q_lo.append(jnp.where(low_head, qr, 0.0).astype(BF16))
            q_hi.append(jnp.where(low_head, 0.0, qr).astype(BF16))
        yield
        fillers = [lambda: gate(C_G1, 1), conv_input]
        yb_blocks = []
        for j in range(SB // QB):
            qs = slice(j * QB, (j + 1) * QB)
            jb = sb * (SB // QB) + j
            blk = i * (TS // QB) + jb
            variant = jnp.where(blk == 0, 1, jnp.where(blk == n_blocks - 1, 2, 0))
            bias1 = bias_ref[variant]
            bias = jnp.concatenate([bias1, bias1, bias1, bias1], axis=0)
            lhs_a = jnp.concatenate([q_lo[0][qs], q_lo[1][qs], q_hi[2][qs], q_hi[3][qs]], axis=0)
            lhs_b = jnp.concatenate([q_hi[0][qs], q_hi[1][qs], q_lo[2][qs], q_lo[3][qs]], axis=0)
            ws = slice(jb * QB, jb * QB + 3 * QB)
            s_a = _dot_nt(lhs_a, k_s[ws]) + bias
            s_b = _dot_nt(lhs_b, ksw_s[ws]) + bias
            fillers[j % len(fillers)]()
            o_a = _window_softmax_pv(s_a, v_s[ws], sink_a)
            o_b = _window_softmax_pv(s_b, vsw_s[ws], sink_b)
            cols = [jnp.where(low_head, o_a[0:QB], o_b[0:QB]),
                    jnp.where(low_head, o_a[QB:2 * QB], o_b[QB:2 * QB]),
                    jnp.where(low_head, o_b[2 * QB:3 * QB], o_a[2 * QB:3 * QB]),
                    jnp.where(low_head, o_b[3 * QB:4 * QB], o_a[3 * QB:4 * QB])]
            yb_blocks.append(jnp.concatenate(cols, axis=1))
            yield
        yb = jnp.concatenate(yb_blocks, axis=0) * b_zs
        ub = _dot(yb.astype(BF16), wub_ref[...])
        mrg_s[...] = gate_s[1] * ub
        gate(C_G0, 0)
        y = (cu_s[CONV_HALO - 1:CONV_HALO - 1 + SB] * wconv_ref[0:1]
             + cu_s[CONV_HALO:CONV_HALO + SB] * wconv_ref[1:2]
             + cu_s[CONV_HALO + 1:CONV_HALO + 1 + SB] * wconv_ref[2:3])
        ya = a_b * y * a_zs
        ua = _dot(ya.astype(BF16), wua_ref[...])
        mrg_s[...] += gate_s[0] * ua
        yield
        mq = _dot(h, win_ref[:, C_MQ:C_MQ + MEM_WIDTH]) * ((MEM_HEAD_DIM ** -0.5) * LOG2E)
        m_zs = _silu(_dot(h, win_ref[:, C_MZ:C_MZ + MEM_WIDTH]))
        scores = []
        for hh in range(MEM_HEADS):
            hs = slice(hh * MEM_HEAD_DIM, (hh + 1) * MEM_HEAD_DIM)
            scores.append(_dot(mq[:, hs].astype(BF16), mkt_ref[hs, :]))
        gate(C_G2, 2)
        ym_heads = []
        for hh in range(MEM_HEADS):
            s = scores[hh]
            p = jnp.exp2(s - jnp.max(s, axis=1, keepdims=True))
            ov = _dot(p.astype(BF16), mvx_ref[hh])
            ym_heads.append(ov[:, :MEM_HEAD_DIM] / ov[:, MEM_HEAD_DIM:])
        ym = (jnp.concatenate(ym_heads, axis=1) * m_zs).astype(BF16)
        yield
        for half in range(SB // OUT_ROWS):
            rs = slice(half * OUT_ROWS, (half + 1) * OUT_ROWS)
            t0 = sb * SB + half * OUT_ROWS
            um = _dot(ym[rs], wum_ref[...])
            merged = mrg_s[rs] + gate_s[2, rs] * um
            o = _dot(merged.astype(BF16), wout_ref[...])
            write_out(t0 // OUT_ROWS, x_ref[0, t0:t0 + OUT_ROWS] + _rms(o, g_post), tile_row + t0)
        yield

    gens = [sub_block(sb) for sb in range(NSB)]
    def run(sb, count=1):
        for _ in range(count):
            next(gens[sb])

    def norm_for_chunk(c):
        if c < NSB:
            run(c)
        else:
            norm_next_halo()

    wait_out_slots()
    carry_prev_halo()
    run(0, 2)
    norm_for_chunk(1)
    kv_chunk(0)
    kv_chunk(1)
    for sb in range(NSB):
        run(sb)
        if sb > 0:
            run(sb - 1)
        run(sb, 2)
        if sb + 1 < NSB:
            norm_for_chunk(sb + 2)
        run(sb)
        flush_out()
        run(sb)
        if sb + 1 < NSB:
            run(sb + 1)
            kv_chunk(sb + 2)
    run(NSB - 1)
    flush_out()

    @pl.when(step == last_step)
    def _last_step():
        wait_out_slots()


def _resident(shape):
    return pl.BlockSpec(shape, lambda b, i: (0,) * len(shape), pipeline_mode=pl.Buffered(1))


def _layer(x, mem, g_pre, w_in, w_conv, attn_sink, g_mem, w_mem_kv, w_up_a, w_up_b, w_up_m, w_out, g_post,
           cos_t, sin_t, bias_t):
    bsz, s, d = x.shape
    assert (s, d) == (SEQ, D_MODEL) and s % TS == 0 and SB % QB == 0 and SB == 2 * HALO
    assert (STAGE_ROWS, STAGE_COLS) == (SB, d)
    nb = TS // HALO
    hbm = pl.BlockSpec(memory_space=pl.ANY)
    in_specs = [
        pl.BlockSpec(memory_space=pltpu.SMEM),
        pl.BlockSpec((1, TS, d), lambda b, i: (b, i, 0)),
        pl.BlockSpec((1, HALO, d), lambda b, i: (b, jnp.minimum((i + 1) * nb, s // HALO - 1), 0)),
        _resident((1, d)),
        hbm,
        _resident((CONV_WIDTH, A_WIDTH)),
        _resident((s + 2 * HALO, LANES)),
        _resident((s + 2 * HALO, LANES)),
        _resident((3, QB, 3 * QB)),
        pl.BlockSpec((1, MEM_LEN, d), lambda b, i: (b, 0, 0)),
        _resident((1, d)),
        hbm, hbm, hbm, hbm, hbm,
        _resident((1, d)),
    ]
    scratch = [
        pltpu.VMEM((EXT, d), BF16),
        pltpu.VMEM((EXT, KV_WIDTH), BF16),
        pltpu.VMEM((EXT, KV_WIDTH), BF16),
        pltpu.VMEM((EXT, 2 * KV_WIDTH), BF16),
        pltpu.VMEM((EXT, 2 * KV_WIDTH), BF16),
        pltpu.VMEM((SB + 2 * CONV_HALO, A_WIDTH), F32),
        pltpu.VMEM((SB, d), F32),
        pltpu.VMEM((3, SB, d), F32),
        pltpu.VMEM((d, IN_WIDTH), BF16),
        pltpu.VMEM((d, 2 * MEM_WIDTH), BF16),
        pltpu.VMEM((A_WIDTH, d), BF16),
        pltpu.VMEM((ATTN_WIDTH, d), BF16),
        pltpu.VMEM((MEM_WIDTH, d), BF16),
        pltpu.VMEM((d, d), BF16),
        pltpu.VMEM((MEM_WIDTH, MEM_LEN), BF16),
        pltpu.VMEM((MEM_HEADS, MEM_LEN, 2 * MEM_HEAD_DIM), BF16),
        pltpu.VMEM((STAGE_SLOTS, STAGE_ROWS, STAGE_COLS), F32),
        pltpu.SemaphoreType.DMA((STAGE_SLOTS + 4,)),
        pltpu.VMEM((OUT_SLOTS, OUT_ROWS, d), F32),
        pltpu.SemaphoreType.DMA((1,)),
    ]
    out = pl.pallas_call(
        _layer_kernel,
        grid=(bsz, s // TS),
        in_specs=in_specs,
        out_specs=hbm,
        out_shape=jax.ShapeDtypeStruct((bsz * s, d), x.dtype),
        scratch_shapes=scratch,
        compiler_params=pltpu.CompilerParams(dimension_semantics=("arbitrary", "arbitrary"),
                                             vmem_limit_bytes=VMEM_LIMIT_BYTES),
        name="hybrid_layer",
    )(attn_sink, x, x, g_pre.reshape(1, d), w_in, w_conv, cos_t, sin_t, bias_t, mem, g_mem.reshape(1, d),
      w_mem_kv, w_up_a, w_up_b, w_up_m, w_out, g_post.reshape(1, d))
    return out.reshape(bsz, s, d)


def kernel(x, mem, g_pre, w_in, w_conv, attn_sink, g_mem, w_mem_kv, w_up_a, w_up_b, w_up_m, w_out, g_post):
    cos_np, sin_np = _rope_tables()
    cos_t, sin_t, bias_t = jnp.asarray(cos_np), jnp.asarray(sin_np), jnp.asarray(_band_bias())
    for l in range(g_pre.shape[0]):
        x = _layer(x, mem, g_pre[l], w_in[l], w_conv[l], attn_sink[l], g_mem[l], w_mem_kv[l], w_up_a[l], w_up_b[l],
                   w_up_m[l], w_out[l], g_post[l], cos_t, sin_t, bias_t)
    return x
```

```python
import math

import numpy as np
import jax
import jax.numpy as jnp
from jax import lax
from jax.experimental import pallas as pl
from jax.experimental.pallas import tpu as pltpu

F32 = jnp.float32
BF16 = jnp.bfloat16

D_MODEL = 1024
SEQ = 4096
MEM_LEN = 256
EPS = 1e-6
CONV_WIDTH = 3
A_WIDTH = 512
HEAD_DIM = 64
ATTN_WIDTH = 512
N_Q_HEADS = 8
KV_WIDTH = 128
WINDOW = 128
ROPE_THETA = 500000.0
ROT_DIM = 16
MEM_HEADS = 4
MEM_HEAD_DIM = 128
MEM_WIDTH = 512

C_AB, C_AC, C_AX, C_AZ = 0, 512, 1024, 1536
C_BQ, C_BK, C_BV, C_BZ = 2048, 2560, 2688, 2816
C_MQ, C_MZ = 3328, 3840
C_G0, C_G1, C_G2 = 4352, 5376, 6400
IN_WIDTH = 7424

LANES = 128
BF16_ROWS = 16
VMEM_LIMIT_BYTES = 62 * 1024 * 1024

SB = 256
NSB = 4
TS = SB * NSB
QB = WINDOW
HALO = WINDOW
CONV_HALO = BF16_ROWS
EXT = TS + 2 * HALO
STAGE_ROWS = SB
STAGE_COLS = D_MODEL
OUT_ROWS = SB // 2
OUT_SLOTS = TS // OUT_ROWS

LOG2E = math.log2(math.e)


def _sigmoid(v):
    return 1.0 / (1.0 + jnp.exp(-v))


def _silu(v):
    return v * _sigmoid(v)


def _rms(v, g):
    ms = jnp.mean(v * v, axis=-1, keepdims=True)
    return v * lax.rsqrt(ms + EPS) * g


def _dot(a, b):
    return jnp.dot(a, b, preferred_element_type=F32)


def _dot_nt(a, b):
    return lax.dot_general(a, b, (((1,), (1,)), ((), ())), preferred_element_type=F32)


def _rope(t, cos, ssin, low8):
    partner = jnp.where(low8, pltpu.roll(t, LANES - ROT_DIM // 2, 1), pltpu.roll(t, ROT_DIM // 2, 1))
    return t * cos + partner * ssin


def _rope_tables():
    half = ROT_DIM // 2
    inv_freq = np.power(np.float32(ROPE_THETA), -np.arange(half, dtype=np.float32) * np.float32(2.0 / ROT_DIM))
    pos = (np.arange(SEQ + 2 * HALO) - HALO).astype(np.float32)
    ang = (pos[:, None] * inv_freq[None, :]).astype(np.float32)
    cos, sin = np.cos(ang).astype(np.float32), np.sin(ang).astype(np.float32)
    ct = np.ones((SEQ + 2 * HALO, HEAD_DIM), np.float32)
    st = np.zeros((SEQ + 2 * HALO, HEAD_DIM), np.float32)
    ct[:, :half], ct[:, half:ROT_DIM] = cos, cos
    st[:, :half], st[:, half:ROT_DIM] = -sin, sin
    return np.tile(ct, (1, LANES // HEAD_DIM)), np.tile(st, (1, LANES // HEAD_DIM))


def _band_bias():
    r = np.arange(QB)[:, None]
    c = np.arange(3 * QB)[None, :]
    band = (c >= r) & (c <= r + 2 * WINDOW)
    masks = [band, band & (c >= QB), band & (c < 2 * QB)]
    return np.stack([np.where(m, 0.0, -np.inf) for m in masks]).astype(np.float32)


def _stream_cast(jobs, slots, sem):
    chunks = [(src, dst, r0, c0, min(STAGE_COLS, src.shape[1] - c0))
              for src, dst in jobs
              for r0 in range(0, src.shape[0], STAGE_ROWS)
              for c0 in range(0, src.shape[1], STAGE_COLS)]
    n_slots = len(slots)

    def copy(n):
        src, _, r0, c0, cols = chunks[n]
        k = n % n_slots
        return pltpu.make_async_copy(src.at[pl.ds(r0, STAGE_ROWS), pl.ds(c0, cols)],
                                     slots[k].at[:, pl.ds(0, cols)], sem.at[k])

    for n in range(min(n_slots, len(chunks))):
        copy(n).start()
    for n, (_, dst, r0, c0, cols) in enumerate(chunks):
        copy(n).wait()
        dst[r0:r0 + STAGE_ROWS, c0:c0 + cols] = slots[n % n_slots][:, 0:cols].astype(BF16)
        if n + n_slots < len(chunks):
            copy(n + n_slots).start()


def _window_softmax_pv(s, v_win, sink_col):
    m = jnp.maximum(jnp.max(s, axis=1, keepdims=True), sink_col)
    p = jnp.exp2(s - m)
    ov = _dot(p.astype(BF16), v_win)
    return ov[:, :LANES] / (ov[:, LANES:] + jnp.exp2(sink_col - m))


def _layer_kernel(sink_ref, x_ref, xn_ref, gpre_ref, win_hbm, wconv_ref, cos_ref, sin_ref, bias_ref,
                  mem_ref, gmem_ref, wmkv_hbm, wua_hbm, wub_hbm, wum_hbm, wout_hbm, gpost_ref,
                  out_hbm, h_s, k_s, ksw_s, v_s, vsw_s, cu_s, mrg_s, gate_s,
                  win_ref, wmkv_ref, wua_ref, wub_ref, wum_ref, wout_ref, mkt_ref, mvx_ref,
                  stage_sem, out_s, out_sem):
    b = pl.program_id(0)
    i = pl.program_id(1)
    n_tiles = pl.num_programs(1)
    step = b * n_tiles + i
    last_step = pl.num_programs(0) * n_tiles - 1
    row0 = pl.multiple_of(i * TS, TS)
    tile_row = step * TS
    g_pre = gpre_ref[...]
    g_post = gpost_ref[...]

    def out_slot(slot):
        return out_s.at[slot // 2, pl.ds((slot % 2) * OUT_ROWS, OUT_ROWS)]

    def out_copy(slot, dst_row):
        return pltpu.make_async_copy(out_slot(slot), out_hbm.at[pl.ds(dst_row, OUT_ROWS)], out_sem.at[0])

    pending_out = []

    def write_out(slot, value, dst_row):
        out_slot(slot)[...] = value
        pending_out.append((slot, dst_row))

    def flush_out():
        for slot, dst_row in pending_out:
            out_copy(slot, dst_row).start()
        pending_out.clear()

    def wait_out_slots():
        for slot in range(OUT_SLOTS):
            out_copy(slot, 0).wait()

    @pl.when(step == 0)
    def _first_step():
        slots = [gate_s.at[k] for k in range(3)] + [mrg_s] + [out_s.at[k] for k in range(OUT_SLOTS // 2)]
        _stream_cast([(win_hbm, win_ref), (wmkv_hbm, wmkv_ref), (wua_hbm, wua_ref), (wub_hbm, wub_ref),
                      (wum_hbm, wum_ref), (wout_hbm, wout_ref)], slots, stage_sem)
        for buf in (k_s, ksw_s, v_s, vsw_s):
            buf[TS:TS + HALO] = jnp.zeros((HALO, buf.shape[1]), BF16)
        cu_s[...] = jnp.zeros(cu_s.shape, F32)
        out_s[...] = jnp.zeros(out_s.shape, F32)
        for slot in range(OUT_SLOTS):
            out_copy(slot, slot * OUT_ROWS).start()

    @pl.when(i == 0)
    def _memory_kv():
        mn = _rms(mem_ref[0], gmem_ref[...]).astype(BF16)
        kv = _dot(mn, wmkv_ref[...])
        mkt_ref[...] = kv[:, :MEM_WIDTH].T.astype(BF16)
        ones_m = jnp.ones((MEM_LEN, MEM_HEAD_DIM), F32)
        for hh in range(MEM_HEADS):
            c0 = MEM_WIDTH + hh * MEM_HEAD_DIM
            mvx_ref[hh] = jnp.concatenate([kv[:, c0:c0 + MEM_HEAD_DIM], ones_m], axis=1).astype(BF16)

    lane = lax.broadcasted_iota(jnp.int32, (1, LANES), 1)
    low_head = lane < HEAD_DIM
    low8 = (lane % HEAD_DIM) < (ROT_DIM // 2)

    rows4 = lax.broadcasted_iota(jnp.int32, (4 * QB, 1), 0)
    def sink_rows(h0, h1, h2, h3):
        return LOG2E * jnp.where(rows4 < QB, sink_ref[h0],
                                 jnp.where(rows4 < 2 * QB, sink_ref[h1],
                                           jnp.where(rows4 < 3 * QB, sink_ref[h2], sink_ref[h3])))
    sink_a = sink_rows(0, 2, 5, 7)
    sink_b = sink_rows(1, 3, 4, 6)
    n_blocks = SEQ // QB
    ones = jnp.ones((SB, LANES), BF16)

    def norm_main(sb):
        r0 = HALO + sb * SB
        h_s[r0:r0 + SB] = _rms(x_ref[0, sb * SB:(sb + 1) * SB], g_pre).astype(BF16)

    def norm_next_halo():
        h_s[HALO + TS:EXT] = _rms(xn_ref[0], g_pre).astype(BF16)

    def carry_prev_halo():
        for buf in (k_s, ksw_s, v_s, vsw_s):
            buf[0:HALO] = buf[TS:TS + HALO]

    def kv_chunk(c):
        r0 = HALO if c == 0 else c * SB
        r1 = (c + 1) * SB
        n = r1 - r0
        kv = _dot(h_s[r0:r1], win_ref[:, C_BK:C_BV + KV_WIDTH])
        cos = cos_ref[pl.ds(row0 + r0, n), :]
        ssin = sin_ref[pl.ds(row0 + r0, n), :]
        k = _rope(kv[:, :KV_WIDTH], cos, ssin, low8)
        v = kv[:, KV_WIDTH:]
        k_s[r0:r1] = k.astype(BF16)
        ksw_s[r0:r1] = pltpu.roll(k, HEAD_DIM, 1).astype(BF16)
        v_s[r0:r1, 0:LANES] = v.astype(BF16)
        v_s[r0:r1, LANES:2 * LANES] = ones[0:n]
        vsw_s[r0:r1, 0:LANES] = pltpu.roll(v, HEAD_DIM, 1).astype(BF16)
        vsw_s[r0:r1, LANES:2 * LANES] = ones[0:n]

    def sub_block(sb):
        m0 = HALO + sb * SB

        def gate(col, slot):
            gate_s[slot] = _sigmoid(_dot(h, win_ref[:, col:col + D_MODEL]))

        def conv_input():
            have = CONV_HALO if sb > 0 else 0
            before = cu_s[SB:SB + CONV_HALO + have]
            hx = h_s[m0 + have:m0 + SB + CONV_HALO]
            cx = _dot(hx, win_ref[:, C_AC:C_AX + A_WIDTH])
            cu = cx[:, :A_WIDTH] * cx[:, A_WIDTH:]
            first = row0 + sb * SB == 0
            last = row0 + (sb + 1) * SB == SEQ
            cu_s[0:CONV_HALO + have] = before if sb > 0 else jnp.where(first, 0.0, before)
            cu_s[CONV_HALO + have:CONV_HALO + SB] = cu[0:SB - have]
            cu_s[CONV_HALO + SB:] = jnp.where(last, 0.0, cu[SB - have:])

        norm_main(sb)
        yield
        h = h_s[m0:m0 + SB]
        q = _dot(h, win_ref[:, C_BQ:C_BQ + ATTN_WIDTH])
        b_zs = _silu(_dot(h, win_ref[:, C_BZ:C_BZ + ATTN_WIDTH]))
        yield
        a_b = _dot(h, win_ref[:, C_AB:C_AB + A_WIDTH])
        a_zs = _silu(_dot(h, win_ref[:, C_AZ:C_AZ + A_WIDTH]))
        scale = (HEAD_DIM ** -0.5) * LOG2E
        qcos = cos_ref[pl.ds(row0 + m0, SB), :] * scale
        qsin = sin_ref[pl.ds(row0 + m0, SB), :] * scale
        q_lo, q_hi = [], []
        for mblk in range(ATTN_WIDTH // LANES):
            qr = _rope(q[:, mblk * LANES:(mblk + 1) * LANES], qcos, qsin, low8)
            q_lo.append(jnp.where(low_head, qr, 0.0).astype(BF16))
            q_hi.append(jnp.where(low_head, 0.0, qr).astype(BF16))
        yield
        fillers = [lambda: gate(C_G1, 1), conv_input]
        yb_blocks = []
        for j in range(SB // QB):
            qs = slice(j * QB, (j + 1) * QB)
            jb = sb * (SB // QB) + j
            blk = i * (TS // QB) + jb
            variant = jnp.where(blk == 0, 1, jnp.where(blk == n_blocks - 1, 2, 0))
            bias1 = bias_ref[variant]
            bias = jnp.concatenate([bias1, bias1, bias1, bias1], axis=0)
            lhs_a = jnp.concatenate([q_lo[0][qs], q_lo[1][qs], q_hi[2][qs], q_hi[3][qs]], axis=0)
            lhs_b = jnp.concatenate([q_hi[0][qs], q_hi[1][qs], q_lo[2][qs], q_lo[3][qs]], axis=0)
            ws = slice(jb * QB, jb * QB + 3 * QB)
            s_a = _dot_nt(lhs_a, k_s[ws]) + bias
            s_b = _dot_nt(lhs_b, ksw_s[ws]) + bias
            fillers[j % len(fillers)]()
            o_a = _window_softmax_pv(s_a, v_s[ws], sink_a)
            o_b = _window_softmax_pv(s_b, vsw_s[ws], sink_b)
            cols = [jnp.where(low_head, o_a[0:QB], o_b[0:QB]),
                    jnp.where(low_head, o_a[QB:2 * QB], o_b[QB:2 * QB]),
                    jnp.where(low_head, o_b[2 * QB:3 * QB], o_a[2 * QB:3 * QB]),
                    jnp.where(low_head, o_b[3 * QB:4 * QB], o_a[3 * QB:4 * QB])]
            yb_blocks.append(jnp.concatenate(cols, axis=1))
            yield
        yb = jnp.concatenate(yb_blocks, axis=0) * b_zs
        ub = _dot(yb.astype(BF16), wub_ref[...])
        mrg_s[...] = gate_s[1] * ub
        gate(C_G0, 0)
        y = (cu_s[CONV_HALO - 1:CONV_HALO - 1 + SB] * wconv_ref[0:1]
             + cu_s[CONV_HALO:CONV_HALO + SB] * wconv_ref[1:2]
             + cu_s[CONV_HALO + 1:CONV_HALO + 1 + SB] * wconv_ref[2:3])
        ya = a_b * y * a_zs
        ua = _dot(ya.astype(BF16), wua_ref[...])
        mrg_s[...] += gate_s[0] * ua
        yield
        mq = _dot(h, win_ref[:, C_MQ:C_MQ + MEM_WIDTH]) * ((MEM_HEAD_DIM ** -0.5) * LOG2E)
        m_zs = _silu(_dot(h, win_ref[:, C_MZ:C_MZ + MEM_WIDTH]))
        scores = []
        for hh in range(MEM_HEADS):
            hs = slice(hh * MEM_HEAD_DIM, (hh + 1) * MEM_HEAD_DIM)
            scores.append(_dot(mq[:, hs].astype(BF16), mkt_ref[hs, :]))
        gate(C_G2, 2)
        ym_heads = []
        for hh in range(MEM_HEADS):
            s = scores[hh]
            p = jnp.exp2(s - jnp.max(s, axis=1, keepdims=True))
            ov = _dot(p.astype(BF16), mvx_ref[hh])
            ym_heads.append(ov[:, :MEM_HEAD_DIM] / ov[:, MEM_HEAD_DIM:])
        ym = (jnp.concatenate(ym_heads, axis=1) * m_zs).astype(BF16)
        yield
        for half in range(SB // OUT_ROWS):
            rs = slice(half * OUT_ROWS, (half + 1) * OUT_ROWS)
            t0 = sb * SB + half * OUT_ROWS
            um = _dot(ym[rs], wum_ref[...])
            merged = mrg_s[rs] + gate_s[2, rs] * um
            o = _dot(merged.astype(BF16), wout_ref[...])
            write_out(t0 // OUT_ROWS, x_ref[0, t0:t0 + OUT_ROWS] + _rms(o, g_post), tile_row + t0)
        yield

    gens = [sub_block(sb) for sb in range(NSB)]
    def run(sb, count=1):
        for _ in range(count):
            next(gens[sb])

    def norm_for_chunk(c):
        if c < NSB:
            run(c)
        else:
            norm_next_halo()

    wait_out_slots()
    carry_prev_halo()
    run(0, 2)
    norm_for_chunk(1)
    kv_chunk(0)
    kv_chunk(1)
    for sb in range(NSB):
        run(sb)
        if sb > 0:
            run(sb - 1)
        run(sb, 2)
        if sb + 1 < NSB:
            norm_for_chunk(sb + 2)
        run(sb)
        flush_out()
        run(sb)
        if sb + 1 < NSB:
            run(sb + 1)
            kv_chunk(sb + 2)
    run(NSB - 1)
    flush_out()

    @pl.when(step == last_step)
    def _last_step():
        wait_out_slots()


def _resident(shape):
    return pl.BlockSpec(shape, lambda b, i: (0,) * len(shape), pipeline_mode=pl.Buffered(1))


def _layer(x, mem, g_pre, w_in, w_conv, attn_sink, g_mem, w_mem_kv, w_up_a, w_up_b, w_up_m, w_out, g_post,
           cos_t, sin_t, bias_t):
    bsz, s, d = x.shape
    assert (s, d) == (SEQ, D_MODEL) and s % TS == 0 and SB % QB == 0 and SB == 2 * HALO
    assert (STAGE_ROWS, STAGE_COLS) == (SB, d)
    nb = TS // HALO
    hbm = pl.BlockSpec(memory_space=pl.ANY)
    in_specs = [
        pl.BlockSpec(memory_space=pltpu.SMEM),
        pl.BlockSpec((1, TS, d), lambda b, i: (b, i, 0)),
        pl.BlockSpec((1, HALO, d), lambda b, i: (b, jnp.minimum((i + 1) * nb, s // HALO - 1), 0)),
        _resident((1, d)),
        hbm,
        _resident((CONV_WIDTH, A_WIDTH)),
        _resident((s + 2 * HALO, LANES)),
        _resident((s + 2 * HALO, LANES)),
        _resident((3, QB, 3 * QB)),
        pl.BlockSpec((1, MEM_LEN, d), lambda b, i: (b, 0, 0)),
        _resident((1, d)),
        hbm, hbm, hbm, hbm, hbm,
        _resident((1, d)),
    ]
    scratch = [
        pltpu.VMEM((EXT, d), BF16),
        pltpu.VMEM((EXT, KV_WIDTH), BF16),
        pltpu.VMEM((EXT, KV_WIDTH), BF16),
        pltpu.VMEM((EXT, 2 * KV_WIDTH), BF16),
        pltpu.VMEM((EXT, 2 * KV_WIDTH), BF16),
        pltpu.VMEM((SB + 2 * CONV_HALO, A_WIDTH), F32),
        pltpu.VMEM((SB, d), F32),
        pltpu.VMEM((3, SB, d), F32),
        pltpu.VMEM((d, IN_WIDTH), BF16),
        pltpu.VMEM((d, 2 * MEM_WIDTH), BF16),
        pltpu.VMEM((A_WIDTH, d), BF16),
        pltpu.VMEM((ATTN_WIDTH, d), BF16),
        pltpu.VMEM((MEM_WIDTH, d), BF16),
        pltpu.VMEM((d, d), BF16),
        pltpu.VMEM((MEM_WIDTH, MEM_LEN), BF16),
        pltpu.VMEM((MEM_HEADS, MEM_LEN, 2 * MEM_HEAD_DIM), BF16),
        pltpu.SemaphoreType.DMA((4 + OUT_SLOTS // 2,)),
        pltpu.VMEM((OUT_SLOTS // 2, STAGE_ROWS, d), F32),
        pltpu.SemaphoreType.DMA((1,)),
    ]
    out = pl.pallas_call(
        _layer_kernel,
        grid=(bsz, s // TS),
        in_specs=in_specs,
        out_specs=hbm,
        out_shape=jax.ShapeDtypeStruct((bsz * s, d), x.dtype),
        scratch_shapes=scratch,
        compiler_params=pltpu.CompilerParams(dimension_semantics=("arbitrary", "arbitrary"),
                                             vmem_limit_bytes=VMEM_LIMIT_BYTES),
        name="hybrid_layer",
    )(attn_sink, x, x, g_pre.reshape(1, d), w_in, w_conv, cos_t, sin_t, bias_t, mem, g_mem.reshape(1, d),
      w_mem_kv, w_up_a, w_up_b, w_up_m, w_out, g_post.reshape(1, d))
    return out.reshape(bsz, s, d)


def kernel(x, mem, g_pre, w_in, w_conv, attn_sink, g_mem, w_mem_kv, w_up_a, w_up_b, w_up_m, w_out, g_post):
    cos_np, sin_np = _rope_tables()
    cos_t, sin_t, bias_t = jnp.asarray(cos_np), jnp.asarray(sin_np), jnp.asarray(_band_bias())
    for l in range(g_pre.shape[0]):
        x = _layer(x, mem, g_pre[l], w_in[l], w_conv[l], attn_sink[l], g_mem[l], w_mem_kv[l], w_up_a[l], w_up_b[l],
                   w_up_m[l], w_out[l], g_post[l], cos_t, sin_t, bias_t)
    return x
```

```python
import math

import numpy as np
import jax
import jax.numpy as jnp
from jax import lax
from jax.experimental import pallas as pl
from jax.experimental.pallas import tpu as pltpu

F32 = jnp.float32
BF16 = jnp.bfloat16

D_MODEL = 1024
SEQ = 4096
MEM_LEN = 256
EPS = 1e-6
CONV_WIDTH = 3
A_WIDTH = 512
HEAD_DIM = 64
ATTN_WIDTH = 512
N_Q_HEADS = 8
KV_WIDTH = 128
WINDOW = 128
ROPE_THETA = 500000.0
ROT_DIM = 16
MEM_HEADS = 4
MEM_HEAD_DIM = 128
MEM_WIDTH = 512

C_AB, C_AC, C_AX, C_AZ = 0, 512, 1024, 1536
C_BQ, C_BK, C_BV, C_BZ = 2048, 2560, 2688, 2816
C_MQ, C_MZ = 3328, 3840
C_G0, C_G1, C_G2 = 4352, 5376, 6400
IN_WIDTH = 7424

LANES = 128
BF16_ROWS = 16
VMEM_LIMIT_BYTES = 62 * 1024 * 1024

SB = 256
NSB = 4
TS = SB * NSB
QB = WINDOW
HALO = WINDOW
CONV_HALO = BF16_ROWS
EXT = TS + 2 * HALO
STAGE_ROWS = SB
STAGE_COLS = D_MODEL
OUT_ROWS = SB // 2
OUT_SLOTS = TS // OUT_ROWS

LOG2E = math.log2(math.e)


def _sigmoid(v):
    return 1.0 / (1.0 + jnp.exp(-v))


def _silu(v):
    return v * _sigmoid(v)


def _rms(v, g):
    ms = jnp.mean(v * v, axis=-1, keepdims=True)
    return v * lax.rsqrt(ms + EPS) * g


def _dot(a, b):
    return jnp.dot(a, b, preferred_element_type=F32)


def _rope(t, cos, ssin, low8):
    partner = jnp.where(low8, pltpu.roll(t, LANES - ROT_DIM // 2, 1), pltpu.roll(t, ROT_DIM // 2, 1))
    return t * cos + partner * ssin


def _rope_tables():
    half = ROT_DIM // 2
    inv_freq = np.power(np.float32(ROPE_THETA), -np.arange(half, dtype=np.float32) * np.float32(2.0 / ROT_DIM))
    pos = (np.arange(SEQ + 2 * HALO) - HALO).astype(np.float32)
    ang = (pos[:, None] * inv_freq[None, :]).astype(np.float32)
    cos, sin = np.cos(ang).astype(np.float32), np.sin(ang).astype(np.float32)
    ct = np.ones((SEQ + 2 * HALO, HEAD_DIM), np.float32)
    st = np.zeros((SEQ + 2 * HALO, HEAD_DIM), np.float32)
    ct[:, :half], ct[:, half:ROT_DIM] = cos, cos
    st[:, :half], st[:, half:ROT_DIM] = -sin, sin
    return np.tile(ct, (1, LANES // HEAD_DIM)), np.tile(st, (1, LANES // HEAD_DIM))


def _band_bias():
    r = np.arange(QB)[:, None]
    c = np.arange(3 * QB)[None, :]
    band = (c >= r) & (c <= r + 2 * WINDOW)
    masks = [band, band & (c >= QB), band & (c < 2 * QB)]
    return np.stack([np.where(m, 0.0, -np.inf) for m in masks]).astype(np.float32)


def _stream_cast(jobs, slots, sem):
    chunks = [(src, dst, r0, c0, min(STAGE_COLS, src.shape[1] - c0))
              for src, dst in jobs
              for r0 in range(0, src.shape[0], STAGE_ROWS)
              for c0 in range(0, src.shape[1], STAGE_COLS)]
    n_slots = len(slots)

    def copy(n):
        src, _, r0, c0, cols = chunks[n]
        k = n % n_slots
        return pltpu.make_async_copy(src.at[pl.ds(r0, STAGE_ROWS), pl.ds(c0, cols)],
                                     slots[k].at[:, pl.ds(0, cols)], sem.at[k])

    for n in range(min(n_slots, len(chunks))):
        copy(n).start()
    for n, (_, dst, r0, c0, cols) in enumerate(chunks):
        copy(n).wait()
        dst[r0:r0 + STAGE_ROWS, c0:c0 + cols] = slots[n % n_slots][:, 0:cols].astype(BF16)
        if n + n_slots < len(chunks):
            copy(n + n_slots).start()


def _window_softmax_pv(s, v_win, sink_col):
    m = jnp.maximum(jnp.max(s, axis=1, keepdims=True), sink_col)
    p = jnp.exp2(s - m)
    ov = _dot(p.astype(BF16), v_win)
    return ov[:, :LANES] / (ov[:, LANES:] + jnp.exp2(sink_col - m))


def _layer_kernel(sink_ref, x_ref, xn_ref, gpre_ref, win_hbm, wconv_ref, cos_ref, sin_ref, bias_ref,
                  mem_ref, gmem_ref, wmkv_hbm, wua_hbm, wub_hbm, wum_hbm, wout_hbm, gpost_ref,
                  out_hbm, h_s, k_s, ksw_s, v_s, vsw_s, cu_s, mrg_s, gate_s,
                  win_ref, wmkv_ref, wua_ref, wub_ref, wum_ref, wout_ref, mkt_ref, mvx_ref,
                  stage_sem, out_s, out_sem):
    b = pl.program_id(0)
    i = pl.program_id(1)
    n_tiles = pl.num_programs(1)
    step = b * n_tiles + i
    last_step = pl.num_programs(0) * n_tiles - 1
    row0 = pl.multiple_of(i * TS, TS)
    tile_row = step * TS
    g_pre = gpre_ref[...]
    g_post = gpost_ref[...]

    def out_slot(slot):
        return out_s.at[slot // 2, pl.ds((slot % 2) * OUT_ROWS, OUT_ROWS)]

    def out_copy(slot, dst_row):
        return pltpu.make_async_copy(out_slot(slot), out_hbm.at[pl.ds(dst_row, OUT_ROWS)], out_sem.at[0])

    pending_out = []

    def write_out(slot, value, dst_row):
        out_slot(slot)[...] = value
        pending_out.append((slot, dst_row))

    def flush_out():
        for slot, dst_row in pending_out:
            out_copy(slot, dst_row).start()
        pending_out.clear()

    def wait_out_slots():
        for slot in range(OUT_SLOTS):
            out_copy(slot, 0).wait()

    @pl.when(step == 0)
    def _first_step():
        slots = [gate_s.at[k] for k in range(3)] + [mrg_s] + [out_s.at[k] for k in range(OUT_SLOTS // 2)]
        _stream_cast([(win_hbm, win_ref), (wmkv_hbm, wmkv_ref), (wua_hbm, wua_ref), (wub_hbm, wub_ref),
                      (wum_hbm, wum_ref), (wout_hbm, wout_ref)], slots, stage_sem)
        for buf in (k_s, ksw_s):
            buf[:, TS:TS + HALO] = jnp.zeros((KV_WIDTH, HALO), BF16)
        for buf in (v_s, vsw_s):
            buf[TS:TS + HALO] = jnp.zeros((HALO, buf.shape[1]), BF16)
        cu_s[...] = jnp.zeros(cu_s.shape, F32)
        out_s[...] = jnp.zeros(out_s.shape, F32)
        for slot in range(OUT_SLOTS):
            out_copy(slot, slot * OUT_ROWS).start()

    @pl.when(i == 0)
    def _memory_kv():
        mn = _rms(mem_ref[0], gmem_ref[...]).astype(BF16)
        kv = _dot(mn, wmkv_ref[...])
        mkt_ref[...] = kv[:, :MEM_WIDTH].T.astype(BF16)
        ones_m = jnp.ones((MEM_LEN, MEM_HEAD_DIM), F32)
        for hh in range(MEM_HEADS):
            c0 = MEM_WIDTH + hh * MEM_HEAD_DIM
            mvx_ref[hh] = jnp.concatenate([kv[:, c0:c0 + MEM_HEAD_DIM], ones_m], axis=1).astype(BF16)

    lane = lax.broadcasted_iota(jnp.int32, (1, LANES), 1)
    low_head = lane < HEAD_DIM
    low8 = (lane % HEAD_DIM) < (ROT_DIM // 2)

    rows4 = lax.broadcasted_iota(jnp.int32, (4 * QB, 1), 0)
    def sink_rows(h0, h1, h2, h3):
        return LOG2E * jnp.where(rows4 < QB, sink_ref[h0],
                                 jnp.where(rows4 < 2 * QB, sink_ref[h1],
                                           jnp.where(rows4 < 3 * QB, sink_ref[h2], sink_ref[h3])))
    sink_a = sink_rows(0, 2, 5, 7)
    sink_b = sink_rows(1, 3, 4, 6)
    n_blocks = SEQ // QB
    ones = jnp.ones((SB, LANES), BF16)

    def norm_main(sb):
        r0 = HALO + sb * SB
        h_s[r0:r0 + SB] = _rms(x_ref[0, sb * SB:(sb + 1) * SB], g_pre).astype(BF16)

    def norm_next_halo():
        h_s[HALO + TS:EXT] = _rms(xn_ref[0], g_pre).astype(BF16)

    def carry_prev_halo():
        for buf in (k_s, ksw_s):
            buf[:, 0:HALO] = buf[:, TS:TS + HALO]
        for buf in (v_s, vsw_s):
            buf[0:HALO] = buf[TS:TS + HALO]

    def kv_chunk(c):
        r0 = HALO if c == 0 else c * SB
        r1 = (c + 1) * SB
        n = r1 - r0
        kv = _dot(h_s[r0:r1], win_ref[:, C_BK:C_BV + KV_WIDTH])
        cos = cos_ref[pl.ds(row0 + r0, n), :]
        ssin = sin_ref[pl.ds(row0 + r0, n), :]
        k = _rope(kv[:, :KV_WIDTH], cos, ssin, low8)
        v = kv[:, KV_WIDTH:]
        k_s[:, r0:r1] = k.T.astype(BF16)
        ksw_s[:, r0:r1] = pltpu.roll(k, HEAD_DIM, 1).T.astype(BF16)
        v_s[r0:r1, 0:LANES] = v.astype(BF16)
        v_s[r0:r1, LANES:2 * LANES] = ones[0:n]
        vsw_s[r0:r1, 0:LANES] = pltpu.roll(v, HEAD_DIM, 1).astype(BF16)
        vsw_s[r0:r1, LANES:2 * LANES] = ones[0:n]

    def sub_block(sb):
        m0 = HALO + sb * SB

        def gate(col, slot):
            gate_s[slot] = _sigmoid(_dot(h, win_ref[:, col:col + D_MODEL]))

        def conv_input():
            have = CONV_HALO if sb > 0 else 0
            before = cu_s[SB:SB + CONV_HALO + have]
            hx = h_s[m0 + have:m0 + SB + CONV_HALO]
            cx = _dot(hx, win_ref[:, C_AC:C_AX + A_WIDTH])
            cu = cx[:, :A_WIDTH] * cx[:, A_WIDTH:]
            first = row0 + sb * SB == 0
            last = row0 + (sb + 1) * SB == SEQ
            cu_s[0:CONV_HALO + have] = before if sb > 0 else jnp.where(first, 0.0, before)
            cu_s[CONV_HALO + have:CONV_HALO + SB] = cu[0:SB - have]
            cu_s[CONV_HALO + SB:] = jnp.where(last, 0.0, cu[SB - have:])

        norm_main(sb)
        yield
        h = h_s[m0:m0 + SB]
        q = _dot(h, win_ref[:, C_BQ:C_BQ + ATTN_WIDTH])
        b_zs = _silu(_dot(h, win_ref[:, C_BZ:C_BZ + ATTN_WIDTH]))
        yield
        a_b = _dot(h, win_ref[:, C_AB:C_AB + A_WIDTH])
        a_zs = _silu(_dot(h, win_ref[:, C_AZ:C_AZ + A_WIDTH]))
        scale = (HEAD_DIM ** -0.5) * LOG2E
        qcos = cos_ref[pl.ds(row0 + m0, SB), :] * scale
        qsin = sin_ref[pl.ds(row0 + m0, SB), :] * scale
        q_lo, q_hi = [], []
        for mblk in range(ATTN_WIDTH // LANES):
            qr = _rope(q[:, mblk * LANES:(mblk + 1) * LANES], qcos, qsin, low8)
            q_lo.append(jnp.where(low_head, qr, 0.0).astype(BF16))
            q_hi.append(jnp.where(low_head, 0.0, qr).astype(BF16))
        yield
        fillers = [lambda: gate(C_G1, 1), conv_input]
        yb_blocks = []
        for j in range(SB // QB):
            qs = slice(j * QB, (j + 1) * QB)
            jb = sb * (SB // QB) + j
            blk = i * (TS // QB) + jb
            variant = jnp.where(blk == 0, 1, jnp.where(blk == n_blocks - 1, 2, 0))
            bias1 = bias_ref[variant]
            bias = jnp.concatenate([bias1, bias1, bias1, bias1], axis=0)
            lhs_a = jnp.concatenate([q_lo[0][qs], q_lo[1][qs], q_hi[2][qs], q_hi[3][qs]], axis=0)
            lhs_b = jnp.concatenate([q_hi[0][qs], q_hi[1][qs], q_lo[2][qs], q_lo[3][qs]], axis=0)
            ws = slice(jb * QB, jb * QB + 3 * QB)
            s_a = _dot(lhs_a, k_s[:, ws]) + bias
            s_b = _dot(lhs_b, ksw_s[:, ws]) + bias
            fillers[j % len(fillers)]()
            o_a = _window_softmax_pv(s_a, v_s[ws], sink_a)
            o_b = _window_softmax_pv(s_b, vsw_s[ws], sink_b)
            cols = [jnp.where(low_head, o_a[0:QB], o_b[0:QB]),
                    jnp.where(low_head, o_a[QB:2 * QB], o_b[QB:2 * QB]),
                    jnp.where(low_head, o_b[2 * QB:3 * QB], o_a[2 * QB:3 * QB]),
                    jnp.where(low_head, o_b[3 * QB:4 * QB], o_a[3 * QB:4 * QB])]
            yb_blocks.append(jnp.concatenate(cols, axis=1))
            yield
        yb = jnp.concatenate(yb_blocks, axis=0) * b_zs
        ub = _dot(yb.astype(BF16), wub_ref[...])
        mrg_s[...] = gate_s[1] * ub
        gate(C_G0, 0)
        y = (cu_s[CONV_HALO - 1:CONV_HALO - 1 + SB] * wconv_ref[0:1]
             + cu_s[CONV_HALO:CONV_HALO + SB] * wconv_ref[1:2]
             + cu_s[CONV_HALO + 1:CONV_HALO + 1 + SB] * wconv_ref[2:3])
        ya = a_b * y * a_zs
        ua = _dot(ya.astype(BF16), wua_ref[...])
        mrg_s[...] += gate_s[0] * ua
        yield
        mq = _dot(h, win_ref[:, C_MQ:C_MQ + MEM_WIDTH]) * ((MEM_HEAD_DIM ** -0.5) * LOG2E)
        m_zs = _silu(_dot(h, win_ref[:, C_MZ:C_MZ + MEM_WIDTH]))
        scores = []
        for hh in range(MEM_HEADS):
            hs = slice(hh * MEM_HEAD_DIM, (hh + 1) * MEM_HEAD_DIM)
            scores.append(_dot(mq[:, hs].astype(BF16), mkt_ref[hs, :]))
        gate(C_G2, 2)
        ym_heads = []
        for hh in range(MEM_HEADS):
            s = scores[hh]
            p = jnp.exp2(s - jnp.max(s, axis=1, keepdims=True))
            ov = _dot(p.astype(BF16), mvx_ref[hh])
            ym_heads.append(ov[:, :MEM_HEAD_DIM] / ov[:, MEM_HEAD_DIM:])
        ym = (jnp.concatenate(ym_heads, axis=1) * m_zs).astype(BF16)
        yield
        for half in range(SB // OUT_ROWS):
            rs = slice(half * OUT_ROWS, (half + 1) * OUT_ROWS)
            t0 = sb * SB + half * OUT_ROWS
            um = _dot(ym[rs], wum_ref[...])
            merged = mrg_s[rs] + gate_s[2, rs] * um
            o = _dot(merged.astype(BF16), wout_ref[...])
            write_out(t0 // OUT_ROWS, x_ref[0, t0:t0 + OUT_ROWS] + _rms(o, g_post), tile_row + t0)
        yield

    gens = [sub_block(sb) for sb in range(NSB)]
    def run(sb, count=1):
        for _ in range(count):
            next(gens[sb])

    def norm_for_chunk(c):
        if c < NSB:
            run(c)
        else:
            norm_next_halo()

    wait_out_slots()
    carry_prev_halo()
    run(0, 2)
    norm_for_chunk(1)
    kv_chunk(0)
    kv_chunk(1)
    for sb in range(NSB):
        run(sb)
        if sb > 0:
            run(sb - 1)
        run(sb, 2)
        if sb + 1 < NSB:
            norm_for_chunk(sb + 2)
        run(sb)
        flush_out()
        run(sb)
        if sb + 1 < NSB:
            run(sb + 1)
            kv_chunk(sb + 2)
    run(NSB - 1)
    flush_out()

    @pl.when(step == last_step)
    def _last_step():
        wait_out_slots()


def _resident(shape):
    return pl.BlockSpec(shape, lambda b, i: (0,) * len(shape), pipeline_mode=pl.Buffered(1))


def _layer(x, mem, g_pre, w_in, w_conv, attn_sink, g_mem, w_mem_kv, w_up_a, w_up_b, w_up_m, w_out, g_post,
           cos_t, sin_t, bias_t):
    bsz, s, d = x.shape
    assert (s, d) == (SEQ, D_MODEL) and s % TS == 0 and SB % QB == 0 and SB == 2 * HALO
    assert (STAGE_ROWS, STAGE_COLS) == (SB, d)
    nb = TS // HALO
    hbm = pl.BlockSpec(memory_space=pl.ANY)
    in_specs = [
        pl.BlockSpec(memory_space=pltpu.SMEM),
        pl.BlockSpec((1, TS, d), lambda b, i: (b, i, 0)),
        pl.BlockSpec((1, HALO, d), lambda b, i: (b, jnp.minimum((i + 1) * nb, s // HALO - 1), 0)),
        _resident((1, d)),
        hbm,
        _resident((CONV_WIDTH, A_WIDTH)),
        _resident((s + 2 * HALO, LANES)),
        _resident((s + 2 * HALO, LANES)),
        _resident((3, QB, 3 * QB)),
        pl.BlockSpec((1, MEM_LEN, d), lambda b, i: (b, 0, 0)),
        _resident((1, d)),
        hbm, hbm, hbm, hbm, hbm,
        _resident((1, d)),
    ]
    scratch = [
        pltpu.VMEM((EXT, d), BF16),
        pltpu.VMEM((KV_WIDTH, EXT), BF16),
        pltpu.VMEM((KV_WIDTH, EXT), BF16),
        pltpu.VMEM((EXT, 2 * KV_WIDTH), BF16),
        pltpu.VMEM((EXT, 2 * KV_WIDTH), BF16),
        pltpu.VMEM((SB + 2 * CONV_HALO, A_WIDTH), F32),
        pltpu.VMEM((SB, d), F32),
        pltpu.VMEM((3, SB, d), F32),
        pltpu.VMEM((d, IN_WIDTH), BF16),
        pltpu.VMEM((d, 2 * MEM_WIDTH), BF16),
        pltpu.VMEM((A_WIDTH, d), BF16),
        pltpu.VMEM((ATTN_WIDTH, d), BF16),
        pltpu.VMEM((MEM_WIDTH, d), BF16),
        pltpu.VMEM((d, d), BF16),
        pltpu.VMEM((MEM_WIDTH, MEM_LEN), BF16),
        pltpu.VMEM((MEM_HEADS, MEM_LEN, 2 * MEM_HEAD_DIM), BF16),
        pltpu.SemaphoreType.DMA((4 + OUT_SLOTS // 2,)),
        pltpu.VMEM((OUT_SLOTS // 2, STAGE_ROWS, d), F32),
        pltpu.SemaphoreType.DMA((1,)),
    ]
    out = pl.pallas_call(
        _layer_kernel,
        grid=(bsz, s // TS),
        in_specs=in_specs,
        out_specs=hbm,
        out_shape=jax.ShapeDtypeStruct((bsz * s, d), x.dtype),
        scratch_shapes=scratch,
        compiler_params=pltpu.CompilerParams(dimension_semantics=("arbitrary", "arbitrary"),
                                             vmem_limit_bytes=VMEM_LIMIT_BYTES),
        name="hybrid_layer",
    )(attn_sink, x, x, g_pre.reshape(1, d), w_in, w_conv, cos_t, sin_t, bias_t, mem, g_mem.reshape(1, d),
      w_mem_kv, w_up_a, w_up_b, w_up_m, w_out, g_post.reshape(1, d))
    return out.reshape(bsz, s, d)


def kernel(x, mem, g_pre, w_in, w_conv, attn_sink, g_mem, w_mem_kv, w_up_a, w_up_b, w_up_m, w_out, g_post):
    cos_np, sin_np = _rope_tables()
    cos_t, sin_t, bias_t = jnp.asarray(cos_np), jnp.asarray(sin_np), jnp.asarray(_band_bias())
    for l in range(g_pre.shape[0]):
        x = _layer(x, mem, g_pre[l], w_in[l], w_conv[l], attn_sink[l], g_mem[l], w_mem_kv[l], w_up_a[l], w_up_b[l],
                   w_up_m[l], w_out[l], g_post[l], cos_t, sin_t, bias_t)
    return x
```

```python
import math

import numpy as np
import jax
import jax.numpy as jnp
from jax import lax
from jax.experimental import pallas as pl
from jax.experimental.pallas import tpu as pltpu

F32 = jnp.float32
BF16 = jnp.bfloat16

D_MODEL = 1024
SEQ = 4096
MEM_LEN = 256
EPS = 1e-6
CONV_WIDTH = 3
A_WIDTH = 512
HEAD_DIM = 64
ATTN_WIDTH = 512
N_Q_HEADS = 8
KV_WIDTH = 128
WINDOW = 128
ROPE_THETA = 500000.0
ROT_DIM = 16
MEM_HEADS = 4
MEM_HEAD_DIM = 128
MEM_WIDTH = 512

C_AB, C_AC, C_AX, C_AZ = 0, 512, 1024, 1536
C_BQ, C_BK, C_BV, C_BZ = 2048, 2560, 2688, 2816
C_MQ, C_MZ = 3328, 3840
C_G0, C_G1, C_G2 = 4352, 5376, 6400
IN_WIDTH = 7424

LANES = 128
BF16_ROWS = 16
VMEM_LIMIT_BYTES = 62 * 1024 * 1024

SB = 256
NSB = 4
TS = SB * NSB
QB = WINDOW
HALO = WINDOW
CONV_HALO = BF16_ROWS
EXT = TS + 2 * HALO
STAGE_ROWS = SB
STAGE_COLS = D_MODEL
OUT_ROWS = SB // 2
OUT_SLOTS = TS // OUT_ROWS

LOG2E = math.log2(math.e)


def _sigmoid(v):
    return 0.5 * jnp.tanh(0.5 * v) + 0.5


def _silu(v):
    return v * _sigmoid(v)


def _rms(v, g):
    ms = jnp.mean(v * v, axis=-1, keepdims=True)
    return v * lax.rsqrt(ms + EPS) * g


def _dot(a, b):
    return jnp.dot(a, b, preferred_element_type=F32)


def _rope(t, cos, ssin, low8):
    partner = jnp.where(low8, pltpu.roll(t, LANES - ROT_DIM // 2, 1), pltpu.roll(t, ROT_DIM // 2, 1))
    return t * cos + partner * ssin


def _rope_tables():
    half = ROT_DIM // 2
    inv_freq = np.power(np.float32(ROPE_THETA), -np.arange(half, dtype=np.float32) * np.float32(2.0 / ROT_DIM))
    pos = (np.arange(SEQ + 2 * HALO) - HALO).astype(np.float32)
    ang = (pos[:, None] * inv_freq[None, :]).astype(np.float32)
    cos, sin = np.cos(ang).astype(np.float32), np.sin(ang).astype(np.float32)
    ct = np.ones((SEQ + 2 * HALO, HEAD_DIM), np.float32)
    st = np.zeros((SEQ + 2 * HALO, HEAD_DIM), np.float32)
    ct[:, :half], ct[:, half:ROT_DIM] = cos, cos
    st[:, :half], st[:, half:ROT_DIM] = -sin, sin
    return np.tile(ct, (1, LANES // HEAD_DIM)), np.tile(st, (1, LANES // HEAD_DIM))


def _band_bias():
    r = np.arange(QB)[:, None]
    c = np.arange(3 * QB)[None, :]
    band = (c >= r) & (c <= r + 2 * WINDOW)
    masks = [band, band & (c >= QB), band & (c < 2 * QB)]
    return np.stack([np.where(m, 0.0, -np.inf) for m in masks]).astype(np.float32)


def _stream_cast(jobs, slots, sem):
    chunks = [(src, dst, r0, c0, min(STAGE_COLS, src.shape[1] - c0))
              for src, dst in jobs
              for r0 in range(0, src.shape[0], STAGE_ROWS)
              for c0 in range(0, src.shape[1], STAGE_COLS)]
    n_slots = len(slots)

    def copy(n):
        src, _, r0, c0, cols = chunks[n]
        k = n % n_slots
        return pltpu.make_async_copy(src.at[pl.ds(r0, STAGE_ROWS), pl.ds(c0, cols)],
                                     slots[k].at[:, pl.ds(0, cols)], sem.at[k])

    for n in range(min(n_slots, len(chunks))):
        copy(n).start()
    for n, (_, dst, r0, c0, cols) in enumerate(chunks):
        copy(n).wait()
        dst[r0:r0 + STAGE_ROWS, c0:c0 + cols] = slots[n % n_slots][:, 0:cols].astype(BF16)
        if n + n_slots < len(chunks):
            copy(n + n_slots).start()


def _window_softmax_pv(s, v_win, sink_col):
    m = jnp.maximum(jnp.max(s, axis=1, keepdims=True), sink_col)
    p = jnp.exp2(s - m)
    ov = _dot(p.astype(BF16), v_win)
    return ov[:, :LANES] / (ov[:, LANES:] + jnp.exp2(sink_col - m))


def _layer_kernel(sink_ref, x_ref, xn_ref, gpre_ref, win_hbm, wconv_ref, cos_ref, sin_ref, bias_ref,
                  mem_ref, gmem_ref, wmkv_hbm, wua_hbm, wub_hbm, wum_hbm, wout_hbm, gpost_ref,
                  out_hbm, h_s, k_s, ksw_s, v_s, vsw_s, cu_s, mrg_s, gate_s,
                  win_ref, wmkv_ref, wua_ref, wub_ref, wum_ref, wout_ref, mkt_ref, mvx_ref,
                  stage_sem, out_s, out_sem):
    b = pl.program_id(0)
    i = pl.program_id(1)
    n_tiles = pl.num_programs(1)
    step = b * n_tiles + i
    last_step = pl.num_programs(0) * n_tiles - 1
    row0 = pl.multiple_of(i * TS, TS)
    tile_row = step * TS
    g_pre = gpre_ref[...]
    g_post = gpost_ref[...]

    def out_slot(slot):
        return out_s.at[slot // 2, pl.ds((slot % 2) * OUT_ROWS, OUT_ROWS)]

    def out_copy(slot, dst_row):
        return pltpu.make_async_copy(out_slot(slot), out_hbm.at[pl.ds(dst_row, OUT_ROWS)], out_sem.at[0])

    pending_out = []

    def write_out(slot, value, dst_row):
        out_slot(slot)[...] = value
        pending_out.append((slot, dst_row))

    def flush_out():
        for slot, dst_row in pending_out:
            out_copy(slot, dst_row).start()
        pending_out.clear()

    def wait_out_slots():
        for slot in range(OUT_SLOTS):
            out_copy(slot, 0).wait()

    @pl.when(step == 0)
    def _first_step():
        slots = [gate_s.at[k] for k in range(3)] + [mrg_s] + [out_s.at[k] for k in range(OUT_SLOTS // 2)]
        _stream_cast([(win_hbm, win_ref), (wmkv_hbm, wmkv_ref), (wua_hbm, wua_ref), (wub_hbm, wub_ref),
                      (wum_hbm, wum_ref), (wout_hbm, wout_ref)], slots, stage_sem)
        for buf in (k_s, ksw_s):
            buf[:, TS:TS + HALO] = jnp.zeros((KV_WIDTH, HALO), BF16)
        for buf in (v_s, vsw_s):
            buf[TS:TS + HALO] = jnp.zeros((HALO, buf.shape[1]), BF16)
        cu_s[...] = jnp.zeros(cu_s.shape, F32)
        out_s[...] = jnp.zeros(out_s.shape, F32)
        for slot in range(OUT_SLOTS):
            out_copy(slot, slot * OUT_ROWS).start()

    @pl.when(i == 0)
    def _memory_kv():
        mn = _rms(mem_ref[0], gmem_ref[...]).astype(BF16)
        kv = _dot(mn, wmkv_ref[...])
        mkt_ref[...] = kv[:, :MEM_WIDTH].T.astype(BF16)
        ones_m = jnp.ones((MEM_LEN, MEM_HEAD_DIM), F32)
        for hh in range(MEM_HEADS):
            c0 = MEM_WIDTH + hh * MEM_HEAD_DIM
            mvx_ref[hh] = jnp.concatenate([kv[:, c0:c0 + MEM_HEAD_DIM], ones_m], axis=1).astype(BF16)

    lane = lax.broadcasted_iota(jnp.int32, (1, LANES), 1)
    low_head = lane < HEAD_DIM
    low8 = (lane % HEAD_DIM) < (ROT_DIM // 2)

    rows4 = lax.broadcasted_iota(jnp.int32, (4 * QB, 1), 0)
    def sink_rows(h0, h1, h2, h3):
        return LOG2E * jnp.where(rows4 < QB, sink_ref[h0],
                                 jnp.where(rows4 < 2 * QB, sink_ref[h1],
                                           jnp.where(rows4 < 3 * QB, sink_ref[h2], sink_ref[h3])))
    sink_a = sink_rows(0, 2, 5, 7)
    sink_b = sink_rows(1, 3, 4, 6)
    n_blocks = SEQ // QB
    ones = jnp.ones((SB, LANES), BF16)

    def norm_main(sb):
        r0 = HALO + sb * SB
        h_s[r0:r0 + SB] = _rms(x_ref[0, sb * SB:(sb + 1) * SB], g_pre).astype(BF16)

    def norm_next_halo():
        h_s[HALO + TS:EXT] = _rms(xn_ref[0], g_pre).astype(BF16)

    def carry_prev_halo():
        for buf in (k_s, ksw_s):
            buf[:, 0:HALO] = buf[:, TS:TS + HALO]
        for buf in (v_s, vsw_s):
            buf[0:HALO] = buf[TS:TS + HALO]

    def kv_chunk(c):
        r0 = HALO if c == 0 else c * SB
        r1 = (c + 1) * SB
        n = r1 - r0
        kv = _dot(h_s[r0:r1], win_ref[:, C_BK:C_BV + KV_WIDTH])
        cos = cos_ref[pl.ds(row0 + r0, n), :]
        ssin = sin_ref[pl.ds(row0 + r0, n), :]
        k = _rope(kv[:, :KV_WIDTH], cos, ssin, low8)
        v = kv[:, KV_WIDTH:]
        k_s[:, r0:r1] = k.T.astype(BF16)
        ksw_s[:, r0:r1] = pltpu.roll(k, HEAD_DIM, 1).T.astype(BF16)
        v_s[r0:r1, 0:LANES] = v.astype(BF16)
        v_s[r0:r1, LANES:2 * LANES] = ones[0:n]
        vsw_s[r0:r1, 0:LANES] = pltpu.roll(v, HEAD_DIM, 1).astype(BF16)
        vsw_s[r0:r1, LANES:2 * LANES] = ones[0:n]

    def sub_block(sb):
        m0 = HALO + sb * SB

        def gate(col, slot):
            gate_s[slot] = _sigmoid(_dot(h, win_ref[:, col:col + D_MODEL]))

        def conv_input():
            have = CONV_HALO if sb > 0 else 0
            before = cu_s[SB:SB + CONV_HALO + have]
            hx = h_s[m0 + have:m0 + SB + CONV_HALO]
            cx = _dot(hx, win_ref[:, C_AC:C_AX + A_WIDTH])
            cu = cx[:, :A_WIDTH] * cx[:, A_WIDTH:]
            first = row0 + sb * SB == 0
            last = row0 + (sb + 1) * SB == SEQ
            cu_s[0:CONV_HALO + have] = before if sb > 0 else jnp.where(first, 0.0, before)
            cu_s[CONV_HALO + have:CONV_HALO + SB] = cu[0:SB - have]
            cu_s[CONV_HALO + SB:] = jnp.where(last, 0.0, cu[SB - have:])

        norm_main(sb)
        yield
        h = h_s[m0:m0 + SB]
        q = _dot(h, win_ref[:, C_BQ:C_BQ + ATTN_WIDTH])
        b_zs = _silu(_dot(h, win_ref[:, C_BZ:C_BZ + ATTN_WIDTH]))
        yield
        a_b = _dot(h, win_ref[:, C_AB:C_AB + A_WIDTH])
        a_zs = _silu(_dot(h, win_ref[:, C_AZ:C_AZ + A_WIDTH]))
        scale = (HEAD_DIM ** -0.5) * LOG2E
        qcos = cos_ref[pl.ds(row0 + m0, SB), :] * scale
        qsin = sin_ref[pl.ds(row0 + m0, SB), :] * scale
        q_lo, q_hi = [], []
        for mblk in range(ATTN_WIDTH // LANES):
            qr = _rope(q[:, mblk * LANES:(mblk + 1) * LANES], qcos, qsin, low8)
            q_lo.append(jnp.where(low_head, qr, 0.0).astype(BF16))
            q_hi.append(jnp.where(low_head, 0.0, qr).astype(BF16))
        yield
        fillers = [lambda: gate(C_G1, 1), conv_input]
        yb_blocks = []
        for j in range(SB // QB):
            qs = slice(j * QB, (j + 1) * QB)
            jb = sb * (SB // QB) + j
            blk = i * (TS // QB) + jb
            variant = jnp.where(blk == 0, 1, jnp.where(blk == n_blocks - 1, 2, 0))
            bias1 = bias_ref[variant]
            bias = jnp.concatenate([bias1, bias1, bias1, bias1], axis=0)
            lhs_a = jnp.concatenate([q_lo[0][qs], q_lo[1][qs], q_hi[2][qs], q_hi[3][qs]], axis=0)
            lhs_b = jnp.concatenate([q_hi[0][qs], q_hi[1][qs], q_lo[2][qs], q_lo[3][qs]], axis=0)
            ws = slice(jb * QB, jb * QB + 3 * QB)
            s_a = _dot(lhs_a, k_s[:, ws]) + bias
            s_b = _dot(lhs_b, ksw_s[:, ws]) + bias
            fillers[j % len(fillers)]()
            o_a = _window_softmax_pv(s_a, v_s[ws], sink_a)
            o_b = _window_softmax_pv(s_b, vsw_s[ws], sink_b)
            cols = [jnp.where(low_head, o_a[0:QB], o_b[0:QB]),
                    jnp.where(low_head, o_a[QB:2 * QB], o_b[QB:2 * QB]),
                    jnp.where(low_head, o_b[2 * QB:3 * QB], o_a[2 * QB:3 * QB]),
                    jnp.where(low_head, o_b[3 * QB:4 * QB], o_a[3 * QB:4 * QB])]
            yb_blocks.append(jnp.concatenate(cols, axis=1))
            yield
        yb = jnp.concatenate(yb_blocks, axis=0) * b_zs
        ub = _dot(yb.astype(BF16), wub_ref[...])
        mrg_s[...] = gate_s[1] * ub
        gate(C_G0, 0)
        y = (cu_s[CONV_HALO - 1:CONV_HALO - 1 + SB] * wconv_ref[0:1]
             + cu_s[CONV_HALO:CONV_HALO + SB] * wconv_ref[1:2]
             + cu_s[CONV_HALO + 1:CONV_HALO + 1 + SB] * wconv_ref[2:3])
        ya = a_b * y * a_zs
        ua = _dot(ya.astype(BF16), wua_ref[...])
        mrg_s[...] += gate_s[0] * ua
        yield
        mq = _dot(h, win_ref[:, C_MQ:C_MQ + MEM_WIDTH]) * ((MEM_HEAD_DIM ** -0.5) * LOG2E)
        m_zs = _silu(_dot(h, win_ref[:, C_MZ:C_MZ + MEM_WIDTH]))
        scores = []
        for hh in range(MEM_HEADS):
            hs = slice(hh * MEM_HEAD_DIM, (hh + 1) * MEM_HEAD_DIM)
            scores.append(_dot(mq[:, hs].astype(BF16), mkt_ref[hs, :]))
        gate(C_G2, 2)
        ym_heads = []
        for hh in range(MEM_HEADS):
            s = scores[hh]
            p = jnp.exp2(s - jnp.max(s, axis=1, keepdims=True))
            ov = _dot(p.astype(BF16), mvx_ref[hh])
            ym_heads.append(ov[:, :MEM_HEAD_DIM] / ov[:, MEM_HEAD_DIM:])
        ym = (jnp.concatenate(ym_heads, axis=1) * m_zs).astype(BF16)
        yield
        for half in range(SB // OUT_ROWS):
            rs = slice(half * OUT_ROWS, (half + 1) * OUT_ROWS)
            t0 = sb * SB + half * OUT_ROWS
            um = _dot(ym[rs], wum_ref[...])
            merged = mrg_s[rs] + gate_s[2, rs] * um
            o = _dot(merged.astype(BF16), wout_ref[...])
            write_out(t0 // OUT_ROWS, x_ref[0, t0:t0 + OUT_ROWS] + _rms(o, g_post), tile_row + t0)
        yield

    gens = [sub_block(sb) for sb in range(NSB)]
    def run(sb, count=1):
        for _ in range(count):
            next(gens[sb])

    def norm_for_chunk(c):
        if c < NSB:
            run(c)
        else:
            norm_next_halo()

    wait_out_slots()
    carry_prev_halo()
    run(0, 2)
    norm_for_chunk(1)
    kv_chunk(0)
    kv_chunk(1)
    for sb in range(NSB):
        run(sb)
        if sb > 0:
            run(sb - 1)
        run(sb, 2)
        if sb + 1 < NSB:
            norm_for_chunk(sb + 2)
        run(sb)
        flush_out()
        run(sb)
        if sb + 1 < NSB:
            run(sb + 1)
            kv_chunk(sb + 2)
    run(NSB - 1)
    flush_out()

    @pl.when(step == last_step)
    def _last_step():
        wait_out_slots()


def _resident(shape):
    return pl.BlockSpec(shape, lambda b, i: (0,) * len(shape), pipeline_mode=pl.Buffered(1))


def _layer(x, mem, g_pre, w_in, w_conv, attn_sink, g_mem, w_mem_kv, w_up_a, w_up_b, w_up_m, w_out, g_post,
           cos_t, sin_t, bias_t):
    bsz, s, d = x.shape
    assert (s, d) == (SEQ, D_MODEL) and s % TS == 0 and SB % QB == 0 and SB == 2 * HALO
    assert (STAGE_ROWS, STAGE_COLS) == (SB, d)
    nb = TS // HALO
    hbm = pl.BlockSpec(memory_space=pl.ANY)
    in_specs = [
        pl.BlockSpec(memory_space=pltpu.SMEM),
        pl.BlockSpec((1, TS, d), lambda b, i: (b, i, 0)),
        pl.BlockSpec((1, HALO, d), lambda b, i: (b, jnp.minimum((i + 1) * nb, s // HALO - 1), 0)),
        _resident((1, d)),
        hbm,
        _resident((CONV_WIDTH, A_WIDTH)),
        _resident((s + 2 * HALO, LANES)),
        _resident((s + 2 * HALO, LANES)),
        _resident((3, QB, 3 * QB)),
        pl.BlockSpec((1, MEM_LEN, d), lambda b, i: (b, 0, 0)),
        _resident((1, d)),
        hbm, hbm, hbm, hbm, hbm,
        _resident((1, d)),
    ]
    scratch = [
        pltpu.VMEM((EXT, d), BF16),
        pltpu.VMEM((KV_WIDTH, EXT), BF16),
        pltpu.VMEM((KV_WIDTH, EXT), BF16),
        pltpu.VMEM((EXT, 2 * KV_WIDTH), BF16),
        pltpu.VMEM((EXT, 2 * KV_WIDTH), BF16),
        pltpu.VMEM((SB + 2 * CONV_HALO, A_WIDTH), F32),
        pltpu.VMEM((SB, d), F32),
        pltpu.VMEM((3, SB, d), F32),
        pltpu.VMEM((d, IN_WIDTH), BF16),
        pltpu.VMEM((d, 2 * MEM_WIDTH), BF16),
        pltpu.VMEM((A_WIDTH, d), BF16),
        pltpu.VMEM((ATTN_WIDTH, d), BF16),
        pltpu.VMEM((MEM_WIDTH, d), BF16),
        pltpu.VMEM((d, d), BF16),
        pltpu.VMEM((MEM_WIDTH, MEM_LEN), BF16),
        pltpu.VMEM((MEM_HEADS, MEM_LEN, 2 * MEM_HEAD_DIM), BF16),
        pltpu.SemaphoreType.DMA((4 + OUT_SLOTS // 2,)),
        pltpu.VMEM((OUT_SLOTS // 2, STAGE_ROWS, d), F32),
        pltpu.SemaphoreType.DMA((1,)),
    ]
    out = pl.pallas_call(
        _layer_kernel,
        grid=(bsz, s // TS),
        in_specs=in_specs,
        out_specs=hbm,
        out_shape=jax.ShapeDtypeStruct((bsz * s, d), x.dtype),
        scratch_shapes=scratch,
        compiler_params=pltpu.CompilerParams(dimension_semantics=("arbitrary", "arbitrary"),
                                             vmem_limit_bytes=VMEM_LIMIT_BYTES),
        name="hybrid_layer",
    )(attn_sink, x, x, g_pre.reshape(1, d), w_in, w_conv, cos_t, sin_t, bias_t, mem, g_mem.reshape(1, d),
      w_mem_kv, w_up_a, w_up_b, w_up_m, w_out, g_post.reshape(1, d))
    return out.reshape(bsz, s, d)


def kernel(x, mem, g_pre, w_in, w_conv, attn_sink, g_mem, w_mem_kv, w_up_a, w_up_b, w_up_m, w_out, g_post):
    cos_np, sin_np = _rope_tables()
    cos_t, sin_t, bias_t = jnp.asarray(cos_np), jnp.asarray(sin_np), jnp.asarray(_band_bias())
    for l in range(g_pre.shape[0]):
        x = _layer(x, mem, g_pre[l], w_in[l], w_conv[l], attn_sink[l], g_mem[l], w_mem_kv[l], w_up_a[l], w_up_b[l],
                   w_up_m[l], w_out[l], g_post[l], cos_t, sin_t, bias_t)
    return x
```

```python
import math

import numpy as np
import jax
import jax.numpy as jnp
from jax import lax
from jax.experimental import pallas as pl
from jax.experimental.pallas import tpu as pltpu

F32 = jnp.float32
BF16 = jnp.bfloat16

D_MODEL = 1024
SEQ = 4096
MEM_LEN = 256
EPS = 1e-6
CONV_WIDTH = 3
A_WIDTH = 512
HEAD_DIM = 64
ATTN_WIDTH = 512
N_Q_HEADS = 8
KV_WIDTH = 128
WINDOW = 128
ROPE_THETA = 500000.0
ROT_DIM = 16
MEM_HEADS = 4
MEM_HEAD_DIM = 128
MEM_WIDTH = 512

C_AB, C_AC, C_AX, C_AZ = 0, 512, 1024, 1536
C_BQ, C_BK, C_BV, C_BZ = 2048, 2560, 2688, 2816
C_MQ, C_MZ = 3328, 3840
C_G0, C_G1, C_G2 = 4352, 5376, 6400
IN_WIDTH = 7424

LANES = 128
BF16_ROWS = 16
VMEM_LIMIT_BYTES = 62 * 1024 * 1024

SB = 256
NSB = 4
TS = SB * NSB
QB = WINDOW
HALO = WINDOW
CONV_HALO = BF16_ROWS
EXT = TS + 2 * HALO
STAGE_ROWS = SB
STAGE_COLS = D_MODEL
OUT_ROWS = SB // 2
OUT_SLOTS = TS // OUT_ROWS

LOG2E = math.log2(math.e)


def _silu_half(hv):
    return hv * jnp.tanh(hv) + hv


def _gated(t, hu):
    return t * hu + hu


def _in_scale():
    sc = np.ones((1, IN_WIDTH), np.float32)
    for c0, width in ((C_AZ, A_WIDTH), (C_BZ, ATTN_WIDTH), (C_MZ, MEM_WIDTH), (C_G0, 3 * D_MODEL)):
        sc[:, c0:c0 + width] = 0.5
    return sc


def _rms(v, g):
    ms = jnp.mean(v * v, axis=-1, keepdims=True)
    return v * lax.rsqrt(ms + EPS) * g


def _dot(a, b):
    return jnp.dot(a, b, preferred_element_type=F32)


def _rope(t, cos, ssin, low8):
    partner = jnp.where(low8, pltpu.roll(t, LANES - ROT_DIM // 2, 1), pltpu.roll(t, ROT_DIM // 2, 1))
    return t * cos + partner * ssin


def _rope_tables():
    half = ROT_DIM // 2
    inv_freq = np.power(np.float32(ROPE_THETA), -np.arange(half, dtype=np.float32) * np.float32(2.0 / ROT_DIM))
    pos = (np.arange(SEQ + 2 * HALO) - HALO).astype(np.float32)
    ang = (pos[:, None] * inv_freq[None, :]).astype(np.float32)
    cos, sin = np.cos(ang).astype(np.float32), np.sin(ang).astype(np.float32)
    ct = np.ones((SEQ + 2 * HALO, HEAD_DIM), np.float32)
    st = np.zeros((SEQ + 2 * HALO, HEAD_DIM), np.float32)
    ct[:, :half], ct[:, half:ROT_DIM] = cos, cos
    st[:, :half], st[:, half:ROT_DIM] = -sin, sin
    return np.tile(ct, (1, LANES // HEAD_DIM)), np.tile(st, (1, LANES // HEAD_DIM))


def _band_bias():
    r = np.arange(QB)[:, None]
    c = np.arange(3 * QB)[None, :]
    band = (c >= r) & (c <= r + 2 * WINDOW)
    masks = [band, band & (c >= QB), band & (c < 2 * QB)]
    return np.stack([np.where(m, 0.0, -np.inf) for m in masks]).astype(np.float32)


def _stream_cast(jobs, slots, sem):
    chunks = [(src, dst, scale, r0, c0, min(STAGE_COLS, src.shape[1] - c0))
              for src, dst, scale in jobs
              for r0 in range(0, src.shape[0], STAGE_ROWS)
              for c0 in range(0, src.shape[1], STAGE_COLS)]
    n_slots = len(slots)

    def copy(n):
        src, _, _, r0, c0, cols = chunks[n]
        k = n % n_slots
        return pltpu.make_async_copy(src.at[pl.ds(r0, STAGE_ROWS), pl.ds(c0, cols)],
                                     slots[k].at[:, pl.ds(0, cols)], sem.at[k])

    for n in range(min(n_slots, len(chunks))):
        copy(n).start()
    for n, (_, dst, scale, r0, c0, cols) in enumerate(chunks):
        copy(n).wait()
        w = slots[n % n_slots][:, 0:cols]
        w = w * (scale if isinstance(scale, float) else scale[:, c0:c0 + cols])
        dst[r0:r0 + STAGE_ROWS, c0:c0 + cols] = w.astype(BF16)
        if n + n_slots < len(chunks):
            copy(n + n_slots).start()


def _window_softmax_pv(s, v_win, sink_col):
    m = jnp.maximum(jnp.max(s, axis=1, keepdims=True), sink_col)
    p = jnp.exp2(s - m)
    ov = _dot(p.astype(BF16), v_win)
    return ov[:, :LANES] / (ov[:, LANES:] + jnp.exp2(sink_col - m))


def _layer_kernel(sink_ref, x_ref, xn_ref, gpre_ref, win_hbm, wconv_ref, cos_ref, sin_ref, bias_ref, wscale_ref,
                  mem_ref, gmem_ref, wmkv_hbm, wua_hbm, wub_hbm, wum_hbm, wout_hbm, gpost_ref,
                  out_hbm, h_s, k_s, ksw_s, v_s, vsw_s, cu_s, mrg_s, gate_s,
                  win_ref, wmkv_ref, wua_ref, wub_ref, wum_ref, wout_ref, mkt_ref, mvx_ref,
                  stage_sem, out_s, out_sem):
    b = pl.program_id(0)
    i = pl.program_id(1)
    n_tiles = pl.num_programs(1)
    step = b * n_tiles + i
    last_step = pl.num_programs(0) * n_tiles - 1
    row0 = pl.multiple_of(i * TS, TS)
    tile_row = step * TS
    g_pre = gpre_ref[...]
    g_post = gpost_ref[...]

    def out_slot(slot):
        return out_s.at[slot // 2, pl.ds((slot % 2) * OUT_ROWS, OUT_ROWS)]

    def out_copy(slot, dst_row):
        return pltpu.make_async_copy(out_slot(slot), out_hbm.at[pl.ds(dst_row, OUT_ROWS)], out_sem.at[0])

    pending_out = []

    def write_out(slot, value, dst_row):
        out_slot(slot)[...] = value
        pending_out.append((slot, dst_row))

    def flush_out():
        for slot, dst_row in pending_out:
            out_copy(slot, dst_row).start()
        pending_out.clear()

    def wait_out_slots():
        for slot in range(OUT_SLOTS):
            out_copy(slot, 0).wait()

    @pl.when(step == 0)
    def _first_step():
        slots = [gate_s.at[k] for k in range(3)] + [mrg_s] + [out_s.at[k] for k in range(OUT_SLOTS // 2)]
        _stream_cast([(win_hbm, win_ref, wscale_ref), (wmkv_hbm, wmkv_ref, 1.0), (wua_hbm, wua_ref, 0.5),
                      (wub_hbm, wub_ref, 0.5), (wum_hbm, wum_ref, 0.5), (wout_hbm, wout_ref, 1.0)],
                     slots, stage_sem)
        for buf in (k_s, ksw_s):
            buf[:, TS:TS + HALO] = jnp.zeros((KV_WIDTH, HALO), BF16)
        for buf in (v_s, vsw_s):
            buf[TS:TS + HALO] = jnp.zeros((HALO, buf.shape[1]), BF16)
        cu_s[...] = jnp.zeros(cu_s.shape, F32)
        out_s[...] = jnp.zeros(out_s.shape, F32)
        for slot in range(OUT_SLOTS):
            out_copy(slot, slot * OUT_ROWS).start()

    @pl.when(i == 0)
    def _memory_kv():
        mn = _rms(mem_ref[0], gmem_ref[...]).astype(BF16)
        kv = _dot(mn, wmkv_ref[...])
        mkt_ref[...] = kv[:, :MEM_WIDTH].T.astype(BF16)
        ones_m = jnp.ones((MEM_LEN, MEM_HEAD_DIM), F32)
        for hh in range(MEM_HEADS):
            c0 = MEM_WIDTH + hh * MEM_HEAD_DIM
            mvx_ref[hh] = jnp.concatenate([kv[:, c0:c0 + MEM_HEAD_DIM], ones_m], axis=1).astype(BF16)

    lane = lax.broadcasted_iota(jnp.int32, (1, LANES), 1)
    low_head = lane < HEAD_DIM
    low8 = (lane % HEAD_DIM) < (ROT_DIM // 2)

    rows4 = lax.broadcasted_iota(jnp.int32, (4 * QB, 1), 0)
    def sink_rows(h0, h1, h2, h3):
        return LOG2E * jnp.where(rows4 < QB, sink_ref[h0],
                                 jnp.where(rows4 < 2 * QB, sink_ref[h1],
                                           jnp.where(rows4 < 3 * QB, sink_ref[h2], sink_ref[h3])))
    sink_a = sink_rows(0, 2, 5, 7)
    sink_b = sink_rows(1, 3, 4, 6)
    n_blocks = SEQ // QB
    ones = jnp.ones((SB, LANES), BF16)

    def norm_main(sb):
        r0 = HALO + sb * SB
        h_s[r0:r0 + SB] = _rms(x_ref[0, sb * SB:(sb + 1) * SB], g_pre).astype(BF16)

    def norm_next_halo():
        h_s[HALO + TS:EXT] = _rms(xn_ref[0], g_pre).astype(BF16)

    def carry_prev_halo():
        for buf in (k_s, ksw_s):
            buf[:, 0:HALO] = buf[:, TS:TS + HALO]
        for buf in (v_s, vsw_s):
            buf[0:HALO] = buf[TS:TS + HALO]

    def kv_chunk(c):
        r0 = HALO if c == 0 else c * SB
        r1 = (c + 1) * SB
        n = r1 - r0
        kv = _dot(h_s[r0:r1], win_ref[:, C_BK:C_BV + KV_WIDTH])
        cos = cos_ref[pl.ds(row0 + r0, n), :]
        ssin = sin_ref[pl.ds(row0 + r0, n), :]
        k = _rope(kv[:, :KV_WIDTH], cos, ssin, low8)
        v = kv[:, KV_WIDTH:]
        k_s[:, r0:r1] = k.T.astype(BF16)
        ksw_s[:, r0:r1] = pltpu.roll(k, HEAD_DIM, 1).T.astype(BF16)
        v_s[r0:r1, 0:LANES] = v.astype(BF16)
        v_s[r0:r1, LANES:2 * LANES] = ones[0:n]
        vsw_s[r0:r1, 0:LANES] = pltpu.roll(v, HEAD_DIM, 1).astype(BF16)
        vsw_s[r0:r1, LANES:2 * LANES] = ones[0:n]

    def sub_block(sb):
        m0 = HALO + sb * SB

        def gate(col, slot):
            gate_s[slot] = jnp.tanh(_dot(h, win_ref[:, col:col + D_MODEL]))

        def conv_input():
            have = CONV_HALO if sb > 0 else 0
            before = cu_s[SB:SB + CONV_HALO + have]
            hx = h_s[m0 + have:m0 + SB + CONV_HALO]
            cx = _dot(hx, win_ref[:, C_AC:C_AX + A_WIDTH])
            cu = cx[:, :A_WIDTH] * cx[:, A_WIDTH:]
            first = row0 + sb * SB == 0
            last = row0 + (sb + 1) * SB == SEQ
            cu_s[0:CONV_HALO + have] = before if sb > 0 else jnp.where(first, 0.0, before)
            cu_s[CONV_HALO + have:CONV_HALO + SB] = cu[0:SB - have]
            cu_s[CONV_HALO + SB:] = jnp.where(last, 0.0, cu[SB - have:])

        norm_main(sb)
        yield
        h = h_s[m0:m0 + SB]
        q = _dot(h, win_ref[:, C_BQ:C_BQ + ATTN_WIDTH])
        b_zs = _silu_half(_dot(h, win_ref[:, C_BZ:C_BZ + ATTN_WIDTH]))
        yield
        a_b = _dot(h, win_ref[:, C_AB:C_AB + A_WIDTH])
        a_zs = _silu_half(_dot(h, win_ref[:, C_AZ:C_AZ + A_WIDTH]))
        scale = (HEAD_DIM ** -0.5) * LOG2E
        qcos = cos_ref[pl.ds(row0 + m0, SB), :] * scale
        qsin = sin_ref[pl.ds(row0 + m0, SB), :] * scale
        q_lo, q_hi = [], []
        for mblk in range(ATTN_WIDTH // LANES):
            qr = _rope(q[:, mblk * LANES:(mblk + 1) * LANES], qcos, qsin, low8)
            q_lo.append(jnp.where(low_head, qr, 0.0).astype(BF16))
            q_hi.append(jnp.where(low_head, 0.0, qr).astype(BF16))
        yield
        fillers = [lambda: gate(C_G1, 1), conv_input]
        yb_blocks = []
        for j in range(SB // QB):
            qs = slice(j * QB, (j + 1) * QB)
            jb = sb * (SB // QB) + j
            blk = i * (TS // QB) + jb
            variant = jnp.where(blk == 0, 1, jnp.where(blk == n_blocks - 1, 2, 0))
            bias1 = bias_ref[variant]
            bias = jnp.concatenate([bias1, bias1, bias1, bias1], axis=0)
            lhs_a = jnp.concatenate([q_lo[0][qs], q_lo[1][qs], q_hi[2][qs], q_hi[3][qs]], axis=0)
            lhs_b = jnp.concatenate([q_hi[0][qs], q_hi[1][qs], q_lo[2][qs], q_lo[3][qs]], axis=0)
            ws = slice(jb * QB, jb * QB + 3 * QB)
            s_a = _dot(lhs_a, k_s[:, ws]) + bias
            s_b = _dot(lhs_b, ksw_s[:, ws]) + bias
            fillers[j % len(fillers)]()
            o_a = _window_softmax_pv(s_a, v_s[ws], sink_a)
            o_b = _window_softmax_pv(s_b, vsw_s[ws], sink_b)
            cols = [jnp.where(low_head, o_a[0:QB], o_b[0:QB]),
                    jnp.where(low_head, o_a[QB:2 * QB], o_b[QB:2 * QB]),
                    jnp.where(low_head, o_b[2 * QB:3 * QB], o_a[2 * QB:3 * QB]),
                    jnp.where(low_head, o_b[3 * QB:4 * QB], o_a[3 * QB:4 * QB])]
            yb_blocks.append(jnp.concatenate(cols, axis=1))
            yield
        yb = jnp.concatenate(yb_blocks, axis=0) * b_zs
        ub = _dot(yb.astype(BF16), wub_ref[...])
        mrg_s[...] = _gated(gate_s[1], ub)
        gate(C_G0, 0)
        y = (cu_s[CONV_HALO - 1:CONV_HALO - 1 + SB] * wconv_ref[0:1]
             + cu_s[CONV_HALO:CONV_HALO + SB] * wconv_ref[1:2]
             + cu_s[CONV_HALO + 1:CONV_HALO + 1 + SB] * wconv_ref[2:3])
        ya = a_b * y * a_zs
        ua = _dot(ya.astype(BF16), wua_ref[...])
        mrg_s[...] += _gated(gate_s[0], ua)
        yield
        mq = _dot(h, win_ref[:, C_MQ:C_MQ + MEM_WIDTH]) * ((MEM_HEAD_DIM ** -0.5) * LOG2E)
        m_zs = _silu_half(_dot(h, win_ref[:, C_MZ:C_MZ + MEM_WIDTH]))
        scores = []
        for hh in range(MEM_HEADS):
            hs = slice(hh * MEM_HEAD_DIM, (hh + 1) * MEM_HEAD_DIM)
            scores.append(_dot(mq[:, hs].astype(BF16), mkt_ref[hs, :]))
        gate(C_G2, 2)
        ym_heads = []
        for hh in range(MEM_HEADS):
            s = scores[hh]
            p = jnp.exp2(s - jnp.max(s, axis=1, keepdims=True))
            ov = _dot(p.astype(BF16), mvx_ref[hh])
            ym_heads.append(ov[:, :MEM_HEAD_DIM] / ov[:, MEM_HEAD_DIM:])
        ym = (jnp.concatenate(ym_heads, axis=1) * m_zs).astype(BF16)
        yield
        for half in range(SB // OUT_ROWS):
            rs = slice(half * OUT_ROWS, (half + 1) * OUT_ROWS)
            t0 = sb * SB + half * OUT_ROWS
            um = _dot(ym[rs], wum_ref[...])
            merged = mrg_s[rs] + _gated(gate_s[2, rs], um)
            o = _dot(merged.astype(BF16), wout_ref[...])
            write_out(t0 // OUT_ROWS, x_ref[0, t0:t0 + OUT_ROWS] + _rms(o, g_post), tile_row + t0)
        yield

    gens = [sub_block(sb) for sb in range(NSB)]
    def run(sb, count=1):
        for _ in range(count):
            next(gens[sb])

    def norm_for_chunk(c):
        if c < NSB:
            run(c)
        else:
            norm_next_halo()

    wait_out_slots()
    carry_prev_halo()
    run(0, 2)
    norm_for_chunk(1)
    kv_chunk(0)
    kv_chunk(1)
    for sb in range(NSB):
        run(sb)
        if sb > 0:
            run(sb - 1)
        run(sb, 2)
        if sb + 1 < NSB:
            norm_for_chunk(sb + 2)
        run(sb)
        flush_out()
        run(sb)
        if sb + 1 < NSB:
            run(sb + 1)
            kv_chunk(sb + 2)
    run(NSB - 1)
    flush_out()

    @pl.when(step == last_step)
    def _last_step():
        wait_out_slots()


def _resident(shape):
    return pl.BlockSpec(shape, lambda b, i: (0,) * len(shape), pipeline_mode=pl.Buffered(1))


def _layer(x, mem, g_pre, w_in, w_conv, attn_sink, g_mem, w_mem_kv, w_up_a, w_up_b, w_up_m, w_out, g_post,
           cos_t, sin_t, bias_t):
    bsz, s, d = x.shape
    assert (s, d) == (SEQ, D_MODEL) and s % TS == 0 and SB % QB == 0 and SB == 2 * HALO
    assert (STAGE_ROWS, STAGE_COLS) == (SB, d)
    nb = TS // HALO
    hbm = pl.BlockSpec(memory_space=pl.ANY)
    in_specs = [
        pl.BlockSpec(memory_space=pltpu.SMEM),
        pl.BlockSpec((1, TS, d), lambda b, i: (b, i, 0)),
        pl.BlockSpec((1, HALO, d), lambda b, i: (b, jnp.minimum((i + 1) * nb, s // HALO - 1), 0)),
        _resident((1, d)),
        hbm,
        _resident((CONV_WIDTH, A_WIDTH)),
        _resident((s + 2 * HALO, LANES)),
        _resident((s + 2 * HALO, LANES)),
        _resident((3, QB, 3 * QB)),
        _resident((1, IN_WIDTH)),
        pl.BlockSpec((1, MEM_LEN, d), lambda b, i: (b, 0, 0)),
        _resident((1, d)),
        hbm, hbm, hbm, hbm, hbm,
        _resident((1, d)),
    ]
    scratch = [
        pltpu.VMEM((EXT, d), BF16),
        pltpu.VMEM((KV_WIDTH, EXT), BF16),
        pltpu.VMEM((KV_WIDTH, EXT), BF16),
        pltpu.VMEM((EXT, 2 * KV_WIDTH), BF16),
        pltpu.VMEM((EXT, 2 * KV_WIDTH), BF16),
        pltpu.VMEM((SB + 2 * CONV_HALO, A_WIDTH), F32),
        pltpu.VMEM((SB, d), F32),
        pltpu.VMEM((3, SB, d), F32),
        pltpu.VMEM((d, IN_WIDTH), BF16),
        pltpu.VMEM((d, 2 * MEM_WIDTH), BF16),
        pltpu.VMEM((A_WIDTH, d), BF16),
        pltpu.VMEM((ATTN_WIDTH, d), BF16),
        pltpu.VMEM((MEM_WIDTH, d), BF16),
        pltpu.VMEM((d, d), BF16),
        pltpu.VMEM((MEM_WIDTH, MEM_LEN), BF16),
        pltpu.VMEM((MEM_HEADS, MEM_LEN, 2 * MEM_HEAD_DIM), BF16),
        pltpu.SemaphoreType.DMA((4 + OUT_SLOTS // 2,)),
        pltpu.VMEM((OUT_SLOTS // 2, STAGE_ROWS, d), F32),
        pltpu.SemaphoreType.DMA((1,)),
    ]
    out = pl.pallas_call(
        _layer_kernel,
        grid=(bsz, s // TS),
        in_specs=in_specs,
        out_specs=hbm,
        out_shape=jax.ShapeDtypeStruct((bsz * s, d), x.dtype),
        scratch_shapes=scratch,
        compiler_params=pltpu.CompilerParams(dimension_semantics=("arbitrary", "arbitrary"),
                                             vmem_limit_bytes=VMEM_LIMIT_BYTES),
        name="hybrid_layer",
    )(attn_sink, x, x, g_pre.reshape(1, d), w_in, w_conv, cos_t, sin_t, bias_t, jnp.asarray(_in_scale()), mem,
      g_mem.reshape(1, d),
      w_mem_kv, w_up_a, w_up_b, w_up_m, w_out, g_post.reshape(1, d))
    return out.reshape(bsz, s, d)


def kernel(x, mem, g_pre, w_in, w_conv, attn_sink, g_mem, w_mem_kv, w_up_a, w_up_b, w_up_m, w_out, g_post):
    cos_np, sin_np = _rope_tables()
    cos_t, sin_t, bias_t = jnp.asarray(cos_np), jnp.asarray(sin_np), jnp.asarray(_band_bias())
    for l in range(g_pre.shape[0]):
        x = _layer(x, mem, g_pre[l], w_in[l], w_conv[l], attn_sink[l], g_mem[l], w_mem_kv[l], w_up_a[l], w_up_b[l],
                   w_up_m[l], w_out[l], g_post[l], cos_t, sin_t, bias_t)
    return x
```

```python
import math

import numpy as np
import jax
import jax.numpy as jnp
from jax import lax
from jax.experimental import pallas as pl
from jax.experimental.pallas import tpu as pltpu

F32 = jnp.float32
BF16 = jnp.bfloat16

D_MODEL = 1024
SEQ = 4096
MEM_LEN = 256
EPS = 1e-6
CONV_WIDTH = 3
A_WIDTH = 512
HEAD_DIM = 64
ATTN_WIDTH = 512
N_Q_HEADS = 8
KV_WIDTH = 128
WINDOW = 128
ROPE_THETA = 500000.0
ROT_DIM = 16
MEM_HEADS = 4
MEM_HEAD_DIM = 128
MEM_WIDTH = 512

C_AB, C_AC, C_AX, C_AZ = 0, 512, 1024, 1536
C_BQ, C_BK, C_BV, C_BZ = 2048, 2560, 2688, 2816
C_MQ, C_MZ = 3328, 3840
C_G0, C_G1, C_G2 = 4352, 5376, 6400
IN_WIDTH = 7424

LANES = 128
BF16_ROWS = 16
VMEM_LIMIT_BYTES = 62 * 1024 * 1024

SB = 256
NSB = 4
TS = SB * NSB
QB = WINDOW
HALO = WINDOW
CONV_HALO = BF16_ROWS
EXT = TS + 2 * HALO
STAGE_ROWS = SB
STAGE_COLS = D_MODEL
OUT_ROWS = SB // 2
OUT_SLOTS = TS // OUT_ROWS
VT_ROWS = HEAD_DIM + BF16_ROWS

LOG2E = math.log2(math.e)


def _silu_half(hv):
    return hv * jnp.tanh(hv) + hv


def _gated(t, hu):
    return t * hu + hu


def _in_scale():
    sc = np.ones((1, IN_WIDTH), np.float32)
    for c0, width in ((C_AZ, A_WIDTH), (C_BZ, ATTN_WIDTH), (C_MZ, MEM_WIDTH), (C_G0, 3 * D_MODEL)):
        sc[:, c0:c0 + width] = 0.5
    return sc


def _rms(v, g):
    ms = jnp.mean(v * v, axis=-1, keepdims=True)
    return v * lax.rsqrt(ms + EPS) * g


def _dot(a, b):
    return jnp.dot(a, b, preferred_element_type=F32)


def _rope(t, cos, ssin, low8):
    partner = jnp.where(low8, pltpu.roll(t, LANES - ROT_DIM // 2, 1), pltpu.roll(t, ROT_DIM // 2, 1))
    return t * cos + partner * ssin


def _rope_tables():
    half = ROT_DIM // 2
    inv_freq = np.power(np.float32(ROPE_THETA), -np.arange(half, dtype=np.float32) * np.float32(2.0 / ROT_DIM))
    pos = (np.arange(SEQ + 2 * HALO) - HALO).astype(np.float32)
    ang = (pos[:, None] * inv_freq[None, :]).astype(np.float32)
    cos, sin = np.cos(ang).astype(np.float32), np.sin(ang).astype(np.float32)
    ct = np.ones((SEQ + 2 * HALO, HEAD_DIM), np.float32)
    st = np.zeros((SEQ + 2 * HALO, HEAD_DIM), np.float32)
    ct[:, :half], ct[:, half:ROT_DIM] = cos, cos
    st[:, :half], st[:, half:ROT_DIM] = -sin, sin
    return np.tile(ct, (1, LANES // HEAD_DIM)), np.tile(st, (1, LANES // HEAD_DIM))


def _band_bias():
    r = np.arange(QB)[None, :]
    c = np.arange(3 * QB)[:, None]
    band = (c >= r) & (c <= r + 2 * WINDOW)
    masks = [band, band & (c >= QB), band & (c < 2 * QB)]
    return np.stack([np.where(m, 0.0, -np.inf) for m in masks]).astype(np.float32)


def _stream_cast(jobs, slots, sem):
    chunks = [(src, dst, scale, r0, c0, min(STAGE_COLS, src.shape[1] - c0))
              for src, dst, scale in jobs
              for r0 in range(0, src.shape[0], STAGE_ROWS)
              for c0 in range(0, src.shape[1], STAGE_COLS)]
    n_slots = len(slots)

    def copy(n):
        src, _, _, r0, c0, cols = chunks[n]
        k = n % n_slots
        return pltpu.make_async_copy(src.at[pl.ds(r0, STAGE_ROWS), pl.ds(c0, cols)],
                                     slots[k].at[:, pl.ds(0, cols)], sem.at[k])

    for n in range(min(n_slots, len(chunks))):
        copy(n).start()
    for n, (_, dst, scale, r0, c0, cols) in enumerate(chunks):
        copy(n).wait()
        w = slots[n % n_slots][:, 0:cols]
        w = w * (scale if isinstance(scale, float) else scale[:, c0:c0 + cols])
        dst[r0:r0 + STAGE_ROWS, c0:c0 + cols] = w.astype(BF16)
        if n + n_slots < len(chunks):
            copy(n + n_slots).start()


def _dot_nt(a, b):
    return lax.dot_general(a, b, (((1,), (1,)), ((), ())), preferred_element_type=F32)


def _window_softmax_pv(st, vt_win, sink_row):
    m = jnp.maximum(jnp.max(st, axis=0, keepdims=True), sink_row)
    p = jnp.exp2(st - m)
    ovt = _dot(vt_win, p.astype(BF16))
    return ovt[0:HEAD_DIM] / (ovt[HEAD_DIM:HEAD_DIM + 1] + jnp.exp2(sink_row - m))


def _layer_kernel(sink_ref, x_ref, xn_ref, gpre_ref, win_hbm, wconv_ref, cos_ref, sin_ref, bias_ref, wscale_ref,
                  mem_ref, gmem_ref, wmkv_hbm, wua_hbm, wub_hbm, wum_hbm, wout_hbm, gpost_ref,
                  out_hbm, h_s, kk0_s, kk1_s, vt0_s, vt1_s, cu_s, mrg_s, gate_s,
                  win_ref, wmkv_ref, wua_ref, wub_ref, wum_ref, wout_ref, mkt_ref, mvx_ref,
                  stage_sem, out_s, out_sem):
    b = pl.program_id(0)
    i = pl.program_id(1)
    n_tiles = pl.num_programs(1)
    step = b * n_tiles + i
    last_step = pl.num_programs(0) * n_tiles - 1
    row0 = pl.multiple_of(i * TS, TS)
    tile_row = step * TS
    g_pre = gpre_ref[...]
    g_post = gpost_ref[...]

    def out_slot(slot):
        return out_s.at[slot // 2, pl.ds((slot % 2) * OUT_ROWS, OUT_ROWS)]

    def out_copy(slot, dst_row):
        return pltpu.make_async_copy(out_slot(slot), out_hbm.at[pl.ds(dst_row, OUT_ROWS)], out_sem.at[0])

    pending_out = []

    def write_out(slot, value, dst_row):
        out_slot(slot)[...] = value
        pending_out.append((slot, dst_row))

    def flush_out():
        for slot, dst_row in pending_out:
            out_copy(slot, dst_row).start()
        pending_out.clear()

    def wait_out_slots():
        for slot in range(OUT_SLOTS):
            out_copy(slot, 0).wait()

    @pl.when(step == 0)
    def _first_step():
        slots = [gate_s.at[k] for k in range(3)] + [mrg_s] + [out_s.at[k] for k in range(OUT_SLOTS // 2)]
        _stream_cast([(win_hbm, win_ref, wscale_ref), (wmkv_hbm, wmkv_ref, 1.0), (wua_hbm, wua_ref, 0.5),
                      (wub_hbm, wub_ref, 0.5), (wum_hbm, wum_ref, 0.5), (wout_hbm, wout_ref, 1.0)],
                     slots, stage_sem)
        for buf in (kk0_s, kk1_s):
            buf[TS:TS + HALO] = jnp.zeros((HALO, KV_WIDTH), BF16)
        for buf in (vt0_s, vt1_s):
            buf[:, TS:TS + HALO] = jnp.zeros((VT_ROWS, HALO), BF16)
        cu_s[...] = jnp.zeros(cu_s.shape, F32)
        out_s[...] = jnp.zeros(out_s.shape, F32)
        for slot in range(OUT_SLOTS):
            out_copy(slot, slot * OUT_ROWS).start()

    @pl.when(i == 0)
    def _memory_kv():
        mn = _rms(mem_ref[0], gmem_ref[...]).astype(BF16)
        kv = _dot(mn, wmkv_ref[...])
        mkt_ref[...] = kv[:, :MEM_WIDTH].T.astype(BF16)
        ones_m = jnp.ones((MEM_LEN, MEM_HEAD_DIM), F32)
        for hh in range(MEM_HEADS):
            c0 = MEM_WIDTH + hh * MEM_HEAD_DIM
            mvx_ref[hh] = jnp.concatenate([kv[:, c0:c0 + MEM_HEAD_DIM], ones_m], axis=1).astype(BF16)

    lane = lax.broadcasted_iota(jnp.int32, (1, LANES), 1)
    low_head = lane < HEAD_DIM
    low8 = (lane % HEAD_DIM) < (ROT_DIM // 2)

    lanes4 = lax.broadcasted_iota(jnp.int32, (1, 4 * QB), 1)
    def sink_lanes(h0):
        return LOG2E * jnp.where(lanes4 < QB, sink_ref[h0],
                                 jnp.where(lanes4 < 2 * QB, sink_ref[h0 + 1],
                                           jnp.where(lanes4 < 3 * QB, sink_ref[h0 + 2], sink_ref[h0 + 3])))
    sink_g = [sink_lanes(0), sink_lanes(4)]
    n_blocks = SEQ // QB

    def norm_main(sb):
        r0 = HALO + sb * SB
        h_s[r0:r0 + SB] = _rms(x_ref[0, sb * SB:(sb + 1) * SB], g_pre).astype(BF16)

    def norm_next_halo():
        h_s[HALO + TS:EXT] = _rms(xn_ref[0], g_pre).astype(BF16)

    def carry_prev_halo():
        for buf in (kk0_s, kk1_s):
            buf[0:HALO] = buf[TS:TS + HALO]
        for buf in (vt0_s, vt1_s):
            buf[:, 0:HALO] = buf[:, TS:TS + HALO]

    def kv_chunk(c):
        r0 = HALO if c == 0 else c * SB
        r1 = (c + 1) * SB
        n = r1 - r0
        kv = _dot(h_s[r0:r1], win_ref[:, C_BK:C_BV + KV_WIDTH])
        cos = cos_ref[pl.ds(row0 + r0, n), :]
        ssin = sin_ref[pl.ds(row0 + r0, n), :]
        k = _rope(kv[:, :KV_WIDTH], cos, ssin, low8)
        v = kv[:, KV_WIDTH:]
        k_sw = pltpu.roll(k, HEAD_DIM, 1)
        kk0_s[r0:r1] = jnp.where(low_head, k, k_sw).astype(BF16)
        kk1_s[r0:r1] = jnp.where(low_head, k_sw, k).astype(BF16)
        vt = v.T.astype(BF16)
        ones_rows = jnp.ones((VT_ROWS - HEAD_DIM, n), BF16)
        vt0_s[0:HEAD_DIM, r0:r1] = vt[0:HEAD_DIM]
        vt0_s[HEAD_DIM:VT_ROWS, r0:r1] = ones_rows
        vt1_s[0:HEAD_DIM, r0:r1] = vt[HEAD_DIM:2 * HEAD_DIM]
        vt1_s[HEAD_DIM:VT_ROWS, r0:r1] = ones_rows

    def sub_block(sb):
        m0 = HALO + sb * SB

        def gate(col, slot):
            gate_s[slot] = jnp.tanh(_dot(h, win_ref[:, col:col + D_MODEL]))

        def conv_input():
            have = CONV_HALO if sb > 0 else 0
            before = cu_s[SB:SB + CONV_HALO + have]
            hx = h_s[m0 + have:m0 + SB + CONV_HALO]
            cx = _dot(hx, win_ref[:, C_AC:C_AX + A_WIDTH])
            cu = cx[:, :A_WIDTH] * cx[:, A_WIDTH:]
            first = row0 + sb * SB == 0
            last = row0 + (sb + 1) * SB == SEQ
            cu_s[0:CONV_HALO + have] = before if sb > 0 else jnp.where(first, 0.0, before)
            cu_s[CONV_HALO + have:CONV_HALO + SB] = cu[0:SB - have]
            cu_s[CONV_HALO + SB:] = jnp.where(last, 0.0, cu[SB - have:])

        norm_main(sb)
        yield
        h = h_s[m0:m0 + SB]
        q = _dot(h, win_ref[:, C_BQ:C_BQ + ATTN_WIDTH])
        b_zs = _silu_half(_dot(h, win_ref[:, C_BZ:C_BZ + ATTN_WIDTH]))
        yield
        a_b = _dot(h, win_ref[:, C_AB:C_AB + A_WIDTH])
        a_zs = _silu_half(_dot(h, win_ref[:, C_AZ:C_AZ + A_WIDTH]))
        scale = (HEAD_DIM ** -0.5) * LOG2E
        qcos = cos_ref[pl.ds(row0 + m0, SB), :] * scale
        qsin = sin_ref[pl.ds(row0 + m0, SB), :] * scale
        q_lo, q_hi = [], []
        for mblk in range(ATTN_WIDTH // LANES):
            qr = _rope(q[:, mblk * LANES:(mblk + 1) * LANES], qcos, qsin, low8)
            q_lo.append(jnp.where(low_head, qr, 0.0).astype(BF16))
            q_hi.append(jnp.where(low_head, 0.0, qr).astype(BF16))
        yield
        fillers = [lambda: gate(C_G1, 1), conv_input]
        yb_blocks = []
        for j in range(SB // QB):
            qs = slice(j * QB, (j + 1) * QB)
            jb = sb * (SB // QB) + j
            blk = i * (TS // QB) + jb
            variant = jnp.where(blk == 0, 1, jnp.where(blk == n_blocks - 1, 2, 0))
            bias1 = bias_ref[variant]
            bias = jnp.concatenate([bias1, bias1, bias1, bias1], axis=1)
            q_g = [jnp.concatenate([q_lo[2 * g][qs], q_hi[2 * g][qs], q_lo[2 * g + 1][qs], q_hi[2 * g + 1][qs]],
                                   axis=0) for g in range(2)]
            ws = slice(jb * QB, jb * QB + 3 * QB)
            st = [_dot_nt(kk_s[ws], q_g[g]) + bias for g, kk_s in enumerate((kk0_s, kk1_s))]
            fillers[j % len(fillers)]()
            ot = [_window_softmax_pv(st[g], vt_s[:, ws], sink_g[g]) for g, vt_s in enumerate((vt0_s, vt1_s))]
            cols = []
            for pair in range(ATTN_WIDTH // LANES):
                o_g = ot[pair // 2]
                c0 = (pair % 2) * 2 * QB
                cols.append(jnp.concatenate([o_g[:, c0:c0 + QB], o_g[:, c0 + QB:c0 + 2 * QB]], axis=0).T)
            yb_blocks.append(jnp.concatenate(cols, axis=1))
            yield
        yb = jnp.concatenate(yb_blocks, axis=0) * b_zs
        ub = _dot(yb.astype(BF16), wub_ref[...])
        mrg_s[...] = _gated(gate_s[1], ub)
        gate(C_G0, 0)
        y = (cu_s[CONV_HALO - 1:CONV_HALO - 1 + SB] * wconv_ref[0:1]
             + cu_s[CONV_HALO:CONV_HALO + SB] * wconv_ref[1:2]
             + cu_s[CONV_HALO + 1:CONV_HALO + 1 + SB] * wconv_ref[2:3])
        ya = a_b * y * a_zs
        ua = _dot(ya.astype(BF16), wua_ref[...])
        mrg_s[...] += _gated(gate_s[0], ua)
        yield
        mq = _dot(h, win_ref[:, C_MQ:C_MQ + MEM_WIDTH]) * ((MEM_HEAD_DIM ** -0.5) * LOG2E)
        m_zs = _silu_half(_dot(h, win_ref[:, C_MZ:C_MZ + MEM_WIDTH]))
        scores = []
        for hh in range(MEM_HEADS):
            hs = slice(hh * MEM_HEAD_DIM, (hh + 1) * MEM_HEAD_DIM)
            scores.append(_dot(mq[:, hs].astype(BF16), mkt_ref[hs, :]))
        gate(C_G2, 2)
        ym_heads = []
        for hh in range(MEM_HEADS):
            s = scores[hh]
            p = jnp.exp2(s - jnp.max(s, axis=1, keepdims=True))
            ov = _dot(p.astype(BF16), mvx_ref[hh])
            ym_heads.append(ov[:, :MEM_HEAD_DIM] / ov[:, MEM_HEAD_DIM:])
        ym = (jnp.concatenate(ym_heads, axis=1) * m_zs).astype(BF16)
        yield
        for half in range(SB // OUT_ROWS):
            rs = slice(half * OUT_ROWS, (half + 1) * OUT_ROWS)
            t0 = sb * SB + half * OUT_ROWS
            um = _dot(ym[rs], wum_ref[...])
            merged = mrg_s[rs] + _gated(gate_s[2, rs], um)
            o = _dot(merged.astype(BF16), wout_ref[...])
            write_out(t0 // OUT_ROWS, x_ref[0, t0:t0 + OUT_ROWS] + _rms(o, g_post), tile_row + t0)
        yield

    gens = [sub_block(sb) for sb in range(NSB)]
    def run(sb, count=1):
        for _ in range(count):
            next(gens[sb])

    def norm_for_chunk(c):
        if c < NSB:
            run(c)
        else:
            norm_next_halo()

    wait_out_slots()
    carry_prev_halo()
    run(0, 2)
    norm_for_chunk(1)
    kv_chunk(0)
    kv_chunk(1)
    for sb in range(NSB):
        run(sb)
        if sb > 0:
            run(sb - 1)
        run(sb, 2)
        if sb + 1 < NSB:
            norm_for_chunk(sb + 2)
        run(sb)
        flush_out()
        run(sb)
        if sb + 1 < NSB:
            run(sb + 1)
            kv_chunk(sb + 2)
    run(NSB - 1)
    flush_out()

    @pl.when(step == last_step)
    def _last_step():
        wait_out_slots()


def _resident(shape):
    return pl.BlockSpec(shape, lambda b, i: (0,) * len(shape), pipeline_mode=pl.Buffered(1))


def _layer(x, mem, g_pre, w_in, w_conv, attn_sink, g_mem, w_mem_kv, w_up_a, w_up_b, w_up_m, w_out, g_post,
           cos_t, sin_t, bias_t):
    bsz, s, d = x.shape
    assert (s, d) == (SEQ, D_MODEL) and s % TS == 0 and SB % QB == 0 and SB == 2 * HALO
    assert (STAGE_ROWS, STAGE_COLS) == (SB, d)
    nb = TS // HALO
    hbm = pl.BlockSpec(memory_space=pl.ANY)
    in_specs = [
        pl.BlockSpec(memory_space=pltpu.SMEM),
        pl.BlockSpec((1, TS, d), lambda b, i: (b, i, 0)),
        pl.BlockSpec((1, HALO, d), lambda b, i: (b, jnp.minimum((i + 1) * nb, s // HALO - 1), 0)),
        _resident((1, d)),
        hbm,
        _resident((CONV_WIDTH, A_WIDTH)),
        _resident((s + 2 * HALO, LANES)),
        _resident((s + 2 * HALO, LANES)),
        _resident((3, 3 * QB, QB)),
        _resident((1, IN_WIDTH)),
        pl.BlockSpec((1, MEM_LEN, d), lambda b, i: (b, 0, 0)),
        _resident((1, d)),
        hbm, hbm, hbm, hbm, hbm,
        _resident((1, d)),
    ]
    scratch = [
        pltpu.VMEM((EXT, d), BF16),
        pltpu.VMEM((EXT, KV_WIDTH), BF16),
        pltpu.VMEM((EXT, KV_WIDTH), BF16),
        pltpu.VMEM((VT_ROWS, EXT), BF16),
        pltpu.VMEM((VT_ROWS, EXT), BF16),
        pltpu.VMEM((SB + 2 * CONV_HALO, A_WIDTH), F32),
        pltpu.VMEM((SB, d), F32),
        pltpu.VMEM((3, SB, d), F32),
        pltpu.VMEM((d, IN_WIDTH), BF16),
        pltpu.VMEM((d, 2 * MEM_WIDTH), BF16),
        pltpu.VMEM((A_WIDTH, d), BF16),
        pltpu.VMEM((ATTN_WIDTH, d), BF16),
        pltpu.VMEM((MEM_WIDTH, d), BF16),
        pltpu.VMEM((d, d), BF16),
        pltpu.VMEM((MEM_WIDTH, MEM_LEN), BF16),
        pltpu.VMEM((MEM_HEADS, MEM_LEN, 2 * MEM_HEAD_DIM), BF16),
        pltpu.SemaphoreType.DMA((4 + OUT_SLOTS // 2,)),
        pltpu.VMEM((OUT_SLOTS // 2, STAGE_ROWS, d), F32),
        pltpu.SemaphoreType.DMA((1,)),
    ]
    out = pl.pallas_call(
        _layer_kernel,
        grid=(bsz, s // TS),
        in_specs=in_specs,
        out_specs=hbm,
        out_shape=jax.ShapeDtypeStruct((bsz * s, d), x.dtype),
        scratch_shapes=scratch,
        compiler_params=pltpu.CompilerParams(dimension_semantics=("arbitrary", "arbitrary"),
                                             vmem_limit_bytes=VMEM_LIMIT_BYTES),
        name="hybrid_layer",
    )(attn_sink, x, x, g_pre.reshape(1, d), w_in, w_conv, cos_t, sin_t, bias_t, jnp.asarray(_in_scale()), mem,
      g_mem.reshape(1, d),
      w_mem_kv, w_up_a, w_up_b, w_up_m, w_out, g_post.reshape(1, d))
    return out.reshape(bsz, s, d)


def kernel(x, mem, g_pre, w_in, w_conv, attn_sink, g_mem, w_mem_kv, w_up_a, w_up_b, w_up_m, w_out, g_post):
    cos_np, sin_np = _rope_tables()
    cos_t, sin_t, bias_t = jnp.asarray(cos_np), jnp.asarray(sin_np), jnp.asarray(_band_bias())
    for l in range(g_pre.shape[0]):
        x = _layer(x, mem, g_pre[l], w_in[l], w_conv[l], attn_sink[l], g_mem[l], w_mem_kv[l], w_up_a[l], w_up_b[l],
                   w_up_m[l], w_out[l], g_post[l], cos_t, sin_t, bias_t)
    return x
```

```python
import math

import numpy as np
import jax
import jax.numpy as jnp
from jax import lax
from jax.experimental import pallas as pl
from jax.experimental.pallas import tpu as pltpu

F32 = jnp.float32
BF16 = jnp.bfloat16

D_MODEL = 1024
SEQ = 4096
MEM_LEN = 256
EPS = 1e-6
CONV_WIDTH = 3
A_WIDTH = 512
HEAD_DIM = 64
ATTN_WIDTH = 512
KV_WIDTH = 128
WINDOW = 128
ROPE_THETA = 500000.0
ROT_DIM = 16
MEM_HEADS = 4
MEM_HEAD_DIM = 128
MEM_WIDTH = 512

C_AB, C_AC, C_AX, C_AZ = 0, 512, 1024, 1536
C_BQ, C_BK, C_BV, C_BZ = 2048, 2560, 2688, 2816
C_MQ, C_MZ = 3328, 3840
C_G0, C_G1, C_G2 = 4352, 5376, 6400
IN_WIDTH = 7424

LANES = 128
BF16_ROWS = 16
V7X_VMEM_BYTES = 64 * 1024 * 1024
VMEM_LIMIT_BYTES = V7X_VMEM_BYTES - 2 * 1024 * 1024

SB = 256
NSB = 4
TS = SB * NSB
QB = WINDOW
HALO = WINDOW
CONV_HALO = BF16_ROWS
EXT = TS + 2 * HALO
STAGE_ROWS = SB
STAGE_COLS = D_MODEL
OUT_ROWS = SB // 2
OUT_SLOTS = TS // OUT_ROWS

LOG2E = math.log2(math.e)


def _silu_half(hv):
    return hv * jnp.tanh(hv) + hv


def _gated(t, hu):
    return t * hu + hu


def _in_scale():
    sc = np.ones((1, IN_WIDTH), np.float32)
    for c0, width in ((C_AZ, A_WIDTH), (C_BZ, ATTN_WIDTH), (C_MZ, MEM_WIDTH), (C_G0, 3 * D_MODEL)):
        sc[:, c0:c0 + width] = 0.5
    return sc


def _rms(v, g):
    ms = jnp.mean(v * v, axis=-1, keepdims=True)
    return v * lax.rsqrt(ms + EPS) * g


def _dot(a, b):
    return jnp.dot(a, b, preferred_element_type=F32)


def _rope(t, cos, ssin, low8):
    partner = jnp.where(low8, pltpu.roll(t, LANES - ROT_DIM // 2, 1), pltpu.roll(t, ROT_DIM // 2, 1))
    return t * cos + partner * ssin


def _rope_tables():
    half = ROT_DIM // 2
    inv_freq = np.power(np.float32(ROPE_THETA), -np.arange(half, dtype=np.float32) * np.float32(2.0 / ROT_DIM))
    pos = (np.arange(SEQ + 2 * HALO) - HALO).astype(np.float32)
    ang = (pos[:, None] * inv_freq[None, :]).astype(np.float32)
    cos, sin = np.cos(ang).astype(np.float32), np.sin(ang).astype(np.float32)
    ct = np.ones((SEQ + 2 * HALO, HEAD_DIM), np.float32)
    st = np.zeros((SEQ + 2 * HALO, HEAD_DIM), np.float32)
    ct[:, :half], ct[:, half:ROT_DIM] = cos, cos
    st[:, :half], st[:, half:ROT_DIM] = -sin, sin
    return np.tile(ct, (1, LANES // HEAD_DIM)), np.tile(st, (1, LANES // HEAD_DIM))


def _band_bias():
    r = np.arange(QB)[:, None]
    c = np.arange(3 * QB)[None, :]
    band = (c >= r) & (c <= r + 2 * WINDOW)
    masks = [band, band & (c >= QB), band & (c < 2 * QB)]
    return np.stack([np.where(m, 0.0, -np.inf) for m in masks]).astype(np.float32)


def _stream_cast(jobs, slots, sem):
    chunks = [(src, dst, scale, r0, c0, min(STAGE_COLS, src.shape[1] - c0))
              for src, dst, scale in jobs
              for r0 in range(0, src.shape[0], STAGE_ROWS)
              for c0 in range(0, src.shape[1], STAGE_COLS)]
    n_slots = len(slots)

    def copy(n):
        src, _, _, r0, c0, cols = chunks[n]
        k = n % n_slots
        return pltpu.make_async_copy(src.at[pl.ds(r0, STAGE_ROWS), pl.ds(c0, cols)],
                                     slots[k].at[:, pl.ds(0, cols)], sem.at[k])

    for n in range(min(n_slots, len(chunks))):
        copy(n).start()
    for n, (_, dst, scale, r0, c0, cols) in enumerate(chunks):
        copy(n).wait()
        w = slots[n % n_slots][:, 0:cols]
        w = w * (scale if isinstance(scale, float) else scale[:, c0:c0 + cols])
        dst[r0:r0 + STAGE_ROWS, c0:c0 + cols] = w.astype(BF16)
        if n + n_slots < len(chunks):
            copy(n + n_slots).start()


def _window_softmax_pv(s, v_win, sink_col):
    m = jnp.maximum(jnp.max(s, axis=1, keepdims=True), sink_col)
    p = jnp.exp2(s - m)
    ov = _dot(p.astype(BF16), v_win)
    return ov[:, :LANES] / (ov[:, LANES:] + jnp.exp2(sink_col - m))


def _layer_kernel(sink_ref, x_ref, xn_ref, gpre_ref, win_hbm, wconv_ref, cos_ref, sin_ref, bias_ref, wscale_ref,
                  mem_ref, gmem_ref, wmkv_hbm, wua_hbm, wub_hbm, wum_hbm, wout_hbm, gpost_ref,
                  out_hbm, h_s, k_s, ksw_s, v_s, vsw_s, cu_s, mrg_s, gate_s,
                  win_ref, wmkv_ref, wua_ref, wub_ref, wum_ref, wout_ref, mkt_ref, mvx_ref,
                  stage_sem, out_s, out_sem):
    b = pl.program_id(0)
    i = pl.program_id(1)
    n_tiles = pl.num_programs(1)
    step = b * n_tiles + i
    last_step = pl.num_programs(0) * n_tiles - 1
    row0 = pl.multiple_of(i * TS, TS)
    tile_row = step * TS
    g_pre = gpre_ref[...]
    g_post = gpost_ref[...]

    def out_slot(slot):
        return out_s.at[slot // 2, pl.ds((slot % 2) * OUT_ROWS, OUT_ROWS)]

    def out_copy(slot, dst_row):
        return pltpu.make_async_copy(out_slot(slot), out_hbm.at[pl.ds(dst_row, OUT_ROWS)], out_sem.at[0])

    pending_out = []

    def write_out(slot, value, dst_row):
        out_slot(slot)[...] = value
        pending_out.append((slot, dst_row))

    def flush_out():
        for slot, dst_row in pending_out:
            out_copy(slot, dst_row).start()
        pending_out.clear()

    def wait_out_slots():
        for slot in range(OUT_SLOTS):
            out_copy(slot, 0).wait()

    @pl.when(step == 0)
    def _first_step():
        slots = [gate_s.at[k] for k in range(3)] + [mrg_s] + [out_s.at[k] for k in range(OUT_SLOTS // 2)]
        _stream_cast([(win_hbm, win_ref, wscale_ref), (wmkv_hbm, wmkv_ref, 1.0), (wua_hbm, wua_ref, 0.5),
                      (wub_hbm, wub_ref, 0.5), (wum_hbm, wum_ref, 0.5), (wout_hbm, wout_ref, 1.0)],
                     slots, stage_sem)
        for buf in (k_s, ksw_s):
            buf[:, TS:TS + HALO] = jnp.zeros((KV_WIDTH, HALO), BF16)
        for buf in (v_s, vsw_s):
            buf[TS:TS + HALO] = jnp.zeros((HALO, buf.shape[1]), BF16)
        cu_s[...] = jnp.zeros(cu_s.shape, F32)
        out_s[...] = jnp.zeros(out_s.shape, F32)
        for slot in range(OUT_SLOTS):
            out_copy(slot, slot * OUT_ROWS).start()

    @pl.when(i == 0)
    def _memory_kv():
        mn = _rms(mem_ref[0], gmem_ref[...]).astype(BF16)
        kv = _dot(mn, wmkv_ref[...])
        mkt_ref[...] = kv[:, :MEM_WIDTH].T.astype(BF16)
        ones_m = jnp.ones((MEM_LEN, MEM_HEAD_DIM), F32)
        for hh in range(MEM_HEADS):
            c0 = MEM_WIDTH + hh * MEM_HEAD_DIM
            mvx_ref[hh] = jnp.concatenate([kv[:, c0:c0 + MEM_HEAD_DIM], ones_m], axis=1).astype(BF16)

    lane = lax.broadcasted_iota(jnp.int32, (1, LANES), 1)
    low_head = lane < HEAD_DIM
    low8 = (lane % HEAD_DIM) < (ROT_DIM // 2)

    rows4 = lax.broadcasted_iota(jnp.int32, (4 * QB, 1), 0)
    def sink_rows(h0, h1, h2, h3):
        return LOG2E * jnp.where(rows4 < QB, sink_ref[h0],
                                 jnp.where(rows4 < 2 * QB, sink_ref[h1],
                                           jnp.where(rows4 < 3 * QB, sink_ref[h2], sink_ref[h3])))
    sink_a = sink_rows(0, 2, 5, 7)
    sink_b = sink_rows(1, 3, 4, 6)
    n_blocks = SEQ // QB
    ones = jnp.ones((SB, LANES), BF16)

    def norm_main(sb):
        r0 = HALO + sb * SB
        h_s[r0:r0 + SB] = _rms(x_ref[0, sb * SB:(sb + 1) * SB], g_pre).astype(BF16)

    def norm_next_halo():
        h_s[HALO + TS:EXT] = _rms(xn_ref[0], g_pre).astype(BF16)

    def carry_prev_halo():
        for buf in (k_s, ksw_s):
            buf[:, 0:HALO] = buf[:, TS:TS + HALO]
        for buf in (v_s, vsw_s):
            buf[0:HALO] = buf[TS:TS + HALO]

    def kv_chunk(c):
        r0 = HALO if c == 0 else c * SB
        r1 = (c + 1) * SB
        n = r1 - r0
        kv = _dot(h_s[r0:r1], win_ref[:, C_BK:C_BV + KV_WIDTH])
        cos = cos_ref[pl.ds(row0 + r0, n), :]
        ssin = sin_ref[pl.ds(row0 + r0, n), :]
        k = _rope(kv[:, :KV_WIDTH], cos, ssin, low8)
        v = kv[:, KV_WIDTH:]
        k_s[:, r0:r1] = k.T.astype(BF16)
        ksw_s[:, r0:r1] = pltpu.roll(k, HEAD_DIM, 1).T.astype(BF16)
        v_s[r0:r1, 0:LANES] = v.astype(BF16)
        v_s[r0:r1, LANES:2 * LANES] = ones[0:n]
        vsw_s[r0:r1, 0:LANES] = pltpu.roll(v, HEAD_DIM, 1).astype(BF16)
        vsw_s[r0:r1, LANES:2 * LANES] = ones[0:n]

    def sub_block(sb):
        m0 = HALO + sb * SB

        def gate(col, slot):
            gate_s[slot] = jnp.tanh(_dot(h, win_ref[:, col:col + D_MODEL]))

        def conv_input():
            have = CONV_HALO if sb > 0 else 0
            before = cu_s[SB:SB + CONV_HALO + have]
            hx = h_s[m0 + have:m0 + SB + CONV_HALO]
            cx = _dot(hx, win_ref[:, C_AC:C_AX + A_WIDTH])
            cu = cx[:, :A_WIDTH] * cx[:, A_WIDTH:]
            first = row0 + sb * SB == 0
            last = row0 + (sb + 1) * SB == SEQ
            cu_s[0:CONV_HALO + have] = before if sb > 0 else jnp.where(first, 0.0, before)
            cu_s[CONV_HALO + have:CONV_HALO + SB] = cu[0:SB - have]
            cu_s[CONV_HALO + SB:] = jnp.where(last, 0.0, cu[SB - have:])

        norm_main(sb)
        yield
        h = h_s[m0:m0 + SB]
        q = _dot(h, win_ref[:, C_BQ:C_BQ + ATTN_WIDTH])
        b_zs = _silu_half(_dot(h, win_ref[:, C_BZ:C_BZ + ATTN_WIDTH]))
        yield
        a_b = _dot(h, win_ref[:, C_AB:C_AB + A_WIDTH])
        a_zs = _silu_half(_dot(h, win_ref[:, C_AZ:C_AZ + A_WIDTH]))
        scale = (HEAD_DIM ** -0.5) * LOG2E
        qcos = cos_ref[pl.ds(row0 + m0, SB), :] * scale
        qsin = sin_ref[pl.ds(row0 + m0, SB), :] * scale
        q_lo, q_hi = [], []
        for mblk in range(ATTN_WIDTH // LANES):
            qr = _rope(q[:, mblk * LANES:(mblk + 1) * LANES], qcos, qsin, low8)
            q_lo.append(jnp.where(low_head, qr, 0.0).astype(BF16))
            q_hi.append(jnp.where(low_head, 0.0, qr).astype(BF16))
        yield
        fillers = [lambda: gate(C_G1, 1), conv_input]
        yb_blocks = []
        for j in range(SB // QB):
            qs = slice(j * QB, (j + 1) * QB)
            jb = sb * (SB // QB) + j
            blk = i * (TS // QB) + jb
            variant = jnp.where(blk == 0, 1, jnp.where(blk == n_blocks - 1, 2, 0))
            bias1 = bias_ref[variant]
            bias = jnp.concatenate([bias1, bias1, bias1, bias1], axis=0)
            lhs_a = jnp.concatenate([q_lo[0][qs], q_lo[1][qs], q_hi[2][qs], q_hi[3][qs]], axis=0)
            lhs_b = jnp.concatenate([q_hi[0][qs], q_hi[1][qs], q_lo[2][qs], q_lo[3][qs]], axis=0)
            ws = slice(jb * QB, jb * QB + 3 * QB)
            s_a = _dot(lhs_a, k_s[:, ws]) + bias
            s_b = _dot(lhs_b, ksw_s[:, ws]) + bias
            fillers[j % len(fillers)]()
            o_a = _window_softmax_pv(s_a, v_s[ws], sink_a)
            o_b = _window_softmax_pv(s_b, vsw_s[ws], sink_b)
            cols = [jnp.where(low_head, o_a[0:QB], o_b[0:QB]),
                    jnp.where(low_head, o_a[QB:2 * QB], o_b[QB:2 * QB]),
                    jnp.where(low_head, o_b[2 * QB:3 * QB], o_a[2 * QB:3 * QB]),
                    jnp.where(low_head, o_b[3 * QB:4 * QB], o_a[3 * QB:4 * QB])]
            yb_blocks.append(jnp.concatenate(cols, axis=1))
            yield
        yb = jnp.concatenate(yb_blocks, axis=0) * b_zs
        ub = _dot(yb.astype(BF16), wub_ref[...])
        mrg_s[...] = _gated(gate_s[1], ub)
        gate(C_G0, 0)
        y = (cu_s[CONV_HALO - 1:CONV_HALO - 1 + SB] * wconv_ref[0:1]
             + cu_s[CONV_HALO:CONV_HALO + SB] * wconv_ref[1:2]
             + cu_s[CONV_HALO + 1:CONV_HALO + 1 + SB] * wconv_ref[2:3])
        ya = a_b * y * a_zs
        ua = _dot(ya.astype(BF16), wua_ref[...])
        mrg_s[...] += _gated(gate_s[0], ua)
        yield
        mq = _dot(h, win_ref[:, C_MQ:C_MQ + MEM_WIDTH]) * ((MEM_HEAD_DIM ** -0.5) * LOG2E)
        m_zs = _silu_half(_dot(h, win_ref[:, C_MZ:C_MZ + MEM_WIDTH]))
        scores = []
        for hh in range(MEM_HEADS):
            hs = slice(hh * MEM_HEAD_DIM, (hh + 1) * MEM_HEAD_DIM)
            scores.append(_dot(mq[:, hs].astype(BF16), mkt_ref[hs, :]))
        gate(C_G2, 2)
        ym_heads = []
        for hh in range(MEM_HEADS):
            s = scores[hh]
            p = jnp.exp2(s - jnp.max(s, axis=1, keepdims=True))
            ov = _dot(p.astype(BF16), mvx_ref[hh])
            ym_heads.append(ov[:, :MEM_HEAD_DIM] / ov[:, MEM_HEAD_DIM:])
        ym = (jnp.concatenate(ym_heads, axis=1) * m_zs).astype(BF16)
        yield
        for half in range(SB // OUT_ROWS):
            rs = slice(half * OUT_ROWS, (half + 1) * OUT_ROWS)
            t0 = sb * SB + half * OUT_ROWS
            um = _dot(ym[rs], wum_ref[...])
            merged = mrg_s[rs] + _gated(gate_s[2, rs], um)
            o = _dot(merged.astype(BF16), wout_ref[...])
            write_out(t0 // OUT_ROWS, x_ref[0, t0:t0 + OUT_ROWS] + _rms(o, g_post), tile_row + t0)
        yield

    gens = [sub_block(sb) for sb in range(NSB)]
    def run(sb, count=1):
        for _ in range(count):
            next(gens[sb])

    def norm_for_chunk(c):
        if c < NSB:
            run(c)
        else:
            norm_next_halo()

    wait_out_slots()
    carry_prev_halo()
    run(0, 2)
    norm_for_chunk(1)
    kv_chunk(0)
    kv_chunk(1)
    for sb in range(NSB):
        run(sb)
        if sb > 0:
            run(sb - 1)
        run(sb, 2)
        if sb + 1 < NSB:
            norm_for_chunk(sb + 2)
        run(sb)
        flush_out()
        run(sb)
        if sb + 1 < NSB:
            run(sb + 1)
            kv_chunk(sb + 2)
    run(NSB - 1)
    flush_out()

    @pl.when(step == last_step)
    def _last_step():
        wait_out_slots()


def _resident(shape):
    return pl.BlockSpec(shape, lambda b, i: (0,) * len(shape), pipeline_mode=pl.Buffered(1))


def _layer(x, mem, g_pre, w_in, w_conv, attn_sink, g_mem, w_mem_kv, w_up_a, w_up_b, w_up_m, w_out, g_post,
           cos_t, sin_t, bias_t):
    bsz, s, d = x.shape
    assert (s, d) == (SEQ, D_MODEL) and s % TS == 0 and SB % QB == 0 and SB == 2 * HALO
    assert (STAGE_ROWS, STAGE_COLS) == (SB, d)
    nb = TS // HALO
    hbm = pl.BlockSpec(memory_space=pl.ANY)
    in_specs = [
        pl.BlockSpec(memory_space=pltpu.SMEM),
        pl.BlockSpec((1, TS, d), lambda b, i: (b, i, 0)),
        pl.BlockSpec((1, HALO, d), lambda b, i: (b, jnp.minimum((i + 1) * nb, s // HALO - 1), 0)),
        _resident((1, d)),
        hbm,
        _resident((CONV_WIDTH, A_WIDTH)),
        _resident((s + 2 * HALO, LANES)),
        _resident((s + 2 * HALO, LANES)),
        _resident((3, QB, 3 * QB)),
        _resident((1, IN_WIDTH)),
        pl.BlockSpec((1, MEM_LEN, d), lambda b, i: (b, 0, 0)),
        _resident((1, d)),
        hbm, hbm, hbm, hbm, hbm,
        _resident((1, d)),
    ]
    scratch = [
        pltpu.VMEM((EXT, d), BF16),
        pltpu.VMEM((KV_WIDTH, EXT), BF16),
        pltpu.VMEM((KV_WIDTH, EXT), BF16),
        pltpu.VMEM((EXT, 2 * KV_WIDTH), BF16),
        pltpu.VMEM((EXT, 2 * KV_WIDTH), BF16),
        pltpu.VMEM((SB + 2 * CONV_HALO, A_WIDTH), F32),
        pltpu.VMEM((SB, d), F32),
        pltpu.VMEM((3, SB, d), F32),
        pltpu.VMEM((d, IN_WIDTH), BF16),
        pltpu.VMEM((d, 2 * MEM_WIDTH), BF16),
        pltpu.VMEM((A_WIDTH, d), BF16),
        pltpu.VMEM((ATTN_WIDTH, d), BF16),
        pltpu.VMEM((MEM_WIDTH, d), BF16),
        pltpu.VMEM((d, d), BF16),
        pltpu.VMEM((MEM_WIDTH, MEM_LEN), BF16),
        pltpu.VMEM((MEM_HEADS, MEM_LEN, 2 * MEM_HEAD_DIM), BF16),
        pltpu.SemaphoreType.DMA((4 + OUT_SLOTS // 2,)),
        pltpu.VMEM((OUT_SLOTS // 2, STAGE_ROWS, d), F32),
        pltpu.SemaphoreType.DMA((1,)),
    ]
    out = pl.pallas_call(
        _layer_kernel,
        grid=(bsz, s // TS),
        in_specs=in_specs,
        out_specs=hbm,
        out_shape=jax.ShapeDtypeStruct((bsz * s, d), x.dtype),
        scratch_shapes=scratch,
        compiler_params=pltpu.CompilerParams(dimension_semantics=("arbitrary", "arbitrary"),
                                             vmem_limit_bytes=VMEM_LIMIT_BYTES),
        name="hybrid_layer",
    )(attn_sink, x, x, g_pre.reshape(1, d), w_in, w_conv, cos_t, sin_t, bias_t, jnp.asarray(_in_scale()), mem,
      g_mem.reshape(1, d), w_mem_kv, w_up_a, w_up_b, w_up_m, w_out, g_post.reshape(1, d))
    return out.reshape(bsz, s, d)


def kernel(x, mem, g_pre, w_in, w_conv, attn_sink, g_mem, w_mem_kv, w_up_a, w_up_b, w_up_m, w_out, g_post):
    cos_np, sin_np = _rope_tables()
    cos_t, sin_t, bias_t = jnp.asarray(cos_np), jnp.asarray(sin_np), jnp.asarray(_band_bias())
    for l in range(g_pre.shape[0]):
        x = _layer(x, mem, g_pre[l], w_in[l], w_conv[l], attn_sink[l], g_mem[l], w_mem_kv[l], w_up_a[l], w_up_b[l],
                   w_up_m[l], w_out[l], g_post[l], cos_t, sin_t, bias_t)
    return x
```

```python
import math

import numpy as np
import jax
import jax.numpy as jnp
from jax import lax
from jax.experimental import pallas as pl
from jax.experimental.pallas import tpu as pltpu

F32 = jnp.float32
BF16 = jnp.bfloat16

D_MODEL = 1024
SEQ = 4096
MEM_LEN = 256
EPS = 1e-6
CONV_WIDTH = 3
A_WIDTH = 512
HEAD_DIM = 64
ATTN_WIDTH = 512
KV_WIDTH = 128
WINDOW = 128
ROPE_THETA = 500000.0
ROT_DIM = 16
MEM_HEADS = 4
MEM_HEAD_DIM = 128
MEM_WIDTH = 512

C_AB, C_AC, C_AX, C_AZ = 0, 512, 1024, 1536
C_BQ, C_BK, C_BV, C_BZ = 2048, 2560, 2688, 2816
C_MQ, C_MZ = 3328, 3840
C_G0, C_G1, C_G2 = 4352, 5376, 6400
IN_WIDTH = 7424

LANES = 128
BF16_ROWS = 16
V7X_VMEM_BYTES = 64 * 1024 * 1024
VMEM_LIMIT_BYTES = V7X_VMEM_BYTES - 2 * 1024 * 1024

SB = 256
NSB = 4
TS = SB * NSB
QB = WINDOW
HALO = WINDOW
CONV_HALO = BF16_ROWS
EXT = TS + 2 * HALO
STAGE_ROWS = SB
STAGE_COLS = D_MODEL
OUT_ROWS = SB
OUT_SLOTS = TS // OUT_ROWS
OUT_PER_BUF = STAGE_ROWS // OUT_ROWS

LOG2E = math.log2(math.e)


def _silu_half(hv):
    return hv * jnp.tanh(hv) + hv


def _gated(t, hu):
    return t * hu + hu


def _in_scale():
    sc = np.ones((1, IN_WIDTH), np.float32)
    for c0, width in ((C_AZ, A_WIDTH), (C_BZ, ATTN_WIDTH), (C_MZ, MEM_WIDTH), (C_G0, 3 * D_MODEL)):
        sc[:, c0:c0 + width] = 0.5
    return sc


def _rms(v, g):
    ms = jnp.mean(v * v, axis=-1, keepdims=True)
    return v * lax.rsqrt(ms + EPS) * g


def _dot(a, b):
    return jnp.dot(a, b, preferred_element_type=F32)


def _rope(t, cos, ssin, low8):
    partner = jnp.where(low8, pltpu.roll(t, LANES - ROT_DIM // 2, 1), pltpu.roll(t, ROT_DIM // 2, 1))
    return t * cos + partner * ssin


def _rope_tables():
    half = ROT_DIM // 2
    inv_freq = np.power(np.float32(ROPE_THETA), -np.arange(half, dtype=np.float32) * np.float32(2.0 / ROT_DIM))
    pos = (np.arange(SEQ + 2 * HALO) - HALO).astype(np.float32)
    ang = (pos[:, None] * inv_freq[None, :]).astype(np.float32)
    cos, sin = np.cos(ang).astype(np.float32), np.sin(ang).astype(np.float32)
    ct = np.ones((SEQ + 2 * HALO, HEAD_DIM), np.float32)
    st = np.zeros((SEQ + 2 * HALO, HEAD_DIM), np.float32)
    ct[:, :half], ct[:, half:ROT_DIM] = cos, cos
    st[:, :half], st[:, half:ROT_DIM] = -sin, sin
    return np.tile(ct, (1, LANES // HEAD_DIM)), np.tile(st, (1, LANES // HEAD_DIM))


def _band_bias():
    r = np.arange(QB)[:, None]
    c = np.arange(3 * QB)[None, :]
    band = (c >= r) & (c <= r + 2 * WINDOW)
    masks = [band, band & (c >= QB), band & (c < 2 * QB)]
    return np.stack([np.where(m, 0.0, -np.inf) for m in masks]).astype(np.float32)


def _stream_cast(jobs, slots, sem):
    chunks = [(src, dst, scale, r0, c0, min(STAGE_COLS, src.shape[1] - c0))
              for src, dst, scale in jobs
              for r0 in range(0, src.shape[0], STAGE_ROWS)
              for c0 in range(0, src.shape[1], STAGE_COLS)]
    n_slots = len(slots)

    def copy(n):
        src, _, _, r0, c0, cols = chunks[n]
        k = n % n_slots
        return pltpu.make_async_copy(src.at[pl.ds(r0, STAGE_ROWS), pl.ds(c0, cols)],
                                     slots[k].at[:, pl.ds(0, cols)], sem.at[k])

    for n in range(min(n_slots, len(chunks))):
        copy(n).start()
    for n, (_, dst, scale, r0, c0, cols) in enumerate(chunks):
        copy(n).wait()
        w = slots[n % n_slots][:, 0:cols]
        w = w * (scale if isinstance(scale, float) else scale[:, c0:c0 + cols])
        dst[r0:r0 + STAGE_ROWS, c0:c0 + cols] = w.astype(BF16)
        if n + n_slots < len(chunks):
            copy(n + n_slots).start()


def _window_softmax_pv(s, v_win, sink_col):
    m = jnp.maximum(jnp.max(s, axis=1, keepdims=True), sink_col)
    p = jnp.exp2(s - m)
    ov = _dot(p.astype(BF16), v_win)
    return ov[:, :LANES] / (ov[:, LANES:] + jnp.exp2(sink_col - m))


def _layer_kernel(sink_ref, x_ref, xn_ref, gpre_ref, win_hbm, wconv_ref, cos_ref, sin_ref, bias_ref, wscale_ref,
                  mem_ref, gmem_ref, wmkv_hbm, wua_hbm, wub_hbm, wum_hbm, wout_hbm, gpost_ref,
                  out_hbm, h_s, k_s, ksw_s, v_s, vsw_s, cu_s, mrg_s, gate_s,
                  win_ref, wmkv_ref, wua_ref, wub_ref, wum_ref, wout_ref, mkt_ref, mvx_ref,
                  stage_sem, out_s, out_sem):
    b = pl.program_id(0)
    i = pl.program_id(1)
    n_tiles = pl.num_programs(1)
    step = b * n_tiles + i
    last_step = pl.num_programs(0) * n_tiles - 1
    row0 = pl.multiple_of(i * TS, TS)
    tile_row = step * TS
    g_pre = gpre_ref[...]
    g_post = gpost_ref[...]

    def out_slot(slot):
        return out_s.at[slot // OUT_PER_BUF, pl.ds((slot % OUT_PER_BUF) * OUT_ROWS, OUT_ROWS)]

    def out_copy(slot, dst_row):
        return pltpu.make_async_copy(out_slot(slot), out_hbm.at[pl.ds(dst_row, OUT_ROWS)], out_sem.at[0])

    pending_out = []

    def write_out(slot, value, dst_row):
        out_slot(slot)[...] = value
        pending_out.append((slot, dst_row))

    def flush_out():
        for slot, dst_row in pending_out:
            out_copy(slot, dst_row).start()
        pending_out.clear()

    def wait_out_slots():
        for slot in range(OUT_SLOTS):
            out_copy(slot, 0).wait()

    @pl.when(step == 0)
    def _first_step():
        slots = [gate_s.at[k] for k in range(3)] + [mrg_s] + [out_s.at[k] for k in range(OUT_SLOTS // OUT_PER_BUF)]
        _stream_cast([(win_hbm, win_ref, wscale_ref), (wmkv_hbm, wmkv_ref, 1.0), (wua_hbm, wua_ref, 0.5),
                      (wub_hbm, wub_ref, 0.5), (wum_hbm, wum_ref, 0.5), (wout_hbm, wout_ref, 1.0)],
                     slots, stage_sem)
        for buf in (k_s, ksw_s):
            buf[:, TS:TS + HALO] = jnp.zeros((KV_WIDTH, HALO), BF16)
        for buf in (v_s, vsw_s):
            buf[TS:TS + HALO] = jnp.zeros((HALO, buf.shape[1]), BF16)
        cu_s[...] = jnp.zeros(cu_s.shape, F32)
        out_s[...] = jnp.zeros(out_s.shape, F32)
        for slot in range(OUT_SLOTS):
            out_copy(slot, slot * OUT_ROWS).start()

    @pl.when(i == 0)
    def _memory_kv():
        mn = _rms(mem_ref[0], gmem_ref[...]).astype(BF16)
        kv = _dot(mn, wmkv_ref[...])
        mkt_ref[...] = kv[:, :MEM_WIDTH].T.astype(BF16)
        ones_m = jnp.ones((MEM_LEN, MEM_HEAD_DIM), F32)
        for hh in range(MEM_HEADS):
            c0 = MEM_WIDTH + hh * MEM_HEAD_DIM
            mvx_ref[hh] = jnp.concatenate([kv[:, c0:c0 + MEM_HEAD_DIM], ones_m], axis=1).astype(BF16)

    lane = lax.broadcasted_iota(jnp.int32, (1, LANES), 1)
    low_head = lane < HEAD_DIM
    low8 = (lane % HEAD_DIM) < (ROT_DIM // 2)

    rows4 = lax.broadcasted_iota(jnp.int32, (4 * QB, 1), 0)
    def sink_rows(h0, h1, h2, h3):
        return LOG2E * jnp.where(rows4 < QB, sink_ref[h0],
                                 jnp.where(rows4 < 2 * QB, sink_ref[h1],
                                           jnp.where(rows4 < 3 * QB, sink_ref[h2], sink_ref[h3])))
    sink_a = sink_rows(0, 2, 5, 7)
    sink_b = sink_rows(1, 3, 4, 6)
    n_blocks = SEQ // QB
    ones = jnp.ones((SB, LANES), BF16)

    def norm_main(sb):
        r0 = HALO + sb * SB
        h_s[r0:r0 + SB] = _rms(x_ref[0, sb * SB:(sb + 1) * SB], g_pre).astype(BF16)

    def norm_next_halo():
        h_s[HALO + TS:EXT] = _rms(xn_ref[0], g_pre).astype(BF16)

    def carry_prev_halo():
        for buf in (k_s, ksw_s):
            buf[:, 0:HALO] = buf[:, TS:TS + HALO]
        for buf in (v_s, vsw_s):
            buf[0:HALO] = buf[TS:TS + HALO]

    def kv_chunk(c):
        r0 = HALO if c == 0 else c * SB
        r1 = (c + 1) * SB
        n = r1 - r0
        kv = _dot(h_s[r0:r1], win_ref[:, C_BK:C_BV + KV_WIDTH])
        cos = cos_ref[pl.ds(row0 + r0, n), :]
        ssin = sin_ref[pl.ds(row0 + r0, n), :]
        k = _rope(kv[:, :KV_WIDTH], cos, ssin, low8)
        v = kv[:, KV_WIDTH:]
        k_s[:, r0:r1] = k.T.astype(BF16)
        ksw_s[:, r0:r1] = pltpu.roll(k, HEAD_DIM, 1).T.astype(BF16)
        v_s[r0:r1, 0:LANES] = v.astype(BF16)
        v_s[r0:r1, LANES:2 * LANES] = ones[0:n]
        vsw_s[r0:r1, 0:LANES] = pltpu.roll(v, HEAD_DIM, 1).astype(BF16)
        vsw_s[r0:r1, LANES:2 * LANES] = ones[0:n]

    def sub_block(sb):
        m0 = HALO + sb * SB

        def gate(col, slot):
            gate_s[slot] = jnp.tanh(_dot(h, win_ref[:, col:col + D_MODEL]))

        def conv_input():
            have = CONV_HALO if sb > 0 else 0
            before = cu_s[SB:SB + CONV_HALO + have]
            hx = h_s[m0 + have:m0 + SB + CONV_HALO]
            cx = _dot(hx, win_ref[:, C_AC:C_AX + A_WIDTH])
            cu = cx[:, :A_WIDTH] * cx[:, A_WIDTH:]
            first = row0 + sb * SB == 0
            last = row0 + (sb + 1) * SB == SEQ
            cu_s[0:CONV_HALO + have] = before if sb > 0 else jnp.where(first, 0.0, before)
            cu_s[CONV_HALO + have:CONV_HALO + SB] = cu[0:SB - have]
            cu_s[CONV_HALO + SB:] = jnp.where(last, 0.0, cu[SB - have:])

        norm_main(sb)
        yield
        h = h_s[m0:m0 + SB]
        q = _dot(h, win_ref[:, C_BQ:C_BQ + ATTN_WIDTH])
        b_zs = _silu_half(_dot(h, win_ref[:, C_BZ:C_BZ + ATTN_WIDTH]))
        yield
        a_b = _dot(h, win_ref[:, C_AB:C_AB + A_WIDTH])
        a_zs = _silu_half(_dot(h, win_ref[:, C_AZ:C_AZ + A_WIDTH]))
        scale = (HEAD_DIM ** -0.5) * LOG2E
        qcos = cos_ref[pl.ds(row0 + m0, SB), :] * scale
        qsin = sin_ref[pl.ds(row0 + m0, SB), :] * scale
        q_lo, q_hi = [], []
        for mblk in range(ATTN_WIDTH // LANES):
            qr = _rope(q[:, mblk * LANES:(mblk + 1) * LANES], qcos, qsin, low8)
            q_lo.append(jnp.where(low_head, qr, 0.0).astype(BF16))
            q_hi.append(jnp.where(low_head, 0.0, qr).astype(BF16))
        yield
        fillers = [lambda: gate(C_G1, 1), conv_input]
        yb_blocks = []
        for j in range(SB // QB):
            qs = slice(j * QB, (j + 1) * QB)
            jb = sb * (SB // QB) + j
            blk = i * (TS // QB) + jb
            variant = jnp.where(blk == 0, 1, jnp.where(blk == n_blocks - 1, 2, 0))
            bias1 = bias_ref[variant]
            bias = jnp.concatenate([bias1, bias1, bias1, bias1], axis=0)
            lhs_a = jnp.concatenate([q_lo[0][qs], q_lo[1][qs], q_hi[2][qs], q_hi[3][qs]], axis=0)
            lhs_b = jnp.concatenate([q_hi[0][qs], q_hi[1][qs], q_lo[2][qs], q_lo[3][qs]], axis=0)
            ws = slice(jb * QB, jb * QB + 3 * QB)
            s_a = _dot(lhs_a, k_s[:, ws]) + bias
            s_b = _dot(lhs_b, ksw_s[:, ws]) + bias
            fillers[j % len(fillers)]()
            o_a = _window_softmax_pv(s_a, v_s[ws], sink_a)
            o_b = _window_softmax_pv(s_b, vsw_s[ws], sink_b)
            cols = [jnp.where(low_head, o_a[0:QB], o_b[0:QB]),
                    jnp.where(low_head, o_a[QB:2 * QB], o_b[QB:2 * QB]),
                    jnp.where(low_head, o_b[2 * QB:3 * QB], o_a[2 * QB:3 * QB]),
                    jnp.where(low_head, o_b[3 * QB:4 * QB], o_a[3 * QB:4 * QB])]
            yb_blocks.append(jnp.concatenate(cols, axis=1))
            yield
        yb = jnp.concatenate(yb_blocks, axis=0) * b_zs
        ub = _dot(yb.astype(BF16), wub_ref[...])
        mrg_s[...] = _gated(gate_s[1], ub)
        gate(C_G0, 0)
        y = (cu_s[CONV_HALO - 1:CONV_HALO - 1 + SB] * wconv_ref[0:1]
             + cu_s[CONV_HALO:CONV_HALO + SB] * wconv_ref[1:2]
             + cu_s[CONV_HALO + 1:CONV_HALO + 1 + SB] * wconv_ref[2:3])
        ya = a_b * y * a_zs
        ua = _dot(ya.astype(BF16), wua_ref[...])
        mrg_s[...] += _gated(gate_s[0], ua)
        yield
        mq = _dot(h, win_ref[:, C_MQ:C_MQ + MEM_WIDTH]) * ((MEM_HEAD_DIM ** -0.5) * LOG2E)
        m_zs = _silu_half(_dot(h, win_ref[:, C_MZ:C_MZ + MEM_WIDTH]))
        scores = []
        for hh in range(MEM_HEADS):
            hs = slice(hh * MEM_HEAD_DIM, (hh + 1) * MEM_HEAD_DIM)
            scores.append(_dot(mq[:, hs].astype(BF16), mkt_ref[hs, :]))
        gate(C_G2, 2)
        ym_heads = []
        for hh in range(MEM_HEADS):
            s = scores[hh]
            p = jnp.exp2(s - jnp.max(s, axis=1, keepdims=True))
            ov = _dot(p.astype(BF16), mvx_ref[hh])
            ym_heads.append(ov[:, :MEM_HEAD_DIM] / ov[:, MEM_HEAD_DIM:])
        ym = (jnp.concatenate(ym_heads, axis=1) * m_zs).astype(BF16)
        yield
        for half in range(SB // OUT_ROWS):
            rs = slice(half * OUT_ROWS, (half + 1) * OUT_ROWS)
            t0 = sb * SB + half * OUT_ROWS
            um = _dot(ym[rs], wum_ref[...])
            merged = mrg_s[rs] + _gated(gate_s[2, rs], um)
            o = _dot(merged.astype(BF16), wout_ref[...])
            write_out(t0 // OUT_ROWS, x_ref[0, t0:t0 + OUT_ROWS] + _rms(o, g_post), tile_row + t0)
        yield

    gens = [sub_block(sb) for sb in range(NSB)]
    def run(sb, count=1):
        for _ in range(count):
            next(gens[sb])

    def norm_for_chunk(c):
        if c < NSB:
            run(c)
        else:
            norm_next_halo()

    wait_out_slots()
    carry_prev_halo()
    run(0, 2)
    norm_for_chunk(1)
    kv_chunk(0)
    kv_chunk(1)
    for sb in range(NSB):
        run(sb)
        if sb > 0:
            run(sb - 1)
        run(sb, 2)
        if sb + 1 < NSB:
            norm_for_chunk(sb + 2)
        run(sb)
        flush_out()
        run(sb)
        if sb + 1 < NSB:
            run(sb + 1)
            kv_chunk(sb + 2)
    run(NSB - 1)
    flush_out()

    @pl.when(step == last_step)
    def _last_step():
        wait_out_slots()


def _resident(shape):
    return pl.BlockSpec(shape, lambda b, i: (0,) * len(shape), pipeline_mode=pl.Buffered(1))


def _layer(x, mem, g_pre, w_in, w_conv, attn_sink, g_mem, w_mem_kv, w_up_a, w_up_b, w_up_m, w_out, g_post,
           cos_t, sin_t, bias_t):
    bsz, s, d = x.shape
    assert (s, d) == (SEQ, D_MODEL) and s % TS == 0 and SB % QB == 0 and SB == 2 * HALO
    assert (STAGE_ROWS, STAGE_COLS) == (SB, d)
    nb = TS // HALO
    hbm = pl.BlockSpec(memory_space=pl.ANY)
    in_specs = [
        pl.BlockSpec(memory_space=pltpu.SMEM),
        pl.BlockSpec((1, TS, d), lambda b, i: (b, i, 0)),
        pl.BlockSpec((1, HALO, d), lambda b, i: (b, jnp.minimum((i + 1) * nb, s // HALO - 1), 0)),
        _resident((1, d)),
        hbm,
        _resident((CONV_WIDTH, A_WIDTH)),
        _resident((s + 2 * HALO, LANES)),
        _resident((s + 2 * HALO, LANES)),
        _resident((3, QB, 3 * QB)),
        _resident((1, IN_WIDTH)),
        pl.BlockSpec((1, MEM_LEN, d), lambda b, i: (b, 0, 0)),
        _resident((1, d)),
        hbm, hbm, hbm, hbm, hbm,
        _resident((1, d)),
    ]
    scratch = [
        pltpu.VMEM((EXT, d), BF16),
        pltpu.VMEM((KV_WIDTH, EXT), BF16),
        pltpu.VMEM((KV_WIDTH, EXT), BF16),
        pltpu.VMEM((EXT, 2 * KV_WIDTH), BF16),
        pltpu.VMEM((EXT, 2 * KV_WIDTH), BF16),
        pltpu.VMEM((SB + 2 * CONV_HALO, A_WIDTH), F32),
        pltpu.VMEM((SB, d), F32),
        pltpu.VMEM((3, SB, d), F32),
        pltpu.VMEM((d, IN_WIDTH), BF16),
        pltpu.VMEM((d, 2 * MEM_WIDTH), BF16),
        pltpu.VMEM((A_WIDTH, d), BF16),
        pltpu.VMEM((ATTN_WIDTH, d), BF16),
        pltpu.VMEM((MEM_WIDTH, d), BF16),
        pltpu.VMEM((d, d), BF16),
        pltpu.VMEM((MEM_WIDTH, MEM_LEN), BF16),
        pltpu.VMEM((MEM_HEADS, MEM_LEN, 2 * MEM_HEAD_DIM), BF16),
        pltpu.SemaphoreType.DMA((4 + OUT_SLOTS // OUT_PER_BUF,)),
        pltpu.VMEM((OUT_SLOTS // OUT_PER_BUF, STAGE_ROWS, d), F32),
        pltpu.SemaphoreType.DMA((1,)),
    ]
    out = pl.pallas_call(
        _layer_kernel,
        grid=(bsz, s // TS),
        in_specs=in_specs,
        out_specs=hbm,
        out_shape=jax.ShapeDtypeStruct((bsz * s, d), x.dtype),
        scratch_shapes=scratch,
        compiler_params=pltpu.CompilerParams(dimension_semantics=("arbitrary", "arbitrary"),
                                             vmem_limit_bytes=VMEM_LIMIT_BYTES),
        name="hybrid_layer",
    )(attn_sink, x, x, g_pre.reshape(1, d), w_in, w_conv, cos_t, sin_t, bias_t, jnp.asarray(_in_scale()), mem,
      g_mem.reshape(1, d), w_mem_kv, w_up_a, w_up_b, w_up_m, w_out, g_post.reshape(1, d))
    return out.reshape(bsz, s, d)


def kernel(x, mem, g_pre, w_in, w_conv, attn_sink, g_mem, w_mem_kv, w_up_a, w_up_b, w_up_m, w_out, g_post):
    cos_np, sin_np = _rope_tables()
    cos_t, sin_t, bias_t = jnp.asarray(cos_np), jnp.asarray(sin_np), jnp.asarray(_band_bias())
    for l in range(g_pre.shape[0]):
        x = _layer(x, mem, g_pre[l], w_in[l], w_conv[l], attn_sink[l], g_mem[l], w_mem_kv[l], w_up_a[l], w_up_b[l],
                   w_up_m[l], w_out[l], g_post[l], cos_t, sin_t, bias_t)
    return x
```

```python
import math

import numpy as np
import jax
import jax.numpy as jnp
from jax import lax
from jax.experimental import pallas as pl
from jax.experimental.pallas import tpu as pltpu

F32 = jnp.float32
BF16 = jnp.bfloat16

D_MODEL = 1024
SEQ = 4096
MEM_LEN = 256
EPS = 1e-6
CONV_WIDTH = 3
A_WIDTH = 512
HEAD_DIM = 64
ATTN_WIDTH = 512
KV_WIDTH = 128
WINDOW = 128
ROPE_THETA = 500000.0
ROT_DIM = 16
MEM_HEADS = 4
MEM_HEAD_DIM = 128
MEM_WIDTH = 512

C_AB, C_AC, C_AX, C_AZ = 0, 512, 1024, 1536
C_BQ, C_BK, C_BV, C_BZ = 2048, 2560, 2688, 2816
C_MQ, C_MZ = 3328, 3840
C_G0, C_G1, C_G2 = 4352, 5376, 6400
IN_WIDTH = 7424

LANES = 128
BF16_ROWS = 16
V7X_VMEM_BYTES = 64 * 1024 * 1024
VMEM_LIMIT_BYTES = V7X_VMEM_BYTES - 2 * 1024 * 1024

SB = 256
NSB = 4
TS = SB * NSB
QB = WINDOW
HALO = WINDOW
CONV_HALO = BF16_ROWS
EXT = TS + 2 * HALO
STAGE_ROWS = SB
STAGE_COLS = D_MODEL
OUT_ROWS = SB
OUT_SLOTS = TS // OUT_ROWS
OUT_PER_BUF = STAGE_ROWS // OUT_ROWS

LOG2E = math.log2(math.e)


def _silu_half(hv):
    return hv * jnp.tanh(hv) + hv


def _gated(t, hu):
    return t * hu + hu


def _in_scale():
    sc = np.ones((1, IN_WIDTH), np.float32)
    for c0, width in ((C_AZ, A_WIDTH), (C_BZ, ATTN_WIDTH), (C_MZ, MEM_WIDTH), (C_G0, 3 * D_MODEL)):
        sc[:, c0:c0 + width] = 0.5
    return sc


def _rms(v, g):
    ms = jnp.mean(v * v, axis=-1, keepdims=True)
    return v * lax.rsqrt(ms + EPS) * g


def _dot(a, b):
    return jnp.dot(a, b, preferred_element_type=F32)


def _rope(t, cos, ssin, low8):
    partner = jnp.where(low8, pltpu.roll(t, LANES - ROT_DIM // 2, 1), pltpu.roll(t, ROT_DIM // 2, 1))
    return t * cos + partner * ssin


def _rope_tables():
    half = ROT_DIM // 2
    inv_freq = np.power(np.float32(ROPE_THETA), -np.arange(half, dtype=np.float32) * np.float32(2.0 / ROT_DIM))
    pos = (np.arange(SEQ + 2 * HALO) - HALO).astype(np.float32)
    ang = (pos[:, None] * inv_freq[None, :]).astype(np.float32)
    cos, sin = np.cos(ang).astype(np.float32), np.sin(ang).astype(np.float32)
    ct = np.ones((SEQ + 2 * HALO, HEAD_DIM), np.float32)
    st = np.zeros((SEQ + 2 * HALO, HEAD_DIM), np.float32)
    ct[:, :half], ct[:, half:ROT_DIM] = cos, cos
    st[:, :half], st[:, half:ROT_DIM] = -sin, sin
    return np.tile(ct, (1, LANES // HEAD_DIM)), np.tile(st, (1, LANES // HEAD_DIM))


def _band_bias():
    r = np.arange(QB)[:, None]
    c = np.arange(3 * QB)[None, :]
    band = (c >= r) & (c <= r + 2 * WINDOW)
    masks = [band, band & (c >= QB), band & (c < 2 * QB)]
    return np.stack([np.where(m, 0.0, -np.inf) for m in masks]).astype(np.float32)


def _stream_cast(jobs, slots, sem):
    chunks = [(src, dst, scale, r0, c0, min(STAGE_COLS, src.shape[1] - c0))
              for src, dst, scale in jobs
              for r0 in range(0, src.shape[0], STAGE_ROWS)
              for c0 in range(0, src.shape[1], STAGE_COLS)]
    n_slots = len(slots)

    def copy(n):
        src, _, _, r0, c0, cols = chunks[n]
        k = n % n_slots
        return pltpu.make_async_copy(src.at[pl.ds(r0, STAGE_ROWS), pl.ds(c0, cols)],
                                     slots[k].at[:, pl.ds(0, cols)], sem.at[k])

    for n in range(min(n_slots, len(chunks))):
        copy(n).start()
    for n, (_, dst, scale, r0, c0, cols) in enumerate(chunks):
        copy(n).wait()
        w = slots[n % n_slots][:, 0:cols]
        w = w * (scale if isinstance(scale, float) else scale[:, c0:c0 + cols])
        dst[r0:r0 + STAGE_ROWS, c0:c0 + cols] = w.astype(BF16)
        if n + n_slots < len(chunks):
            copy(n + n_slots).start()


def _window_softmax_pv(s, v_win, sink_col):
    m = jnp.maximum(jnp.max(s, axis=1, keepdims=True), sink_col)
    p = jnp.exp2(s - m)
    ov = _dot(p.astype(BF16), v_win)
    return ov[:, :LANES] / (ov[:, LANES:] + jnp.exp2(sink_col - m))


def _layer_kernel(sink_ref, x_ref, xn_ref, gpre_ref, win_hbm, wconv_ref, cos_ref, sin_ref, bias_ref, wscale_ref,
                  mem_ref, gmem_ref, wmkv_hbm, wua_hbm, wub_hbm, wum_hbm, wout_hbm, gpost_ref,
                  out_hbm, h_s, k_s, ksw_s, v_s, vsw_s, cu_s, mrg_s, gate_s,
                  win_ref, wmkv_ref, wua_ref, wub_ref, wum_ref, wout_ref, mkt_ref, mvx_ref,
                  stage_sem, out_s, out_sem):
    b = pl.program_id(0)
    i = pl.program_id(1)
    n_tiles = pl.num_programs(1)
    step = b * n_tiles + i
    last_step = pl.num_programs(0) * n_tiles - 1
    row0 = pl.multiple_of(i * TS, TS)
    tile_row = step * TS
    g_pre = gpre_ref[...]
    g_post = gpost_ref[...]

    def out_slot(slot):
        return out_s.at[slot // OUT_PER_BUF, pl.ds((slot % OUT_PER_BUF) * OUT_ROWS, OUT_ROWS)]

    def out_copy(slot, dst_row):
        return pltpu.make_async_copy(out_slot(slot), out_hbm.at[pl.ds(dst_row, OUT_ROWS)], out_sem.at[0])

    pending_out = []

    def write_out(slot, value, dst_row):
        out_slot(slot)[...] = value
        pending_out.append((slot, dst_row))

    def flush_out():
        for slot, dst_row in pending_out:
            out_copy(slot, dst_row).start()
        pending_out.clear()

    def wait_out_slots():
        for slot in range(OUT_SLOTS):
            out_copy(slot, 0).wait()

    @pl.when(step == 0)
    def _first_step():
        slots = ([gate_s.at[k, pl.ds(half * SB, SB)] for k in range(3) for half in range(2)] + [mrg_s]
                 + [out_s.at[k] for k in range(OUT_SLOTS // OUT_PER_BUF)])
        _stream_cast([(win_hbm, win_ref, wscale_ref), (wmkv_hbm, wmkv_ref, 1.0), (wua_hbm, wua_ref, 0.5),
                      (wub_hbm, wub_ref, 0.5), (wum_hbm, wum_ref, 0.5), (wout_hbm, wout_ref, 1.0)],
                     slots, stage_sem)
        for buf in (k_s, ksw_s):
            buf[:, TS:TS + HALO] = jnp.zeros((KV_WIDTH, HALO), BF16)
        for buf in (v_s, vsw_s):
            buf[TS:TS + HALO] = jnp.zeros((HALO, buf.shape[1]), BF16)
        cu_s[...] = jnp.zeros(cu_s.shape, F32)
        out_s[...] = jnp.zeros(out_s.shape, F32)
        for slot in range(OUT_SLOTS):
            out_copy(slot, slot * OUT_ROWS).start()

    @pl.when(i == 0)
    def _memory_kv():
        mn = _rms(mem_ref[0], gmem_ref[...]).astype(BF16)
        kv = _dot(mn, wmkv_ref[...])
        mkt_ref[...] = kv[:, :MEM_WIDTH].T.astype(BF16)
        ones_m = jnp.ones((MEM_LEN, MEM_HEAD_DIM), F32)
        for hh in range(MEM_HEADS):
            c0 = MEM_WIDTH + hh * MEM_HEAD_DIM
            mvx_ref[hh] = jnp.concatenate([kv[:, c0:c0 + MEM_HEAD_DIM], ones_m], axis=1).astype(BF16)

    lane = lax.broadcasted_iota(jnp.int32, (1, LANES), 1)
    low_head = lane < HEAD_DIM
    low8 = (lane % HEAD_DIM) < (ROT_DIM // 2)

    rows4 = lax.broadcasted_iota(jnp.int32, (4 * QB, 1), 0)
    def sink_rows(h0, h1, h2, h3):
        return LOG2E * jnp.where(rows4 < QB, sink_ref[h0],
                                 jnp.where(rows4 < 2 * QB, sink_ref[h1],
                                           jnp.where(rows4 < 3 * QB, sink_ref[h2], sink_ref[h3])))
    sink_a = sink_rows(0, 2, 5, 7)
    sink_b = sink_rows(1, 3, 4, 6)
    n_blocks = SEQ // QB
    ones = jnp.ones((SB, LANES), BF16)

    def norm_main(sb):
        r0 = HALO + sb * SB
        h_s[r0:r0 + SB] = _rms(x_ref[0, sb * SB:(sb + 1) * SB], g_pre).astype(BF16)

    def norm_next_halo():
        h_s[HALO + TS:EXT] = _rms(xn_ref[0], g_pre).astype(BF16)

    def carry_prev_halo():
        for buf in (k_s, ksw_s):
            buf[:, 0:HALO] = buf[:, TS:TS + HALO]
        for buf in (v_s, vsw_s):
            buf[0:HALO] = buf[TS:TS + HALO]

    def kv_chunk(c):
        r0 = HALO if c == 0 else c * SB
        r1 = (c + 1) * SB
        n = r1 - r0
        kv = _dot(h_s[r0:r1], win_ref[:, C_BK:C_BV + KV_WIDTH])
        cos = cos_ref[pl.ds(row0 + r0, n), :]
        ssin = sin_ref[pl.ds(row0 + r0, n), :]
        k = _rope(kv[:, :KV_WIDTH], cos, ssin, low8)
        v = kv[:, KV_WIDTH:]
        k_s[:, r0:r1] = k.T.astype(BF16)
        ksw_s[:, r0:r1] = pltpu.roll(k, HEAD_DIM, 1).T.astype(BF16)
        v_s[r0:r1, 0:LANES] = v.astype(BF16)
        v_s[r0:r1, LANES:2 * LANES] = ones[0:n]
        vsw_s[r0:r1, 0:LANES] = pltpu.roll(v, HEAD_DIM, 1).astype(BF16)
        vsw_s[r0:r1, LANES:2 * LANES] = ones[0:n]

    def sub_block(sb):
        m0 = HALO + sb * SB

        pair_rows = slice((sb % 2) * SB, (sb % 2 + 1) * SB)
        late = {}

        def gate(col, slot):
            if sb % 2 == 0:
                gate_s[slot] = jnp.tanh(_dot(h_s[m0:m0 + 2 * SB], win_ref[:, col:col + D_MODEL]))

        def branch_a_projections():
            late["a_b"] = _dot(h, win_ref[:, C_AB:C_AB + A_WIDTH])
            late["a_zs"] = _silu_half(_dot(h, win_ref[:, C_AZ:C_AZ + A_WIDTH]))

        def conv_input():
            have = CONV_HALO if sb > 0 else 0
            before = cu_s[SB:SB + CONV_HALO + have]
            hx = h_s[m0 + have:m0 + SB + CONV_HALO]
            cx = _dot(hx, win_ref[:, C_AC:C_AX + A_WIDTH])
            cu = cx[:, :A_WIDTH] * cx[:, A_WIDTH:]
            first = row0 + sb * SB == 0
            last = row0 + (sb + 1) * SB == SEQ
            cu_s[0:CONV_HALO + have] = before if sb > 0 else jnp.where(first, 0.0, before)
            cu_s[CONV_HALO + have:CONV_HALO + SB] = cu[0:SB - have]
            cu_s[CONV_HALO + SB:] = jnp.where(last, 0.0, cu[SB - have:])

        norm_main(sb)
        yield
        h = h_s[m0:m0 + SB]
        q = _dot(h, win_ref[:, C_BQ:C_BQ + ATTN_WIDTH])
        b_zs = _silu_half(_dot(h, win_ref[:, C_BZ:C_BZ + ATTN_WIDTH]))
        yield
        if sb % 2 == 0:
            branch_a_projections()
        scale = (HEAD_DIM ** -0.5) * LOG2E
        qcos = cos_ref[pl.ds(row0 + m0, SB), :] * scale
        qsin = sin_ref[pl.ds(row0 + m0, SB), :] * scale
        q_lo, q_hi = [], []
        for mblk in range(ATTN_WIDTH // LANES):
            qr = _rope(q[:, mblk * LANES:(mblk + 1) * LANES], qcos, qsin, low8)
            q_lo.append(jnp.where(low_head, qr, 0.0).astype(BF16))
            q_hi.append(jnp.where(low_head, 0.0, qr).astype(BF16))
        yield
        fillers = [(lambda: gate(C_G1, 1)) if sb % 2 == 0 else branch_a_projections, conv_input]
        yb_blocks = []
        for j in range(SB // QB):
            qs = slice(j * QB, (j + 1) * QB)
            jb = sb * (SB // QB) + j
            blk = i * (TS // QB) + jb
            variant = jnp.where(blk == 0, 1, jnp.where(blk == n_blocks - 1, 2, 0))
            bias1 = bias_ref[variant]
            bias = jnp.concatenate([bias1, bias1, bias1, bias1], axis=0)
            lhs_a = jnp.concatenate([q_lo[0][qs], q_lo[1][qs], q_hi[2][qs], q_hi[3][qs]], axis=0)
            lhs_b = jnp.concatenate([q_hi[0][qs], q_hi[1][qs], q_lo[2][qs], q_lo[3][qs]], axis=0)
            ws = slice(jb * QB, jb * QB + 3 * QB)
            s_a = _dot(lhs_a, k_s[:, ws]) + bias
            s_b = _dot(lhs_b, ksw_s[:, ws]) + bias
            fillers[j % len(fillers)]()
            o_a = _window_softmax_pv(s_a, v_s[ws], sink_a)
            o_b = _window_softmax_pv(s_b, vsw_s[ws], sink_b)
            cols = [jnp.where(low_head, o_a[0:QB], o_b[0:QB]),
                    jnp.where(low_head, o_a[QB:2 * QB], o_b[QB:2 * QB]),
                    jnp.where(low_head, o_b[2 * QB:3 * QB], o_a[2 * QB:3 * QB]),
                    jnp.where(low_head, o_b[3 * QB:4 * QB], o_a[3 * QB:4 * QB])]
            yb_blocks.append(jnp.concatenate(cols, axis=1))
            yield
        yb = jnp.concatenate(yb_blocks, axis=0) * b_zs
        ub = _dot(yb.astype(BF16), wub_ref[...])
        mrg_s[...] = _gated(gate_s[1, pair_rows], ub)
        gate(C_G0, 0)
        y = (cu_s[CONV_HALO - 1:CONV_HALO - 1 + SB] * wconv_ref[0:1]
             + cu_s[CONV_HALO:CONV_HALO + SB] * wconv_ref[1:2]
             + cu_s[CONV_HALO + 1:CONV_HALO + 1 + SB] * wconv_ref[2:3])
        ya = late["a_b"] * y * late["a_zs"]
        ua = _dot(ya.astype(BF16), wua_ref[...])
        mrg_s[...] += _gated(gate_s[0, pair_rows], ua)
        yield
        mq = _dot(h, win_ref[:, C_MQ:C_MQ + MEM_WIDTH]) * ((MEM_HEAD_DIM ** -0.5) * LOG2E)
        m_zs = _silu_half(_dot(h, win_ref[:, C_MZ:C_MZ + MEM_WIDTH]))
        scores = []
        for hh in range(MEM_HEADS):
            hs = slice(hh * MEM_HEAD_DIM, (hh + 1) * MEM_HEAD_DIM)
            scores.append(_dot(mq[:, hs].astype(BF16), mkt_ref[hs, :]))
        gate(C_G2, 2)
        ym_heads = []
        for hh in range(MEM_HEADS):
            s = scores[hh]
            p = jnp.exp2(s - jnp.max(s, axis=1, keepdims=True))
            ov = _dot(p.astype(BF16), mvx_ref[hh])
            ym_heads.append(ov[:, :MEM_HEAD_DIM] / ov[:, MEM_HEAD_DIM:])
        ym = (jnp.concatenate(ym_heads, axis=1) * m_zs).astype(BF16)
        yield
        for half in range(SB // OUT_ROWS):
            rs = slice(half * OUT_ROWS, (half + 1) * OUT_ROWS)
            t0 = sb * SB + half * OUT_ROWS
            um = _dot(ym[rs], wum_ref[...])
            g2 = gate_s[2, pair_rows.start + rs.start:pair_rows.start + rs.stop]
            merged = mrg_s[rs] + _gated(g2, um)
            o = _dot(merged.astype(BF16), wout_ref[...])
            write_out(t0 // OUT_ROWS, x_ref[0, t0:t0 + OUT_ROWS] + _rms(o, g_post), tile_row + t0)
        yield

    gens = [sub_block(sb) for sb in range(NSB)]
    def run(sb, count=1):
        for _ in range(count):
            next(gens[sb])

    def norm_for_chunk(c):
        if c < NSB:
            run(c)
        else:
            norm_next_halo()

    wait_out_slots()
    carry_prev_halo()
    run(0, 2)
    norm_for_chunk(1)
    kv_chunk(0)
    kv_chunk(1)
    for sb in range(NSB):
        run(sb)
        if sb > 0:
            run(sb - 1)
        run(sb, 2)
        if sb + 1 < NSB:
            norm_for_chunk(sb + 2)
        run(sb)
        flush_out()
        run(sb)
        if sb + 1 < NSB:
            run(sb + 1)
            kv_chunk(sb + 2)
    run(NSB - 1)
    flush_out()

    @pl.when(step == last_step)
    def _last_step():
        wait_out_slots()


def _resident(shape):
    return pl.BlockSpec(shape, lambda b, i: (0,) * len(shape), pipeline_mode=pl.Buffered(1))


def _layer(x, mem, g_pre, w_in, w_conv, attn_sink, g_mem, w_mem_kv, w_up_a, w_up_b, w_up_m, w_out, g_post,
           cos_t, sin_t, bias_t):
    bsz, s, d = x.shape
    assert (s, d) == (SEQ, D_MODEL) and s % TS == 0 and SB % QB == 0 and SB == 2 * HALO
    assert (STAGE_ROWS, STAGE_COLS) == (SB, d)
    nb = TS // HALO
    hbm = pl.BlockSpec(memory_space=pl.ANY)
    in_specs = [
        pl.BlockSpec(memory_space=pltpu.SMEM),
        pl.BlockSpec((1, TS, d), lambda b, i: (b, i, 0)),
        pl.BlockSpec((1, HALO, d), lambda b, i: (b, jnp.minimum((i + 1) * nb, s // HALO - 1), 0)),
        _resident((1, d)),
        hbm,
        _resident((CONV_WIDTH, A_WIDTH)),
        _resident((s + 2 * HALO, LANES)),
        _resident((s + 2 * HALO, LANES)),
        _resident((3, QB, 3 * QB)),
        _resident((1, IN_WIDTH)),
        pl.BlockSpec((1, MEM_LEN, d), lambda b, i: (b, 0, 0)),
        _resident((1, d)),
        hbm, hbm, hbm, hbm, hbm,
        _resident((1, d)),
    ]
    scratch = [
        pltpu.VMEM((EXT, d), BF16),
        pltpu.VMEM((KV_WIDTH, EXT), BF16),
        pltpu.VMEM((KV_WIDTH, EXT), BF16),
        pltpu.VMEM((EXT, 2 * KV_WIDTH), BF16),
        pltpu.VMEM((EXT, 2 * KV_WIDTH), BF16),
        pltpu.VMEM((SB + 2 * CONV_HALO, A_WIDTH), F32),
        pltpu.VMEM((SB, d), F32),
        pltpu.VMEM((3, 2 * SB, d), F32),
        pltpu.VMEM((d, IN_WIDTH), BF16),
        pltpu.VMEM((d, 2 * MEM_WIDTH), BF16),
        pltpu.VMEM((A_WIDTH, d), BF16),
        pltpu.VMEM((ATTN_WIDTH, d), BF16),
        pltpu.VMEM((MEM_WIDTH, d), BF16),
        pltpu.VMEM((d, d), BF16),
        pltpu.VMEM((MEM_WIDTH, MEM_LEN), BF16),
        pltpu.VMEM((MEM_HEADS, MEM_LEN, 2 * MEM_HEAD_DIM), BF16),
        pltpu.SemaphoreType.DMA((7 + OUT_SLOTS // OUT_PER_BUF,)),
        pltpu.VMEM((OUT_SLOTS // OUT_PER_BUF, STAGE_ROWS, d), F32),
        pltpu.SemaphoreType.DMA((1,)),
    ]
    out = pl.pallas_call(
        _layer_kernel,
        grid=(bsz, s // TS),
        in_specs=in_specs,
        out_specs=hbm,
        out_shape=jax.ShapeDtypeStruct((bsz * s, d), x.dtype),
        scratch_shapes=scratch,
        compiler_params=pltpu.CompilerParams(dimension_semantics=("arbitrary", "arbitrary"),
                                             vmem_limit_bytes=VMEM_LIMIT_BYTES),
        name="hybrid_layer",
    )(attn_sink, x, x, g_pre.reshape(1, d), w_in, w_conv, cos_t, sin_t, bias_t, jnp.asarray(_in_scale()), mem,
      g_mem.reshape(1, d), w_mem_kv, w_up_a, w_up_b, w_up_m, w_out, g_post.reshape(1, d))
    return out.reshape(bsz, s, d)


def kernel(x, mem, g_pre, w_in, w_conv, attn_sink, g_mem, w_mem_kv, w_up_a, w_up_b, w_up_m, w_out, g_post):
    cos_np, sin_np = _rope_tables()
    cos_t, sin_t, bias_t = jnp.asarray(cos_np), jnp.asarray(sin_np), jnp.asarray(_band_bias())
    for l in range(g_pre.shape[0]):
        x = _layer(x, mem, g_pre[l], w_in[l], w_conv[l], attn_sink[l], g_mem[l], w_mem_kv[l], w_up_a[l], w_up_b[l],
                   w_up_m[l], w_out[l], g_post[l], cos_t, sin_t, bias_t)
    return x
```

```python
import math

import numpy as np
import jax
import jax.numpy as jnp
from jax import lax
from jax.experimental import pallas as pl
from jax.experimental.pallas import tpu as pltpu

F32 = jnp.float32
BF16 = jnp.bfloat16

D_MODEL = 1024
SEQ = 4096
MEM_LEN = 256
EPS = 1e-6
CONV_WIDTH = 3
A_WIDTH = 512
HEAD_DIM = 64
ATTN_WIDTH = 512
KV_WIDTH = 128
WINDOW = 128
ROPE_THETA = 500000.0
ROT_DIM = 16
MEM_HEADS = 4
MEM_HEAD_DIM = 128
MEM_WIDTH = 512

C_AB, C_AC, C_AX, C_AZ = 0, 512, 1024, 1536
C_BQ, C_BK, C_BV, C_BZ = 2048, 2560, 2688, 2816
C_MQ, C_MZ = 3328, 3840
C_G0, C_G1, C_G2 = 4352, 5376, 6400
IN_WIDTH = 7424

LANES = 128
BF16_ROWS = 16
V7X_VMEM_BYTES = 64 * 1024 * 1024
VMEM_LIMIT_BYTES = V7X_VMEM_BYTES - 2 * 1024 * 1024

SB = 256
NSB = 4
TS = SB * NSB
QB = WINDOW
HALO = WINDOW
CONV_HALO = BF16_ROWS
EXT = TS + 2 * HALO
STAGE_ROWS = SB
STAGE_COLS = D_MODEL
OUT_ROWS = SB
OUT_SLOTS = TS // OUT_ROWS
OUT_PER_BUF = STAGE_ROWS // OUT_ROWS

LOG2E = math.log2(math.e)


def _silu_half(hv):
    return hv * jnp.tanh(hv) + hv


def _gated(t, hu):
    return t * hu + hu


def _in_scale():
    sc = np.ones((1, IN_WIDTH), np.float32)
    for c0, width in ((C_AZ, A_WIDTH), (C_BZ, ATTN_WIDTH), (C_MZ, MEM_WIDTH), (C_G0, 3 * D_MODEL)):
        sc[:, c0:c0 + width] = 0.5
    return sc


def _rms(v, g):
    ms = jnp.mean(v * v, axis=-1, keepdims=True)
    return v * lax.rsqrt(ms + EPS) * g


def _dot(a, b):
    return jnp.dot(a, b, preferred_element_type=F32)


def _rope(t, cos, ssin, low8):
    partner = jnp.where(low8, pltpu.roll(t, LANES - ROT_DIM // 2, 1), pltpu.roll(t, ROT_DIM // 2, 1))
    return t * cos + partner * ssin


def _rope_tables():
    half = ROT_DIM // 2
    inv_freq = np.power(np.float32(ROPE_THETA), -np.arange(half, dtype=np.float32) * np.float32(2.0 / ROT_DIM))
    pos = (np.arange(SEQ + 2 * HALO) - HALO).astype(np.float32)
    ang = (pos[:, None] * inv_freq[None, :]).astype(np.float32)
    cos, sin = np.cos(ang).astype(np.float32), np.sin(ang).astype(np.float32)
    ct = np.ones((SEQ + 2 * HALO, HEAD_DIM), np.float32)
    st = np.zeros((SEQ + 2 * HALO, HEAD_DIM), np.float32)
    ct[:, :half], ct[:, half:ROT_DIM] = cos, cos
    st[:, :half], st[:, half:ROT_DIM] = -sin, sin
    return np.tile(ct, (1, LANES // HEAD_DIM)), np.tile(st, (1, LANES // HEAD_DIM))


def _band_bias():
    r = np.arange(QB)[:, None]
    c = np.arange(3 * QB)[None, :]
    band = (c >= r) & (c <= r + 2 * WINDOW)
    masks = [band, band & (c >= QB), band & (c < 2 * QB)]
    return np.stack([np.where(m, 0.0, -np.inf) for m in masks]).astype(np.float32)


def _stream_cast(jobs, slots, sem):
    chunks = [(src, dst, scale, r0, c0, min(STAGE_COLS, src.shape[1] - c0))
              for src, dst, scale in jobs
              for r0 in range(0, src.shape[0], STAGE_ROWS)
              for c0 in range(0, src.shape[1], STAGE_COLS)]
    n_slots = len(slots)

    def copy(n):
        src, _, _, r0, c0, cols = chunks[n]
        k = n % n_slots
        return pltpu.make_async_copy(src.at[pl.ds(r0, STAGE_ROWS), pl.ds(c0, cols)],
                                     slots[k].at[:, pl.ds(0, cols)], sem.at[k])

    for n in range(min(n_slots, len(chunks))):
        copy(n).start()
    for n, (_, dst, scale, r0, c0, cols) in enumerate(chunks):
        copy(n).wait()
        w = slots[n % n_slots][:, 0:cols]
        w = w * (scale if isinstance(scale, float) else scale[:, c0:c0 + cols])
        dst[r0:r0 + STAGE_ROWS, c0:c0 + cols] = w.astype(BF16)
        if n + n_slots < len(chunks):
            copy(n + n_slots).start()


def _window_softmax_pv(s, v_win, sink_col):
    m = jnp.maximum(jnp.max(s, axis=1, keepdims=True), sink_col)
    p = jnp.exp2(s - m)
    ov = _dot(p.astype(BF16), v_win)
    return ov[:, :LANES] / (ov[:, LANES:] + jnp.exp2(sink_col - m))


def _layer_kernel(sink_ref, x_ref, xn_ref, gpre_ref, win_hbm, wconv_ref, cos_ref, sin_ref, bias_ref, wscale_ref,
                  mem_ref, gmem_ref, wmkv_hbm, wua_hbm, wub_hbm, wum_hbm, wout_hbm, gpost_ref,
                  out_hbm, h_s, k_s, ksw_s, v_s, vsw_s, cu_s, mrg_s, gate_s,
                  win_ref, wmkv_ref, wua_ref, wub_ref, wum_ref, wout_ref, mkt_ref, mvx_ref,
                  stage_sem, out_s, out_sem):
    b = pl.program_id(0)
    i = pl.program_id(1)
    n_tiles = pl.num_programs(1)
    step = b * n_tiles + i
    last_step = pl.num_programs(0) * n_tiles - 1
    row0 = pl.multiple_of(i * TS, TS)
    tile_row = step * TS
    g_pre = gpre_ref[...]
    g_post = gpost_ref[...]

    def out_slot(slot):
        return out_s.at[slot // OUT_PER_BUF, pl.ds((slot % OUT_PER_BUF) * OUT_ROWS, OUT_ROWS)]

    def out_copy(slot, dst_row):
        return pltpu.make_async_copy(out_slot(slot), out_hbm.at[pl.ds(dst_row, OUT_ROWS)], out_sem.at[0])

    pending_out = []

    def write_out(slot, value, dst_row):
        out_slot(slot)[...] = value
        pending_out.append((slot, dst_row))

    def flush_out():
        for slot, dst_row in pending_out:
            out_copy(slot, dst_row).start()
        pending_out.clear()

    def wait_out_slots():
        for slot in range(OUT_SLOTS):
            out_copy(slot, 0).wait()

    @pl.when(step == 0)
    def _first_step():
        slots = [gate_s.at[k] for k in range(3)] + [mrg_s] + [out_s.at[k] for k in range(OUT_SLOTS // OUT_PER_BUF)]
        _stream_cast([(win_hbm, win_ref, wscale_ref), (wmkv_hbm, wmkv_ref, 1.0), (wua_hbm, wua_ref, 0.5),
                      (wub_hbm, wub_ref, 0.5), (wum_hbm, wum_ref, 0.5), (wout_hbm, wout_ref, 1.0)],
                     slots, stage_sem)
        for buf in (k_s, ksw_s):
            buf[:, TS:TS + HALO] = jnp.zeros((KV_WIDTH, HALO), BF16)
        for buf in (v_s, vsw_s):
            buf[TS:TS + HALO] = jnp.zeros((HALO, buf.shape[1]), BF16)
        cu_s[...] = jnp.zeros(cu_s.shape, F32)
        out_s[...] = jnp.zeros(out_s.shape, F32)
        for slot in range(OUT_SLOTS):
            out_copy(slot, slot * OUT_ROWS).start()

    @pl.when(i == 0)
    def _memory_kv():
        mn = _rms(mem_ref[0], gmem_ref[...]).astype(BF16)
        kv = _dot(mn, wmkv_ref[...])
        mkt_ref[...] = kv[:, :MEM_WIDTH].T.astype(BF16)
        ones_m = jnp.ones((MEM_LEN, MEM_HEAD_DIM), F32)
        for hh in range(MEM_HEADS):
            c0 = MEM_WIDTH + hh * MEM_HEAD_DIM
            mvx_ref[hh] = jnp.concatenate([kv[:, c0:c0 + MEM_HEAD_DIM], ones_m], axis=1).astype(BF16)

    lane = lax.broadcasted_iota(jnp.int32, (1, LANES), 1)
    low_head = lane < HEAD_DIM
    low8 = (lane % HEAD_DIM) < (ROT_DIM // 2)

    rows4 = lax.broadcasted_iota(jnp.int32, (4 * QB, 1), 0)
    def sink_rows(h0, h1, h2, h3):
        return LOG2E * jnp.where(rows4 < QB, sink_ref[h0],
                                 jnp.where(rows4 < 2 * QB, sink_ref[h1],
                                           jnp.where(rows4 < 3 * QB, sink_ref[h2], sink_ref[h3])))
    sink_a = sink_rows(0, 2, 5, 7)
    sink_b = sink_rows(1, 3, 4, 6)
    n_blocks = SEQ // QB

    def norm_main(sb):
        r0 = HALO + sb * SB
        h_s[r0:r0 + SB] = _rms(x_ref[0, sb * SB:(sb + 1) * SB], g_pre).astype(BF16)

    def norm_next_halo():
        h_s[HALO + TS:EXT] = _rms(xn_ref[0], g_pre).astype(BF16)

    def carry_prev_halo():
        for buf in (k_s, ksw_s):
            buf[:, 0:HALO] = buf[:, TS:TS + HALO]
        for buf in (v_s, vsw_s):
            buf[0:HALO] = buf[TS:TS + HALO]

    def kv_chunk(c):
        if c == 0:
            return
        r0 = HALO if c == 1 else c * SB
        r1 = (c + 1) * SB
        n = r1 - r0
        kv = _dot(h_s[r0:r1], win_ref[:, C_BK:C_BV + KV_WIDTH])
        cos = cos_ref[pl.ds(row0 + r0, n), :]
        ssin = sin_ref[pl.ds(row0 + r0, n), :]
        k = _rope(kv[:, :KV_WIDTH], cos, ssin, low8)
        v = kv[:, KV_WIDTH:]
        k_s[:, r0:r1] = k.T.astype(BF16)
        ksw_s[:, r0:r1] = pltpu.roll(k, HEAD_DIM, 1).T.astype(BF16)
        v_s[r0:r1, 0:LANES] = v.astype(BF16)
        ones = jnp.ones((n, LANES), BF16)
        v_s[r0:r1, LANES:2 * LANES] = ones
        vsw_s[r0:r1, 0:LANES] = pltpu.roll(v, HEAD_DIM, 1).astype(BF16)
        vsw_s[r0:r1, LANES:2 * LANES] = ones

    def sub_block(sb):
        m0 = HALO + sb * SB

        def gate(col, slot):
            gate_s[slot] = jnp.tanh(_dot(h, win_ref[:, col:col + D_MODEL]))

        def conv_input():
            have = CONV_HALO if sb > 0 else 0
            before = cu_s[SB:SB + CONV_HALO + have]
            hx = h_s[m0 + have:m0 + SB + CONV_HALO]
            cx = _dot(hx, win_ref[:, C_AC:C_AX + A_WIDTH])
            cu = cx[:, :A_WIDTH] * cx[:, A_WIDTH:]
            first = row0 + sb * SB == 0
            last = row0 + (sb + 1) * SB == SEQ
            cu_s[0:CONV_HALO + have] = before if sb > 0 else jnp.where(first, 0.0, before)
            cu_s[CONV_HALO + have:CONV_HALO + SB] = cu[0:SB - have]
            cu_s[CONV_HALO + SB:] = jnp.where(last, 0.0, cu[SB - have:])

        norm_main(sb)
        yield
        h = h_s[m0:m0 + SB]
        q = _dot(h, win_ref[:, C_BQ:C_BQ + ATTN_WIDTH])
        b_zs = _silu_half(_dot(h, win_ref[:, C_BZ:C_BZ + ATTN_WIDTH]))
        yield
        a_b = _dot(h, win_ref[:, C_AB:C_AB + A_WIDTH])
        a_zs = _silu_half(_dot(h, win_ref[:, C_AZ:C_AZ + A_WIDTH]))
        scale = (HEAD_DIM ** -0.5) * LOG2E
        qcos = cos_ref[pl.ds(row0 + m0, SB), :] * scale
        qsin = sin_ref[pl.ds(row0 + m0, SB), :] * scale
        q_lo, q_hi = [], []
        for mblk in range(ATTN_WIDTH // LANES):
            qr = _rope(q[:, mblk * LANES:(mblk + 1) * LANES], qcos, qsin, low8)
            q_lo.append(jnp.where(low_head, qr, 0.0).astype(BF16))
            q_hi.append(jnp.where(low_head, 0.0, qr).astype(BF16))
        yield
        fillers = [lambda: gate(C_G1, 1), conv_input]
        yb_blocks = []
        for j in range(SB // QB):
            qs = slice(j * QB, (j + 1) * QB)
            jb = sb * (SB // QB) + j
            blk = i * (TS // QB) + jb
            variant = jnp.where(blk == 0, 1, jnp.where(blk == n_blocks - 1, 2, 0))
            bias1 = bias_ref[variant]
            bias = jnp.concatenate([bias1, bias1, bias1, bias1], axis=0)
            lhs_a = jnp.concatenate([q_lo[0][qs], q_lo[1][qs], q_hi[2][qs], q_hi[3][qs]], axis=0)
            lhs_b = jnp.concatenate([q_hi[0][qs], q_hi[1][qs], q_lo[2][qs], q_lo[3][qs]], axis=0)
            ws = slice(jb * QB, jb * QB + 3 * QB)
            s_a = _dot(lhs_a, k_s[:, ws]) + bias
            s_b = _dot(lhs_b, ksw_s[:, ws]) + bias
            fillers[j % len(fillers)]()
            o_a = _window_softmax_pv(s_a, v_s[ws], sink_a)
            o_b = _window_softmax_pv(s_b, vsw_s[ws], sink_b)
            cols = [jnp.where(low_head, o_a[0:QB], o_b[0:QB]),
                    jnp.where(low_head, o_a[QB:2 * QB], o_b[QB:2 * QB]),
                    jnp.where(low_head, o_b[2 * QB:3 * QB], o_a[2 * QB:3 * QB]),
                    jnp.where(low_head, o_b[3 * QB:4 * QB], o_a[3 * QB:4 * QB])]
            yb_blocks.append(jnp.concatenate(cols, axis=1))
            yield
        yb = jnp.concatenate(yb_blocks, axis=0) * b_zs
        ub = _dot(yb.astype(BF16), wub_ref[...])
        mrg_s[...] = _gated(gate_s[1], ub)
        gate(C_G0, 0)
        y = (cu_s[CONV_HALO - 1:CONV_HALO - 1 + SB] * wconv_ref[0:1]
             + cu_s[CONV_HALO:CONV_HALO + SB] * wconv_ref[1:2]
             + cu_s[CONV_HALO + 1:CONV_HALO + 1 + SB] * wconv_ref[2:3])
        ya = a_b * y * a_zs
        ua = _dot(ya.astype(BF16), wua_ref[...])
        mrg_s[...] += _gated(gate_s[0], ua)
        yield
        mq = _dot(h, win_ref[:, C_MQ:C_MQ + MEM_WIDTH]) * ((MEM_HEAD_DIM ** -0.5) * LOG2E)
        m_zs = _silu_half(_dot(h, win_ref[:, C_MZ:C_MZ + MEM_WIDTH]))
        scores = []
        for hh in range(MEM_HEADS):
            hs = slice(hh * MEM_HEAD_DIM, (hh + 1) * MEM_HEAD_DIM)
            scores.append(_dot(mq[:, hs].astype(BF16), mkt_ref[hs, :]))
        gate(C_G2, 2)
        ym_heads = []
        for hh in range(MEM_HEADS):
            s = scores[hh]
            p = jnp.exp2(s - jnp.max(s, axis=1, keepdims=True))
            ov = _dot(p.astype(BF16), mvx_ref[hh])
            ym_heads.append(ov[:, :MEM_HEAD_DIM] / ov[:, MEM_HEAD_DIM:])
        ym = (jnp.concatenate(ym_heads, axis=1) * m_zs).astype(BF16)
        yield
        for half in range(SB // OUT_ROWS):
            rs = slice(half * OUT_ROWS, (half + 1) * OUT_ROWS)
            t0 = sb * SB + half * OUT_ROWS
            um = _dot(ym[rs], wum_ref[...])
            merged = mrg_s[rs] + _gated(gate_s[2, rs], um)
            o = _dot(merged.astype(BF16), wout_ref[...])
            write_out(t0 // OUT_ROWS, x_ref[0, t0:t0 + OUT_ROWS] + _rms(o, g_post), tile_row + t0)
        yield

    gens = [sub_block(sb) for sb in range(NSB)]
    def run(sb, count=1):
        for _ in range(count):
            next(gens[sb])

    def norm_for_chunk(c):
        if c < NSB:
            run(c)
        else:
            norm_next_halo()

    wait_out_slots()
    carry_prev_halo()
    run(0, 2)
    norm_for_chunk(1)
    kv_chunk(0)
    kv_chunk(1)
    for sb in range(NSB):
        run(sb)
        if sb > 0:
            run(sb - 1)
        run(sb, 2)
        if sb + 1 < NSB:
            norm_for_chunk(sb + 2)
        run(sb)
        flush_out()
        run(sb)
        if sb + 1 < NSB:
            run(sb + 1)
            kv_chunk(sb + 2)
    run(NSB - 1)
    flush_out()

    @pl.when(step == last_step)
    def _last_step():
        wait_out_slots()


def _resident(shape):
    return pl.BlockSpec(shape, lambda b, i: (0,) * len(shape), pipeline_mode=pl.Buffered(1))


def _layer(x, mem, g_pre, w_in, w_conv, attn_sink, g_mem, w_mem_kv, w_up_a, w_up_b, w_up_m, w_out, g_post,
           cos_t, sin_t, bias_t):
    bsz, s, d = x.shape
    assert (s, d) == (SEQ, D_MODEL) and s % TS == 0 and SB % QB == 0 and SB == 2 * HALO
    assert (STAGE_ROWS, STAGE_COLS) == (SB, d)
    nb = TS // HALO
    hbm = pl.BlockSpec(memory_space=pl.ANY)
    in_specs = [
        pl.BlockSpec(memory_space=pltpu.SMEM),
        pl.BlockSpec((1, TS, d), lambda b, i: (b, i, 0)),
        pl.BlockSpec((1, HALO, d), lambda b, i: (b, jnp.minimum((i + 1) * nb, s // HALO - 1), 0)),
        _resident((1, d)),
        hbm,
        _resident((CONV_WIDTH, A_WIDTH)),
        _resident((s + 2 * HALO, LANES)),
        _resident((s + 2 * HALO, LANES)),
        _resident((3, QB, 3 * QB)),
        _resident((1, IN_WIDTH)),
        pl.BlockSpec((1, MEM_LEN, d), lambda b, i: (b, 0, 0)),
        _resident((1, d)),
        hbm, hbm, hbm, hbm, hbm,
        _resident((1, d)),
    ]
    scratch = [
        pltpu.VMEM((EXT, d), BF16),
        pltpu.VMEM((KV_WIDTH, EXT), BF16),
        pltpu.VMEM((KV_WIDTH, EXT), BF16),
        pltpu.VMEM((EXT, 2 * KV_WIDTH), BF16),
        pltpu.VMEM((EXT, 2 * KV_WIDTH), BF16),
        pltpu.VMEM((SB + 2 * CONV_HALO, A_WIDTH), F32),
        pltpu.VMEM((SB, d), F32),
        pltpu.VMEM((3, SB, d), F32),
        pltpu.VMEM((d, IN_WIDTH), BF16),
        pltpu.VMEM((d, 2 * MEM_WIDTH), BF16),
        pltpu.VMEM((A_WIDTH, d), BF16),
        pltpu.VMEM((ATTN_WIDTH, d), BF16),
        pltpu.VMEM((MEM_WIDTH, d), BF16),
        pltpu.VMEM((d, d), BF16),
        pltpu.VMEM((MEM_WIDTH, MEM_LEN), BF16),
        pltpu.VMEM((MEM_HEADS, MEM_LEN, 2 * MEM_HEAD_DIM), BF16),
        pltpu.SemaphoreType.DMA((4 + OUT_SLOTS // OUT_PER_BUF,)),
        pltpu.VMEM((OUT_SLOTS // OUT_PER_BUF, STAGE_ROWS, d), F32),
        pltpu.SemaphoreType.DMA((1,)),
    ]
    out = pl.pallas_call(
        _layer_kernel,
        grid=(bsz, s // TS),
        in_specs=in_specs,
        out_specs=hbm,
        out_shape=jax.ShapeDtypeStruct((bsz * s, d), x.dtype),
        scratch_shapes=scratch,
        compiler_params=pltpu.CompilerParams(dimension_semantics=("arbitrary", "arbitrary"),
                                             vmem_limit_bytes=VMEM_LIMIT_BYTES),
        name="hybrid_layer",
    )(attn_sink, x, x, g_pre.reshape(1, d), w_in, w_conv, cos_t, sin_t, bias_t, jnp.asarray(_in_scale()), mem,
      g_mem.reshape(1, d), w_mem_kv, w_up_a, w_up_b, w_up_m, w_out, g_post.reshape(1, d))
    return out.reshape(bsz, s, d)


def kernel(x, mem, g_pre, w_in, w_conv, attn_sink, g_mem, w_mem_kv, w_up_a, w_up_b, w_up_m, w_out, g_post):
    cos_np, sin_np = _rope_tables()
    cos_t, sin_t, bias_t = jnp.asarray(cos_np), jnp.asarray(sin_np), jnp.asarray(_band_bias())
    for l in range(g_pre.shape[0]):
        x = _layer(x, mem, g_pre[l], w_in[l], w_conv[l], attn_sink[l], g_mem[l], w_mem_kv[l], w_up_a[l], w_up_b[l],
                   w_up_m[l], w_out[l], g_post[l], cos_t, sin_t, bias_t)
    return x
```

```python
import math

import numpy as np
import jax
import jax.numpy as jnp
from jax import lax
from jax.experimental import pallas as pl
from jax.experimental.pallas import tpu as pltpu

F32 = jnp.float32
BF16 = jnp.bfloat16

D_MODEL = 1024
SEQ = 4096
MEM_LEN = 256
EPS = 1e-6
CONV_WIDTH = 3
A_WIDTH = 512
HEAD_DIM = 64
ATTN_WIDTH = 512
KV_WIDTH = 128
WINDOW = 128
ROPE_THETA = 500000.0
ROT_DIM = 16
MEM_HEADS = 4
MEM_HEAD_DIM = 128
MEM_WIDTH = 512

C_AB, C_AC, C_AX, C_AZ = 0, 512, 1024, 1536
C_BQ, C_BK, C_BV, C_BZ = 2048, 2560, 2688, 2816
C_MQ, C_MZ = 3328, 3840
C_G0, C_G1, C_G2 = 4352, 5376, 6400
IN_WIDTH = 7424

LANES = 128
BF16_ROWS = 16
V7X_VMEM_BYTES = 64 * 1024 * 1024
VMEM_LIMIT_BYTES = V7X_VMEM_BYTES - 2 * 1024 * 1024

SB = 256
NSB = 4
TS = SB * NSB
QB = WINDOW
HALO = WINDOW
CONV_HALO = BF16_ROWS
EXT = TS + 2 * HALO
STAGE_ROWS = SB
STAGE_COLS = D_MODEL
OUT_ROWS = SB
OUT_SLOTS = TS // OUT_ROWS
OUT_PER_BUF = STAGE_ROWS // OUT_ROWS

LOG2E = math.log2(math.e)


def _silu_half(hv):
    return hv * jnp.tanh(hv) + hv


def _gated(t, hu):
    return t * hu + hu


def _in_scale():
    sc = np.ones((1, IN_WIDTH), np.float32)
    for c0, width in ((C_AZ, A_WIDTH), (C_BZ, ATTN_WIDTH), (C_MZ, MEM_WIDTH), (C_G0, 3 * D_MODEL)):
        sc[:, c0:c0 + width] = 0.5
    return sc


def _rms(v, g):
    ms = jnp.mean(v * v, axis=-1, keepdims=True)
    return v * lax.rsqrt(ms + EPS) * g


def _dot(a, b):
    return jnp.dot(a, b, preferred_element_type=F32)


def _rope(t, cos, ssin, low8):
    partner = jnp.where(low8, pltpu.roll(t, LANES - ROT_DIM // 2, 1), pltpu.roll(t, ROT_DIM // 2, 1))
    return t * cos + partner * ssin


def _rope_tables():
    half = ROT_DIM // 2
    inv_freq = np.power(np.float32(ROPE_THETA), -np.arange(half, dtype=np.float32) * np.float32(2.0 / ROT_DIM))
    pos = (np.arange(SEQ + 2 * HALO) - HALO).astype(np.float32)
    ang = (pos[:, None] * inv_freq[None, :]).astype(np.float32)
    cos, sin = np.cos(ang).astype(np.float32), np.sin(ang).astype(np.float32)
    ct = np.ones((SEQ + 2 * HALO, HEAD_DIM), np.float32)
    st = np.zeros((SEQ + 2 * HALO, HEAD_DIM), np.float32)
    ct[:, :half], ct[:, half:ROT_DIM] = cos, cos
    st[:, :half], st[:, half:ROT_DIM] = -sin, sin
    return np.tile(ct, (1, LANES // HEAD_DIM)), np.tile(st, (1, LANES // HEAD_DIM))


def _band_bias():
    r = np.arange(QB)[:, None]
    c = np.arange(3 * QB)[None, :]
    band = (c >= r) & (c <= r + 2 * WINDOW)
    masks = [band, band & (c >= QB), band & (c < 2 * QB)]
    return np.stack([np.where(m, 0.0, -np.inf) for m in masks]).astype(np.float32)


def _stream_cast(jobs, slots, sem):
    chunks = [(src, dst, scale, r0, c0, min(STAGE_COLS, src.shape[1] - c0))
              for src, dst, scale in jobs
              for r0 in range(0, src.shape[0], STAGE_ROWS)
              for c0 in range(0, src.shape[1], STAGE_COLS)]
    n_slots = len(slots)

    def copy(n):
        src, _, _, r0, c0, cols = chunks[n]
        k = n % n_slots
        return pltpu.make_async_copy(src.at[pl.ds(r0, STAGE_ROWS), pl.ds(c0, cols)],
                                     slots[k].at[:, pl.ds(0, cols)], sem.at[k])

    for n in range(min(n_slots, len(chunks))):
        copy(n).start()
    for n, (_, dst, scale, r0, c0, cols) in enumerate(chunks):
        copy(n).wait()
        w = slots[n % n_slots][:, 0:cols]
        w = w * (scale if isinstance(scale, float) else scale[:, c0:c0 + cols])
        dst[r0:r0 + STAGE_ROWS, c0:c0 + cols] = w.astype(BF16)
        if n + n_slots < len(chunks):
            copy(n + n_slots).start()


def _window_softmax_pv(s, v_win, sink_col):
    m = jnp.maximum(jnp.max(s, axis=1, keepdims=True), sink_col)
    p = jnp.exp2(s - m)
    ov = _dot(p.astype(BF16), v_win)
    return ov[:, :LANES] / (ov[:, LANES:] + jnp.exp2(sink_col - m))


def _layer_kernel(sink_ref, x_ref, xn_ref, gpre_ref, win_hbm, wconv_ref, cos_ref, sin_ref, bias_ref, wscale_ref,
                  mem_ref, gmem_ref, wmkv_hbm, wua_hbm, wub_hbm, wum_hbm, wout_hbm, gpost_ref,
                  out_hbm, h_s, k_s, ksw_s, v_s, vsw_s, cu_s, mrg_s, gate_s,
                  win_ref, wmkv_ref, wua_ref, wub_ref, wum_ref, wout_ref, mkt_ref, mvx_ref,
                  stage_sem, out_s, out_sem):
    b = pl.program_id(0)
    i = pl.program_id(1)
    n_tiles = pl.num_programs(1)
    step = b * n_tiles + i
    last_step = pl.num_programs(0) * n_tiles - 1
    row0 = pl.multiple_of(i * TS, TS)
    tile_row = step * TS
    g_pre = gpre_ref[...]
    g_post = gpost_ref[...]

    def out_slot(slot):
        return out_s.at[slot // OUT_PER_BUF, pl.ds((slot % OUT_PER_BUF) * OUT_ROWS, OUT_ROWS)]

    def out_copy(slot, dst_row):
        return pltpu.make_async_copy(out_slot(slot), out_hbm.at[pl.ds(dst_row, OUT_ROWS)], out_sem.at[0])

    pending_out = []

    def write_out(slot, value, dst_row):
        out_slot(slot)[...] = value
        pending_out.append((slot, dst_row))

    def flush_out():
        for slot, dst_row in pending_out:
            out_copy(slot, dst_row).start()
        pending_out.clear()

    def wait_out_slots():
        for slot in range(OUT_SLOTS):
            out_copy(slot, 0).wait()

    @pl.when(step == 0)
    def _first_step():
        slots = [gate_s.at[k] for k in range(3)] + [mrg_s] + [out_s.at[k] for k in range(OUT_SLOTS // OUT_PER_BUF)]
        _stream_cast([(win_hbm, win_ref, wscale_ref), (wmkv_hbm, wmkv_ref, 1.0), (wua_hbm, wua_ref, 0.5),
                      (wub_hbm, wub_ref, 0.5), (wum_hbm, wum_ref, 0.5), (wout_hbm, wout_ref, 1.0)],
                     slots, stage_sem)
        for buf in (k_s, ksw_s):
            buf[:, TS:TS + HALO] = jnp.zeros((KV_WIDTH, HALO), BF16)
        for buf in (v_s, vsw_s):
            buf[TS:TS + HALO] = jnp.zeros((HALO, buf.shape[1]), BF16)
        cu_s[...] = jnp.zeros(cu_s.shape, F32)
        out_s[...] = jnp.zeros(out_s.shape, F32)
        for slot in range(OUT_SLOTS):
            out_copy(slot, slot * OUT_ROWS).start()

    @pl.when(i == 0)
    def _memory_kv():
        mn = _rms(mem_ref[0], gmem_ref[...]).astype(BF16)
        kv = _dot(mn, wmkv_ref[...])
        mkt_ref[...] = kv[:, :MEM_WIDTH].T.astype(BF16)
        ones_m = jnp.ones((MEM_LEN, MEM_HEAD_DIM), F32)
        for hh in range(MEM_HEADS):
            c0 = MEM_WIDTH + hh * MEM_HEAD_DIM
            mvx_ref[hh] = jnp.concatenate([kv[:, c0:c0 + MEM_HEAD_DIM], ones_m], axis=1).astype(BF16)

    lane = lax.broadcasted_iota(jnp.int32, (1, LANES), 1)
    low_head = lane < HEAD_DIM
    low8 = (lane % HEAD_DIM) < (ROT_DIM // 2)

    rows4 = lax.broadcasted_iota(jnp.int32, (4 * QB, 1), 0)
    def sink_rows(h0, h1, h2, h3):
        return LOG2E * jnp.where(rows4 < QB, sink_ref[h0],
                                 jnp.where(rows4 < 2 * QB, sink_ref[h1],
                                           jnp.where(rows4 < 3 * QB, sink_ref[h2], sink_ref[h3])))
    sink_a = sink_rows(0, 2, 5, 7)
    sink_b = sink_rows(1, 3, 4, 6)
    n_blocks = SEQ // QB
    ones = jnp.ones((SB, LANES), BF16)

    def norm_main(sb):
        r0 = HALO + sb * SB
        h_s[r0:r0 + SB] = _rms(x_ref[0, sb * SB:(sb + 1) * SB], g_pre).astype(BF16)

    def norm_next_halo():
        h_s[HALO + TS:EXT] = _rms(xn_ref[0], g_pre).astype(BF16)

    def carry_prev_halo():
        inside = i > 0
        for buf in (k_s, ksw_s):
            buf[:, 0:HALO] = jnp.where(inside, buf[:, TS:TS + HALO], jnp.zeros((KV_WIDTH, HALO), BF16))
        for buf in (v_s, vsw_s):
            buf[0:HALO] = jnp.where(inside, buf[TS:TS + HALO], jnp.zeros((HALO, buf.shape[1]), BF16))

    def kv_chunk(c):
        r0 = HALO if c == 0 else c * SB
        r1 = (c + 1) * SB
        n = r1 - r0
        kv = _dot(h_s[r0:r1], win_ref[:, C_BK:C_BV + KV_WIDTH])
        cos = cos_ref[pl.ds(row0 + r0, n), :]
        ssin = sin_ref[pl.ds(row0 + r0, n), :]
        k = _rope(kv[:, :KV_WIDTH], cos, ssin, low8)
        v = kv[:, KV_WIDTH:]
        k_s[:, r0:r1] = k.T.astype(BF16)
        ksw_s[:, r0:r1] = pltpu.roll(k, HEAD_DIM, 1).T.astype(BF16)
        v_s[r0:r1, 0:LANES] = v.astype(BF16)
        v_s[r0:r1, LANES:2 * LANES] = ones[0:n]
        vsw_s[r0:r1, 0:LANES] = pltpu.roll(v, HEAD_DIM, 1).astype(BF16)
        vsw_s[r0:r1, LANES:2 * LANES] = ones[0:n]

    def sub_block(sb):
        m0 = HALO + sb * SB

        def gate(col, slot):
            gate_s[slot] = jnp.tanh(_dot(h, win_ref[:, col:col + D_MODEL]))

        def conv_input():
            have = CONV_HALO if sb > 0 else 0
            before = cu_s[SB:SB + CONV_HALO + have]
            hx = h_s[m0 + have:m0 + SB + CONV_HALO]
            cx = _dot(hx, win_ref[:, C_AC:C_AX + A_WIDTH])
            cu = cx[:, :A_WIDTH] * cx[:, A_WIDTH:]
            first = row0 + sb * SB == 0
            last = row0 + (sb + 1) * SB == SEQ
            cu_s[0:CONV_HALO + have] = before if sb > 0 else jnp.where(first, 0.0, before)
            cu_s[CONV_HALO + have:CONV_HALO + SB] = cu[0:SB - have]
            cu_s[CONV_HALO + SB:] = jnp.where(last, 0.0, cu[SB - have:])

        norm_main(sb)
        yield
        h = h_s[m0:m0 + SB]
        q = _dot(h, win_ref[:, C_BQ:C_BQ + ATTN_WIDTH])
        b_zs = _silu_half(_dot(h, win_ref[:, C_BZ:C_BZ + ATTN_WIDTH]))
        yield
        a_b = _dot(h, win_ref[:, C_AB:C_AB + A_WIDTH])
        a_zs = _silu_half(_dot(h, win_ref[:, C_AZ:C_AZ + A_WIDTH]))
        scale = (HEAD_DIM ** -0.5) * LOG2E
        qcos = cos_ref[pl.ds(row0 + m0, SB), :] * scale
        qsin = sin_ref[pl.ds(row0 + m0, SB), :] * scale
        q_lo, q_hi = [], []
        for mblk in range(ATTN_WIDTH // LANES):
            qr = _rope(q[:, mblk * LANES:(mblk + 1) * LANES], qcos, qsin, low8)
            q_lo.append(jnp.where(low_head, qr, 0.0).astype(BF16))
            q_hi.append(jnp.where(low_head, 0.0, qr).astype(BF16))
        yield
        fillers = [lambda: gate(C_G1, 1), conv_input]
        yb_blocks = []
        for j in range(SB // QB):
            qs = slice(j * QB, (j + 1) * QB)
            jb = sb * (SB // QB) + j
            blk = i * (TS // QB) + jb
            variant = jnp.where(blk == 0, 1, jnp.where(blk == n_blocks - 1, 2, 0))
            bias1 = bias_ref[variant]
            bias = jnp.concatenate([bias1, bias1, bias1, bias1], axis=0)
            lhs_a = jnp.concatenate([q_lo[0][qs], q_lo[1][qs], q_hi[2][qs], q_hi[3][qs]], axis=0)
            lhs_b = jnp.concatenate([q_hi[0][qs], q_hi[1][qs], q_lo[2][qs], q_lo[3][qs]], axis=0)
            ws = slice(jb * QB, jb * QB + 3 * QB)
            s_a = _dot(lhs_a, k_s[:, ws]) + bias
            s_b = _dot(lhs_b, ksw_s[:, ws]) + bias
            fillers[j % len(fillers)]()
            o_a = _window_softmax_pv(s_a, v_s[ws], sink_a)
            o_b = _window_softmax_pv(s_b, vsw_s[ws], sink_b)
            cols = [jnp.where(low_head, o_a[0:QB], o_b[0:QB]),
                    jnp.where(low_head, o_a[QB:2 * QB], o_b[QB:2 * QB]),
                    jnp.where(low_head, o_b[2 * QB:3 * QB], o_a[2 * QB:3 * QB]),
                    jnp.where(low_head, o_b[3 * QB:4 * QB], o_a[3 * QB:4 * QB])]
            yb_blocks.append(jnp.concatenate(cols, axis=1))
            yield
        yb = jnp.concatenate(yb_blocks, axis=0) * b_zs
        ub = _dot(yb.astype(BF16), wub_ref[...])
        mrg_s[...] = _gated(gate_s[1], ub)
        gate(C_G0, 0)
        y = (cu_s[CONV_HALO - 1:CONV_HALO - 1 + SB] * wconv_ref[0:1]
             + cu_s[CONV_HALO:CONV_HALO + SB] * wconv_ref[1:2]
             + cu_s[CONV_HALO + 1:CONV_HALO + 1 + SB] * wconv_ref[2:3])
        ya = a_b * y * a_zs
        ua = _dot(ya.astype(BF16), wua_ref[...])
        mrg_s[...] += _gated(gate_s[0], ua)
        yield
        mq = _dot(h, win_ref[:, C_MQ:C_MQ + MEM_WIDTH]) * ((MEM_HEAD_DIM ** -0.5) * LOG2E)
        m_zs = _silu_half(_dot(h, win_ref[:, C_MZ:C_MZ + MEM_WIDTH]))
        scores = []
        for hh in range(MEM_HEADS):
            hs = slice(hh * MEM_HEAD_DIM, (hh + 1) * MEM_HEAD_DIM)
            scores.append(_dot(mq[:, hs].astype(BF16), mkt_ref[hs, :]))
        gate(C_G2, 2)
        ym_heads = []
        for hh in range(MEM_HEADS):
            s = scores[hh]
            p = jnp.exp2(s - jnp.max(s, axis=1, keepdims=True))
            ov = _dot(p.astype(BF16), mvx_ref[hh])
            ym_heads.append(ov[:, :MEM_HEAD_DIM] / ov[:, MEM_HEAD_DIM:])
        ym = (jnp.concatenate(ym_heads, axis=1) * m_zs).astype(BF16)
        yield
        for half in range(SB // OUT_ROWS):
            rs = slice(half * OUT_ROWS, (half + 1) * OUT_ROWS)
            t0 = sb * SB + half * OUT_ROWS
            um = _dot(ym[rs], wum_ref[...])
            merged = mrg_s[rs] + _gated(gate_s[2, rs], um)
            o = _dot(merged.astype(BF16), wout_ref[...])
            write_out(t0 // OUT_ROWS, x_ref[0, t0:t0 + OUT_ROWS] + _rms(o, g_post), tile_row + t0)
        yield

    gens = [sub_block(sb) for sb in range(NSB)]
    def run(sb, count=1):
        for _ in range(count):
            next(gens[sb])

    def norm_for_chunk(c):
        if c < NSB:
            run(c)
        else:
            norm_next_halo()

    wait_out_slots()
    carry_prev_halo()
    run(0, 2)
    norm_for_chunk(1)
    kv_chunk(0)
    kv_chunk(1)
    for sb in range(NSB):
        run(sb)
        if sb > 0:
            run(sb - 1)
        run(sb, 2)
        if sb + 1 < NSB:
            norm_for_chunk(sb + 2)
        run(sb)
        flush_out()
        run(sb)
        if sb + 1 < NSB:
            run(sb + 1)
            kv_chunk(sb + 2)
    run(NSB - 1)
    flush_out()

    @pl.when(step == last_step)
    def _last_step():
        wait_out_slots()


def _resident(shape):
    return pl.BlockSpec(shape, lambda b, i: (0,) * len(shape), pipeline_mode=pl.Buffered(1))


def _layer(x, mem, g_pre, w_in, w_conv, attn_sink, g_mem, w_mem_kv, w_up_a, w_up_b, w_up_m, w_out, g_post,
           cos_t, sin_t, bias_t):
    bsz, s, d = x.shape
    assert (s, d) == (SEQ, D_MODEL) and s % TS == 0 and SB % QB == 0 and SB == 2 * HALO
    assert (STAGE_ROWS, STAGE_COLS) == (SB, d)
    nb = TS // HALO
    hbm = pl.BlockSpec(memory_space=pl.ANY)
    in_specs = [
        pl.BlockSpec(memory_space=pltpu.SMEM),
        pl.BlockSpec((1, TS, d), lambda b, i: (b, i, 0)),
        pl.BlockSpec((1, HALO, d), lambda b, i: (b, jnp.minimum((i + 1) * nb, s // HALO - 1), 0)),
        _resident((1, d)),
        hbm,
        _resident((CONV_WIDTH, A_WIDTH)),
        _resident((s + 2 * HALO, LANES)),
        _resident((s + 2 * HALO, LANES)),
        _resident((3, QB, 3 * QB)),
        _resident((1, IN_WIDTH)),
        pl.BlockSpec((1, MEM_LEN, d), lambda b, i: (b, 0, 0)),
        _resident((1, d)),
        hbm, hbm, hbm, hbm, hbm,
        _resident((1, d)),
    ]
    scratch = [
        pltpu.VMEM((EXT, d), BF16),
        pltpu.VMEM((KV_WIDTH, EXT), BF16),
        pltpu.VMEM((KV_WIDTH, EXT), BF16),
        pltpu.VMEM((EXT, 2 * KV_WIDTH), BF16),
        pltpu.VMEM((EXT, 2 * KV_WIDTH), BF16),
        pltpu.VMEM((SB + 2 * CONV_HALO, A_WIDTH), F32),
        pltpu.VMEM((SB, d), F32),
        pltpu.VMEM((3, SB, d), F32),
        pltpu.VMEM((d, IN_WIDTH), BF16),
        pltpu.VMEM((d, 2 * MEM_WIDTH), BF16),
        pltpu.VMEM((A_WIDTH, d), BF16),
        pltpu.VMEM((ATTN_WIDTH, d), BF16),
        pltpu.VMEM((MEM_WIDTH, d), BF16),
        pltpu.VMEM((d, d), BF16),
        pltpu.VMEM((MEM_WIDTH, MEM_LEN), BF16),
        pltpu.VMEM((MEM_HEADS, MEM_LEN, 2 * MEM_HEAD_DIM), BF16),
        pltpu.SemaphoreType.DMA((4 + OUT_SLOTS // OUT_PER_BUF,)),
        pltpu.VMEM((OUT_SLOTS // OUT_PER_BUF, STAGE_ROWS, d), F32),
        pltpu.SemaphoreType.DMA((1,)),
    ]
    out = pl.pallas_call(
        _layer_kernel,
        grid=(bsz, s // TS),
        in_specs=in_specs,
        out_specs=hbm,
        out_shape=jax.ShapeDtypeStruct((bsz * s, d), x.dtype),
        scratch_shapes=scratch,
        compiler_params=pltpu.CompilerParams(dimension_semantics=("arbitrary", "arbitrary"),
                                             vmem_limit_bytes=VMEM_LIMIT_BYTES),
        name="hybrid_layer",
    )(attn_sink, x, x, g_pre.reshape(1, d), w_in, w_conv, cos_t, sin_t, bias_t, jnp.asarray(_in_scale()), mem,
      g_mem.reshape(1, d), w_mem_kv, w_up_a, w_up_b, w_up_m, w_out, g_post.reshape(1, d))
    return out.reshape(bsz, s, d)


def kernel(x, mem, g_pre, w_in, w_conv, attn_sink, g_mem, w_mem_kv, w_up_a, w_up_b, w_up_m, w_out, g_post):
    cos_np, sin_np = _rope_tables()
    cos_t, sin_t, bias_t = jnp.asarray(cos_np), jnp.asarray(sin_np), jnp.asarray(_band_bias())
    for l in range(g_pre.shape[0]):
        x = _layer(x, mem, g_pre[l], w_in[l], w_conv[l], attn_sink[l], g_mem[l], w_mem_kv[l], w_up_a[l], w_up_b[l],
                   w_up_m[l], w_out[l], g_post[l], cos_t, sin_t, bias_t)
    return x
```

```python
import math

import numpy as np
import jax
import jax.numpy as jnp
from jax import lax
from jax.experimental import pallas as pl
from jax.experimental.pallas import tpu as pltpu

F32 = jnp.float32
BF16 = jnp.bfloat16

D_MODEL = 1024
SEQ = 4096
MEM_LEN = 256
EPS = 1e-6
CONV_WIDTH = 3
A_WIDTH = 512
HEAD_DIM = 64
ATTN_WIDTH = 512
KV_WIDTH = 128
WINDOW = 128
ROPE_THETA = 500000.0
ROT_DIM = 16
MEM_HEADS = 4
MEM_HEAD_DIM = 128
MEM_WIDTH = 512

C_AB, C_AC, C_AX, C_AZ = 0, 512, 1024, 1536
C_BQ, C_BK, C_BV, C_BZ = 2048, 2560, 2688, 2816
C_MQ, C_MZ = 3328, 3840
C_G0, C_G1, C_G2 = 4352, 5376, 6400
IN_WIDTH = 7424

LANES = 128
BF16_ROWS = 16
V7X_VMEM_BYTES = 64 * 1024 * 1024
VMEM_LIMIT_BYTES = V7X_VMEM_BYTES - 2 * 1024 * 1024

SB = 256
NSB = 4
TS = SB * NSB
QB = WINDOW
HALO = WINDOW
CONV_HALO = BF16_ROWS
EXT = TS + 2 * HALO
STAGE_ROWS = SB
STAGE_COLS = D_MODEL
OUT_ROWS = SB
OUT_SLOTS = TS // OUT_ROWS
OUT_PER_BUF = STAGE_ROWS // OUT_ROWS

LOG2E = math.log2(math.e)


def _silu_half(hv):
    return hv * jnp.tanh(hv) + hv


def _gated(t, hu):
    return t * hu + hu


def _in_scale():
    sc = np.ones((1, IN_WIDTH), np.float32)
    for c0, width in ((C_AZ, A_WIDTH), (C_BZ, ATTN_WIDTH), (C_MZ, MEM_WIDTH), (C_G0, 3 * D_MODEL)):
        sc[:, c0:c0 + width] = 0.5
    return sc


def _rms(v, g):
    ms = jnp.mean(v * v, axis=-1, keepdims=True)
    return v * lax.rsqrt(ms + EPS) * g


def _dot(a, b):
    return jnp.dot(a, b, preferred_element_type=F32)


def _rope(t, cos, ssin, low8):
    partner = jnp.where(low8, pltpu.roll(t, LANES - ROT_DIM // 2, 1), pltpu.roll(t, ROT_DIM // 2, 1))
    return t * cos + partner * ssin


def _rope_tables():
    half = ROT_DIM // 2
    inv_freq = np.power(np.float32(ROPE_THETA), -np.arange(half, dtype=np.float32) * np.float32(2.0 / ROT_DIM))
    pos = (np.arange(SEQ + 2 * HALO) - HALO).astype(np.float32)
    ang = (pos[:, None] * inv_freq[None, :]).astype(np.float32)
    cos, sin = np.cos(ang).astype(np.float32), np.sin(ang).astype(np.float32)
    ct = np.ones((SEQ + 2 * HALO, HEAD_DIM), np.float32)
    st = np.zeros((SEQ + 2 * HALO, HEAD_DIM), np.float32)
    ct[:, :half], ct[:, half:ROT_DIM] = cos, cos
    st[:, :half], st[:, half:ROT_DIM] = -sin, sin
    return np.tile(ct, (1, LANES // HEAD_DIM)), np.tile(st, (1, LANES // HEAD_DIM))


def _band_bias():
    r = np.arange(QB)[:, None]
    c = np.arange(3 * QB)[None, :]
    band = (c >= r) & (c <= r + 2 * WINDOW)
    masks = [band, band & (c >= QB), band & (c < 2 * QB)]
    return np.stack([np.where(m, 0.0, -np.inf) for m in masks]).astype(np.float32)


def _stream_cast(jobs, slots, sem):
    chunks = [(src, dst, scale, r0, c0, min(STAGE_COLS, src.shape[1] - c0))
              for src, dst, scale in jobs
              for r0 in range(0, src.shape[0], STAGE_ROWS)
              for c0 in range(0, src.shape[1], STAGE_COLS)]
    n_slots = len(slots)

    def copy(n):
        src, _, _, r0, c0, cols = chunks[n]
        k = n % n_slots
        return pltpu.make_async_copy(src.at[pl.ds(r0, STAGE_ROWS), pl.ds(c0, cols)],
                                     slots[k].at[:, pl.ds(0, cols)], sem.at[k])

    for n in range(min(n_slots, len(chunks))):
        copy(n).start()
    for n, (_, dst, scale, r0, c0, cols) in enumerate(chunks):
        copy(n).wait()
        w = slots[n % n_slots][:, 0:cols]
        w = w * (scale if isinstance(scale, float) else scale[:, c0:c0 + cols])
        dst[r0:r0 + STAGE_ROWS, c0:c0 + cols] = w.astype(BF16)
        if n + n_slots < len(chunks):
            copy(n + n_slots).start()


def _window_softmax_pv(s, v_win, sink_col):
    m = jnp.maximum(jnp.max(s, axis=1, keepdims=True), sink_col)
    p = jnp.exp2(s - m)
    ov = _dot(p.astype(BF16), v_win)
    return ov[:, :LANES] / (ov[:, LANES:] + jnp.exp2(sink_col - m))


def _layer_kernel(sink_ref, x_ref, xn_ref, gpre_ref, win_hbm, wconv_ref, cos_ref, sin_ref, bias_ref, wscale_ref,
                  mem_ref, gmem_ref, wmkv_hbm, wua_hbm, wub_hbm, wum_hbm, wout_hbm, gpost_ref,
                  out_hbm, h_s, k_s, ksw_s, v_s, vsw_s, cu_s, mrg_s, gate_s,
                  win_ref, wmkv_ref, wua_ref, wub_ref, wum_ref, wout_ref, mkt_ref, mvx_ref,
                  stage_sem, out_s, out_sem):
    b = pl.program_id(0)
    i = pl.program_id(1)
    n_tiles = pl.num_programs(1)
    step = b * n_tiles + i
    last_step = pl.num_programs(0) * n_tiles - 1
    row0 = pl.multiple_of(i * TS, TS)
    tile_row = step * TS
    g_pre = gpre_ref[...]
    g_post = gpost_ref[...]

    def out_slot(slot):
        return out_s.at[slot // OUT_PER_BUF, pl.ds((slot % OUT_PER_BUF) * OUT_ROWS, OUT_ROWS)]

    def out_copy(slot, dst_row):
        sem = out_sem.at[1 if slot == OUT_SLOTS - 1 else 0]
        return pltpu.make_async_copy(out_slot(slot), out_hbm.at[pl.ds(dst_row, OUT_ROWS)], sem)

    pending_out = []

    def write_out(slot, value, dst_row):
        if slot == OUT_SLOTS - 1:
            out_copy(slot, 0).wait()
        out_slot(slot)[...] = value
        pending_out.append((slot, dst_row))

    def flush_out():
        for slot, dst_row in pending_out:
            out_copy(slot, dst_row).start()
        pending_out.clear()

    def wait_out_slots(slots):
        for slot in slots:
            out_copy(slot, 0).wait()

    @pl.when(step == 0)
    def _first_step():
        slots = [gate_s.at[k] for k in range(3)] + [mrg_s] + [out_s.at[k] for k in range(OUT_SLOTS // OUT_PER_BUF)]
        _stream_cast([(win_hbm, win_ref, wscale_ref), (wmkv_hbm, wmkv_ref, 1.0), (wua_hbm, wua_ref, 0.5),
                      (wub_hbm, wub_ref, 0.5), (wum_hbm, wum_ref, 0.5), (wout_hbm, wout_ref, 1.0)],
                     slots, stage_sem)
        for buf in (k_s, ksw_s):
            buf[:, TS:TS + HALO] = jnp.zeros((KV_WIDTH, HALO), BF16)
        for buf in (v_s, vsw_s):
            buf[TS:TS + HALO] = jnp.zeros((HALO, buf.shape[1]), BF16)
        cu_s[...] = jnp.zeros(cu_s.shape, F32)
        out_s[...] = jnp.zeros(out_s.shape, F32)
        for slot in range(OUT_SLOTS):
            out_copy(slot, slot * OUT_ROWS).start()

    @pl.when(i == 0)
    def _memory_kv():
        mn = _rms(mem_ref[0], gmem_ref[...]).astype(BF16)
        kv = _dot(mn, wmkv_ref[...])
        mkt_ref[...] = kv[:, :MEM_WIDTH].T.astype(BF16)
        ones_m = jnp.ones((MEM_LEN, MEM_HEAD_DIM), F32)
        for hh in range(MEM_HEADS):
            c0 = MEM_WIDTH + hh * MEM_HEAD_DIM
            mvx_ref[hh] = jnp.concatenate([kv[:, c0:c0 + MEM_HEAD_DIM], ones_m], axis=1).astype(BF16)

    lane = lax.broadcasted_iota(jnp.int32, (1, LANES), 1)
    low_head = lane < HEAD_DIM
    low8 = (lane % HEAD_DIM) < (ROT_DIM // 2)

    rows4 = lax.broadcasted_iota(jnp.int32, (4 * QB, 1), 0)
    def sink_rows(h0, h1, h2, h3):
        return LOG2E * jnp.where(rows4 < QB, sink_ref[h0],
                                 jnp.where(rows4 < 2 * QB, sink_ref[h1],
                                           jnp.where(rows4 < 3 * QB, sink_ref[h2], sink_ref[h3])))
    sink_a = sink_rows(0, 2, 5, 7)
    sink_b = sink_rows(1, 3, 4, 6)
    n_blocks = SEQ // QB
    ones = jnp.ones((SB, LANES), BF16)

    def norm_main(sb):
        r0 = HALO + sb * SB
        h_s[r0:r0 + SB] = _rms(x_ref[0, sb * SB:(sb + 1) * SB], g_pre).astype(BF16)

    def norm_next_halo():
        h_s[HALO + TS:EXT] = _rms(xn_ref[0], g_pre).astype(BF16)

    def carry_prev_halo():
        inside = i > 0
        for buf in (k_s, ksw_s):
            buf[:, 0:HALO] = jnp.where(inside, buf[:, TS:TS + HALO], jnp.zeros((KV_WIDTH, HALO), BF16))
        for buf in (v_s, vsw_s):
            buf[0:HALO] = jnp.where(inside, buf[TS:TS + HALO], jnp.zeros((HALO, buf.shape[1]), BF16))

    def kv_chunk(c):
        r0 = HALO if c == 0 else c * SB
        r1 = (c + 1) * SB
        n = r1 - r0
        kv = _dot(h_s[r0:r1], win_ref[:, C_BK:C_BV + KV_WIDTH])
        cos = cos_ref[pl.ds(row0 + r0, n), :]
        ssin = sin_ref[pl.ds(row0 + r0, n), :]
        k = _rope(kv[:, :KV_WIDTH], cos, ssin, low8)
        v = kv[:, KV_WIDTH:]
        k_s[:, r0:r1] = k.T.astype(BF16)
        ksw_s[:, r0:r1] = pltpu.roll(k, HEAD_DIM, 1).T.astype(BF16)
        v_s[r0:r1, 0:LANES] = v.astype(BF16)
        v_s[r0:r1, LANES:2 * LANES] = ones[0:n]
        vsw_s[r0:r1, 0:LANES] = pltpu.roll(v, HEAD_DIM, 1).astype(BF16)
        vsw_s[r0:r1, LANES:2 * LANES] = ones[0:n]

    def sub_block(sb):
        m0 = HALO + sb * SB

        def gate(col, slot):
            gate_s[slot] = jnp.tanh(_dot(h, win_ref[:, col:col + D_MODEL]))

        def conv_input():
            have = CONV_HALO if sb > 0 else 0
            before = cu_s[SB:SB + CONV_HALO + have]
            hx = h_s[m0 + have:m0 + SB + CONV_HALO]
            cx = _dot(hx, win_ref[:, C_AC:C_AX + A_WIDTH])
            cu = cx[:, :A_WIDTH] * cx[:, A_WIDTH:]
            first = row0 + sb * SB == 0
            last = row0 + (sb + 1) * SB == SEQ
            cu_s[0:CONV_HALO + have] = before if sb > 0 else jnp.where(first, 0.0, before)
            cu_s[CONV_HALO + have:CONV_HALO + SB] = cu[0:SB - have]
            cu_s[CONV_HALO + SB:] = jnp.where(last, 0.0, cu[SB - have:])

        norm_main(sb)
        yield
        h = h_s[m0:m0 + SB]
        q = _dot(h, win_ref[:, C_BQ:C_BQ + ATTN_WIDTH])
        b_zs = _silu_half(_dot(h, win_ref[:, C_BZ:C_BZ + ATTN_WIDTH]))
        yield
        a_b = _dot(h, win_ref[:, C_AB:C_AB + A_WIDTH])
        a_zs = _silu_half(_dot(h, win_ref[:, C_AZ:C_AZ + A_WIDTH]))
        scale = (HEAD_DIM ** -0.5) * LOG2E
        qcos = cos_ref[pl.ds(row0 + m0, SB), :] * scale
        qsin = sin_ref[pl.ds(row0 + m0, SB), :] * scale
        q_lo, q_hi = [], []
        for mblk in range(ATTN_WIDTH // LANES):
            qr = _rope(q[:, mblk * LANES:(mblk + 1) * LANES], qcos, qsin, low8)
            q_lo.append(jnp.where(low_head, qr, 0.0).astype(BF16))
            q_hi.append(jnp.where(low_head, 0.0, qr).astype(BF16))
        yield
        fillers = [lambda: gate(C_G1, 1), conv_input]
        yb_blocks = []
        for j in range(SB // QB):
            qs = slice(j * QB, (j + 1) * QB)
            jb = sb * (SB // QB) + j
            blk = i * (TS // QB) + jb
            variant = jnp.where(blk == 0, 1, jnp.where(blk == n_blocks - 1, 2, 0))
            bias1 = bias_ref[variant]
            bias = jnp.concatenate([bias1, bias1, bias1, bias1], axis=0)
            lhs_a = jnp.concatenate([q_lo[0][qs], q_lo[1][qs], q_hi[2][qs], q_hi[3][qs]], axis=0)
            lhs_b = jnp.concatenate([q_hi[0][qs], q_hi[1][qs], q_lo[2][qs], q_lo[3][qs]], axis=0)
            ws = slice(jb * QB, jb * QB + 3 * QB)
            s_a = _dot(lhs_a, k_s[:, ws]) + bias
            s_b = _dot(lhs_b, ksw_s[:, ws]) + bias
            fillers[j % len(fillers)]()
            o_a = _window_softmax_pv(s_a, v_s[ws], sink_a)
            o_b = _window_softmax_pv(s_b, vsw_s[ws], sink_b)
            cols = [jnp.where(low_head, o_a[0:QB], o_b[0:QB]),
                    jnp.where(low_head, o_a[QB:2 * QB], o_b[QB:2 * QB]),
                    jnp.where(low_head, o_b[2 * QB:3 * QB], o_a[2 * QB:3 * QB]),
                    jnp.where(low_head, o_b[3 * QB:4 * QB], o_a[3 * QB:4 * QB])]
            yb_blocks.append(jnp.concatenate(cols, axis=1))
            yield
        yb = jnp.concatenate(yb_blocks, axis=0) * b_zs
        ub = _dot(yb.astype(BF16), wub_ref[...])
        mrg_s[...] = _gated(gate_s[1], ub)
        gate(C_G0, 0)
        y = (cu_s[CONV_HALO - 1:CONV_HALO - 1 + SB] * wconv_ref[0:1]
             + cu_s[CONV_HALO:CONV_HALO + SB] * wconv_ref[1:2]
             + cu_s[CONV_HALO + 1:CONV_HALO + 1 + SB] * wconv_ref[2:3])
        ya = a_b * y * a_zs
        ua = _dot(ya.astype(BF16), wua_ref[...])
        mrg_s[...] += _gated(gate_s[0], ua)
        yield
        mq = _dot(h, win_ref[:, C_MQ:C_MQ + MEM_WIDTH]) * ((MEM_HEAD_DIM ** -0.5) * LOG2E)
        m_zs = _silu_half(_dot(h, win_ref[:, C_MZ:C_MZ + MEM_WIDTH]))
        scores = []
        for hh in range(MEM_HEADS):
            hs = slice(hh * MEM_HEAD_DIM, (hh + 1) * MEM_HEAD_DIM)
            scores.append(_dot(mq[:, hs].astype(BF16), mkt_ref[hs, :]))
        gate(C_G2, 2)
        ym_heads = []
        for hh in range(MEM_HEADS):
            s = scores[hh]
            p = jnp.exp2(s - jnp.max(s, axis=1, keepdims=True))
            ov = _dot(p.astype(BF16), mvx_ref[hh])
            ym_heads.append(ov[:, :MEM_HEAD_DIM] / ov[:, MEM_HEAD_DIM:])
        ym = (jnp.concatenate(ym_heads, axis=1) * m_zs).astype(BF16)
        yield
        for half in range(SB // OUT_ROWS):
            rs = slice(half * OUT_ROWS, (half + 1) * OUT_ROWS)
            t0 = sb * SB + half * OUT_ROWS
            um = _dot(ym[rs], wum_ref[...])
            merged = mrg_s[rs] + _gated(gate_s[2, rs], um)
            o = _dot(merged.astype(BF16), wout_ref[...])
            write_out(t0 // OUT_ROWS, x_ref[0, t0:t0 + OUT_ROWS] + _rms(o, g_post), tile_row + t0)
        yield

    gens = [sub_block(sb) for sb in range(NSB)]
    def run(sb, count=1):
        for _ in range(count):
            next(gens[sb])

    def norm_for_chunk(c):
        if c < NSB:
            run(c)
        else:
            norm_next_halo()

    wait_out_slots(range(OUT_SLOTS - 1))
    carry_prev_halo()
    run(0, 2)
    norm_for_chunk(1)
    kv_chunk(0)
    kv_chunk(1)
    for sb in range(NSB):
        run(sb)
        if sb > 0:
            run(sb - 1)
        run(sb, 2)
        if sb + 1 < NSB:
            norm_for_chunk(sb + 2)
        run(sb)
        flush_out()
        run(sb)
        if sb + 1 < NSB:
            run(sb + 1)
            kv_chunk(sb + 2)
    run(NSB - 1)
    flush_out()

    @pl.when(step == last_step)
    def _last_step():
        wait_out_slots(range(OUT_SLOTS))


def _resident(shape):
    return pl.BlockSpec(shape, lambda b, i: (0,) * len(shape), pipeline_mode=pl.Buffered(1))


def _layer(x, mem, g_pre, w_in, w_conv, attn_sink, g_mem, w_mem_kv, w_up_a, w_up_b, w_up_m, w_out, g_post,
           cos_t, sin_t, bias_t):
    bsz, s, d = x.shape
    assert (s, d) == (SEQ, D_MODEL) and s % TS == 0 and SB % QB == 0 and SB == 2 * HALO
    assert (STAGE_ROWS, STAGE_COLS) == (SB, d)
    nb = TS // HALO
    hbm = pl.BlockSpec(memory_space=pl.ANY)
    in_specs = [
        pl.BlockSpec(memory_space=pltpu.SMEM),
        pl.BlockSpec((1, TS, d), lambda b, i: (b, i, 0)),
        pl.BlockSpec((1, HALO, d), lambda b, i: (b, jnp.minimum((i + 1) * nb, s // HALO - 1), 0)),
        _resident((1, d)),
        hbm,
        _resident((CONV_WIDTH, A_WIDTH)),
        _resident((s + 2 * HALO, LANES)),
        _resident((s + 2 * HALO, LANES)),
        _resident((3, QB, 3 * QB)),
        _resident((1, IN_WIDTH)),
        pl.BlockSpec((1, MEM_LEN, d), lambda b, i: (b, 0, 0)),
        _resident((1, d)),
        hbm, hbm, hbm, hbm, hbm,
        _resident((1, d)),
    ]
    scratch = [
        pltpu.VMEM((EXT, d), BF16),
        pltpu.VMEM((KV_WIDTH, EXT), BF16),
        pltpu.VMEM((KV_WIDTH, EXT), BF16),
        pltpu.VMEM((EXT, 2 * KV_WIDTH), BF16),
        pltpu.VMEM((EXT, 2 * KV_WIDTH), BF16),
        pltpu.VMEM((SB + 2 * CONV_HALO, A_WIDTH), F32),
        pltpu.VMEM((SB, d), F32),
        pltpu.VMEM((3, SB, d), F32),
        pltpu.VMEM((d, IN_WIDTH), BF16),
        pltpu.VMEM((d, 2 * MEM_WIDTH), BF16),
        pltpu.VMEM((A_WIDTH, d), BF16),
        pltpu.VMEM((ATTN_WIDTH, d), BF16),
        pltpu.VMEM((MEM_WIDTH, d), BF16),
        pltpu.VMEM((d, d), BF16),
        pltpu.VMEM((MEM_WIDTH, MEM_LEN), BF16),
        pltpu.VMEM((MEM_HEADS, MEM_LEN, 2 * MEM_HEAD_DIM), BF16),
        pltpu.SemaphoreType.DMA((4 + OUT_SLOTS // OUT_PER_BUF,)),
        pltpu.VMEM((OUT_SLOTS // OUT_PER_BUF, STAGE_ROWS, d), F32),
        pltpu.SemaphoreType.DMA((2,)),
    ]
    out = pl.pallas_call(
        _layer_kernel,
        grid=(bsz, s // TS),
        in_specs=in_specs,
        out_specs=hbm,
        out_shape=jax.ShapeDtypeStruct((bsz * s, d), x.dtype),
        scratch_shapes=scratch,
        compiler_params=pltpu.CompilerParams(dimension_semantics=("arbitrary", "arbitrary"),
                                             vmem_limit_bytes=VMEM_LIMIT_BYTES),
        name="hybrid_layer",
    )(attn_sink, x, x, g_pre.reshape(1, d), w_in, w_conv, cos_t, sin_t, bias_t, jnp.asarray(_in_scale()), mem,
      g_mem.reshape(1, d), w_mem_kv, w_up_a, w_up_b, w_up_m, w_out, g_post.reshape(1, d))
    return out.reshape(bsz, s, d)


def kernel(x, mem, g_pre, w_in, w_conv, attn_sink, g_mem, w_mem_kv, w_up_a, w_up_b, w_up_m, w_out, g_post):
    cos_np, sin_np = _rope_tables()
    cos_t, sin_t, bias_t = jnp.asarray(cos_np), jnp.asarray(sin_np), jnp.asarray(_band_bias())
    for l in range(g_pre.shape[0]):
        x = _layer(x, mem, g_pre[l], w_in[l], w_conv[l], attn_sink[l], g_mem[l], w_mem_kv[l], w_up_a[l], w_up_b[l],
                   w_up_m[l], w_out[l], g_post[l], cos_t, sin_t, bias_t)
    return x
```

```python
import math

import numpy as np
import jax
import jax.numpy as jnp
from jax import lax
from jax.experimental import pallas as pl
from jax.experimental.pallas import tpu as pltpu

F32 = jnp.float32
BF16 = jnp.bfloat16

D_MODEL = 1024
SEQ = 4096
MEM_LEN = 256
EPS = 1e-6
CONV_WIDTH = 3
A_WIDTH = 512
HEAD_DIM = 64
ATTN_WIDTH = 512
KV_WIDTH = 128
WINDOW = 128
ROPE_THETA = 500000.0
ROT_DIM = 16
MEM_HEADS = 4
MEM_HEAD_DIM = 128
MEM_WIDTH = 512

C_AB, C_AC, C_AX, C_AZ = 0, 512, 1024, 1536
C_BQ, C_BK, C_BV, C_BZ = 2048, 2560, 2688, 2816
C_MQ, C_MZ = 3328, 3840
C_G0, C_G1, C_G2 = 4352, 5376, 6400
IN_WIDTH = 7424

LANES = 128
BF16_ROWS = 16
V7X_VMEM_BYTES = 64 * 1024 * 1024
VMEM_LIMIT_BYTES = V7X_VMEM_BYTES - 2 * 1024 * 1024

SB = 256
NSB = 4
TS = SB * NSB
QB = WINDOW
HALO = WINDOW
CONV_HALO = BF16_ROWS
EXT = TS + 2 * HALO
STAGE_ROWS = SB
STAGE_COLS = D_MODEL
OUT_ROWS = SB
OUT_SLOTS = TS // OUT_ROWS
OUT_PER_BUF = STAGE_ROWS // OUT_ROWS

LOG2E = math.log2(math.e)


def _silu_half(hv):
    return hv * jnp.tanh(hv) + hv


def _gated(t, hu):
    return t * hu + hu


def _in_scale():
    sc = np.ones((1, IN_WIDTH), np.float32)
    for c0, width in ((C_AZ, A_WIDTH), (C_BZ, ATTN_WIDTH), (C_MZ, MEM_WIDTH), (C_G0, 3 * D_MODEL)):
        sc[:, c0:c0 + width] = 0.5
    return sc


def _rms(v, g):
    ms = jnp.mean(v * v, axis=-1, keepdims=True)
    return v * lax.rsqrt(ms + EPS) * g


def _dot(a, b):
    return jnp.dot(a, b, preferred_element_type=F32)


def _rope(t, cos, ssin, low8):
    partner = jnp.where(low8, pltpu.roll(t, LANES - ROT_DIM // 2, 1), pltpu.roll(t, ROT_DIM // 2, 1))
    return t * cos + partner * ssin


def _rope_tables():
    half = ROT_DIM // 2
    inv_freq = np.power(np.float32(ROPE_THETA), -np.arange(half, dtype=np.float32) * np.float32(2.0 / ROT_DIM))
    pos = (np.arange(SEQ + 2 * HALO) - HALO).astype(np.float32)
    ang = (pos[:, None] * inv_freq[None, :]).astype(np.float32)
    cos, sin = np.cos(ang).astype(np.float32), np.sin(ang).astype(np.float32)
    ct = np.ones((SEQ + 2 * HALO, HEAD_DIM), np.float32)
    st = np.zeros((SEQ + 2 * HALO, HEAD_DIM), np.float32)
    ct[:, :half], ct[:, half:ROT_DIM] = cos, cos
    st[:, :half], st[:, half:ROT_DIM] = -sin, sin
    return np.tile(ct, (1, LANES // HEAD_DIM)), np.tile(st, (1, LANES // HEAD_DIM))


def _band_bias():
    r = np.arange(QB)[:, None]
    c = np.arange(3 * QB)[None, :]
    band = (c >= r) & (c <= r + 2 * WINDOW)
    masks = [band, band & (c >= QB), band & (c < 2 * QB)]
    return np.stack([np.where(m, 0.0, -np.inf) for m in masks]).astype(np.float32)


def _stream_cast(jobs, slots, sem):
    chunks = [(src, dst, scale, r0, c0, min(STAGE_COLS, src.shape[1] - c0))
              for src, dst, scale in jobs
              for r0 in range(0, src.shape[0], STAGE_ROWS)
              for c0 in range(0, src.shape[1], STAGE_COLS)]
    n_slots = len(slots)

    def copy(n):
        src, _, _, r0, c0, cols = chunks[n]
        k = n % n_slots
        return pltpu.make_async_copy(src.at[pl.ds(r0, STAGE_ROWS), pl.ds(c0, cols)],
                                     slots[k].at[:, pl.ds(0, cols)], sem.at[k])

    for n in range(min(n_slots, len(chunks))):
        copy(n).start()
    for n, (_, dst, scale, r0, c0, cols) in enumerate(chunks):
        copy(n).wait()
        w = slots[n % n_slots][:, 0:cols]
        w = w * (scale if isinstance(scale, float) else scale[:, c0:c0 + cols])
        dst[r0:r0 + STAGE_ROWS, c0:c0 + cols] = w.astype(BF16)
        if n + n_slots < len(chunks):
            copy(n + n_slots).start()


def _window_softmax_pv(s, v_win, sink_col):
    m = jnp.maximum(jnp.max(s, axis=1, keepdims=True), sink_col)
    p = jnp.exp2(s - m)
    ov = _dot(p.astype(BF16), v_win)
    return ov[:, :LANES] / (ov[:, LANES:] + jnp.exp2(sink_col - m))


def _layer_kernel(sink_ref, x_ref, xn_ref, gpre_ref, win_hbm, wconv_ref, cos_ref, sin_ref, bias_ref, wscale_ref,
                  mem_ref, gmem_ref, wmkv_hbm, wua_hbm, wub_hbm, wum_hbm, wout_hbm, gpost_ref,
                  out_hbm, h_s, k_s, ksw_s, v_s, vsw_s, cu_s, mrg_s, gate_s,
                  win_ref, wmkv_ref, wua_ref, wub_ref, wum_ref, wout_ref, mkt_ref, mvx_ref,
                  stage_sem, out_s, out_sem, ym_s, xprev_s):
    b = pl.program_id(0)
    i = pl.program_id(1)
    n_tiles = pl.num_programs(1)
    step = b * n_tiles + i
    last_step = pl.num_programs(0) * n_tiles - 1
    row0 = pl.multiple_of(i * TS, TS)
    tile_row = step * TS
    g_pre = gpre_ref[...]
    g_post = gpost_ref[...]

    def out_slot(slot):
        return out_s.at[slot // OUT_PER_BUF, pl.ds((slot % OUT_PER_BUF) * OUT_ROWS, OUT_ROWS)]

    def out_copy(slot, dst_row):
        sem = out_sem.at[1 if slot == OUT_SLOTS - 1 else 0]
        return pltpu.make_async_copy(out_slot(slot), out_hbm.at[pl.ds(dst_row, OUT_ROWS)], sem)

    pending_out = []

    def write_out(slot, value, dst_row):
        if slot == OUT_SLOTS - 1:
            out_copy(slot, 0).wait()
        out_slot(slot)[...] = value
        pending_out.append((slot, dst_row))

    def flush_out():
        for slot, dst_row in pending_out:
            out_copy(slot, dst_row).start()
        pending_out.clear()

    def wait_out_slots(slots):
        for slot in slots:
            out_copy(slot, 0).wait()

    @pl.when(step == 0)
    def _first_step():
        slots = [gate_s.at[k] for k in range(3)] + [mrg_s] + [out_s.at[k] for k in range(OUT_SLOTS // OUT_PER_BUF)]
        _stream_cast([(win_hbm, win_ref, wscale_ref), (wmkv_hbm, wmkv_ref, 1.0), (wua_hbm, wua_ref, 0.5),
                      (wub_hbm, wub_ref, 0.5), (wum_hbm, wum_ref, 0.5), (wout_hbm, wout_ref, 1.0)],
                     slots, stage_sem)
        for buf in (k_s, ksw_s):
            buf[:, TS:TS + HALO] = jnp.zeros((KV_WIDTH, HALO), BF16)
        for buf in (v_s, vsw_s):
            buf[TS:TS + HALO] = jnp.zeros((HALO, buf.shape[1]), BF16)
        cu_s[...] = jnp.zeros(cu_s.shape, F32)
        ym_s[...] = jnp.zeros(ym_s.shape, BF16)
        xprev_s[...] = jnp.zeros(xprev_s.shape, F32)
        out_s[...] = jnp.zeros(out_s.shape, F32)
        for slot in range(OUT_SLOTS):
            out_copy(slot, slot * OUT_ROWS).start()

    @pl.when(i == 0)
    def _memory_kv():
        mn = _rms(mem_ref[0], gmem_ref[...]).astype(BF16)
        kv = _dot(mn, wmkv_ref[...])
        mkt_ref[...] = kv[:, :MEM_WIDTH].T.astype(BF16)
        ones_m = jnp.ones((MEM_LEN, MEM_HEAD_DIM), F32)
        for hh in range(MEM_HEADS):
            c0 = MEM_WIDTH + hh * MEM_HEAD_DIM
            mvx_ref[hh] = jnp.concatenate([kv[:, c0:c0 + MEM_HEAD_DIM], ones_m], axis=1).astype(BF16)

    lane = lax.broadcasted_iota(jnp.int32, (1, LANES), 1)
    low_head = lane < HEAD_DIM
    low8 = (lane % HEAD_DIM) < (ROT_DIM // 2)

    rows4 = lax.broadcasted_iota(jnp.int32, (4 * QB, 1), 0)
    def sink_rows(h0, h1, h2, h3):
        return LOG2E * jnp.where(rows4 < QB, sink_ref[h0],
                                 jnp.where(rows4 < 2 * QB, sink_ref[h1],
                                           jnp.where(rows4 < 3 * QB, sink_ref[h2], sink_ref[h3])))
    sink_a = sink_rows(0, 2, 5, 7)
    sink_b = sink_rows(1, 3, 4, 6)
    n_blocks = SEQ // QB
    ones = jnp.ones((SB, LANES), BF16)

    def norm_main(sb):
        r0 = HALO + sb * SB
        h_s[r0:r0 + SB] = _rms(x_ref[0, sb * SB:(sb + 1) * SB], g_pre).astype(BF16)

    def norm_next_halo():
        h_s[HALO + TS:EXT] = _rms(xn_ref[0], g_pre).astype(BF16)

    def carry_prev_halo():
        inside = i > 0
        for buf in (k_s, ksw_s):
            buf[:, 0:HALO] = jnp.where(inside, buf[:, TS:TS + HALO], jnp.zeros((KV_WIDTH, HALO), BF16))
        for buf in (v_s, vsw_s):
            buf[0:HALO] = jnp.where(inside, buf[TS:TS + HALO], jnp.zeros((HALO, buf.shape[1]), BF16))

    def kv_chunk(c):
        r0 = HALO if c == 0 else c * SB
        r1 = (c + 1) * SB
        n = r1 - r0
        kv = _dot(h_s[r0:r1], win_ref[:, C_BK:C_BV + KV_WIDTH])
        cos = cos_ref[pl.ds(row0 + r0, n), :]
        ssin = sin_ref[pl.ds(row0 + r0, n), :]
        k = _rope(kv[:, :KV_WIDTH], cos, ssin, low8)
        v = kv[:, KV_WIDTH:]
        k_s[:, r0:r1] = k.T.astype(BF16)
        ksw_s[:, r0:r1] = pltpu.roll(k, HEAD_DIM, 1).T.astype(BF16)
        v_s[r0:r1, 0:LANES] = v.astype(BF16)
        v_s[r0:r1, LANES:2 * LANES] = ones[0:n]
        vsw_s[r0:r1, 0:LANES] = pltpu.roll(v, HEAD_DIM, 1).astype(BF16)
        vsw_s[r0:r1, LANES:2 * LANES] = ones[0:n]

    def sub_block(sb):
        m0 = HALO + sb * SB

        def gate(col, slot):
            gate_s[slot] = jnp.tanh(_dot(h, win_ref[:, col:col + D_MODEL]))

        def conv_input():
            have = CONV_HALO if sb > 0 else 0
            before = cu_s[SB:SB + CONV_HALO + have]
            hx = h_s[m0 + have:m0 + SB + CONV_HALO]
            cx = _dot(hx, win_ref[:, C_AC:C_AX + A_WIDTH])
            cu = cx[:, :A_WIDTH] * cx[:, A_WIDTH:]
            first = row0 + sb * SB == 0
            last = row0 + (sb + 1) * SB == SEQ
            cu_s[0:CONV_HALO + have] = before if sb > 0 else jnp.where(first, 0.0, before)
            cu_s[CONV_HALO + have:CONV_HALO + SB] = cu[0:SB - have]
            cu_s[CONV_HALO + SB:] = jnp.where(last, 0.0, cu[SB - have:])

        norm_main(sb)
        yield
        h = h_s[m0:m0 + SB]
        q = _dot(h, win_ref[:, C_BQ:C_BQ + ATTN_WIDTH])
        b_zs = _silu_half(_dot(h, win_ref[:, C_BZ:C_BZ + ATTN_WIDTH]))
        yield
        a_b = _dot(h, win_ref[:, C_AB:C_AB + A_WIDTH])
        a_zs = _silu_half(_dot(h, win_ref[:, C_AZ:C_AZ + A_WIDTH]))
        scale = (HEAD_DIM ** -0.5) * LOG2E
        qcos = cos_ref[pl.ds(row0 + m0, SB), :] * scale
        qsin = sin_ref[pl.ds(row0 + m0, SB), :] * scale
        q_lo, q_hi = [], []
        for mblk in range(ATTN_WIDTH // LANES):
            qr = _rope(q[:, mblk * LANES:(mblk + 1) * LANES], qcos, qsin, low8)
            q_lo.append(jnp.where(low_head, qr, 0.0).astype(BF16))
            q_hi.append(jnp.where(low_head, 0.0, qr).astype(BF16))
        yield
        fillers = [lambda: gate(C_G1, 1), conv_input]
        yb_blocks = []
        for j in range(SB // QB):
            qs = slice(j * QB, (j + 1) * QB)
            jb = sb * (SB // QB) + j
            blk = i * (TS // QB) + jb
            variant = jnp.where(blk == 0, 1, jnp.where(blk == n_blocks - 1, 2, 0))
            bias1 = bias_ref[variant]
            bias = jnp.concatenate([bias1, bias1, bias1, bias1], axis=0)
            lhs_a = jnp.concatenate([q_lo[0][qs], q_lo[1][qs], q_hi[2][qs], q_hi[3][qs]], axis=0)
            lhs_b = jnp.concatenate([q_hi[0][qs], q_hi[1][qs], q_lo[2][qs], q_lo[3][qs]], axis=0)
            ws = slice(jb * QB, jb * QB + 3 * QB)
            s_a = _dot(lhs_a, k_s[:, ws]) + bias
            s_b = _dot(lhs_b, ksw_s[:, ws]) + bias
            fillers[j % len(fillers)]()
            o_a = _window_softmax_pv(s_a, v_s[ws], sink_a)
            o_b = _window_softmax_pv(s_b, vsw_s[ws], sink_b)
            cols = [jnp.where(low_head, o_a[0:QB], o_b[0:QB]),
                    jnp.where(low_head, o_a[QB:2 * QB], o_b[QB:2 * QB]),
                    jnp.where(low_head, o_b[2 * QB:3 * QB], o_a[2 * QB:3 * QB]),
                    jnp.where(low_head, o_b[3 * QB:4 * QB], o_a[3 * QB:4 * QB])]
            yb_blocks.append(jnp.concatenate(cols, axis=1))
            yield
        yb = jnp.concatenate(yb_blocks, axis=0) * b_zs
        ub = _dot(yb.astype(BF16), wub_ref[...])
        mrg_s[...] = _gated(gate_s[1], ub)
        gate(C_G0, 0)
        y = (cu_s[CONV_HALO - 1:CONV_HALO - 1 + SB] * wconv_ref[0:1]
             + cu_s[CONV_HALO:CONV_HALO + SB] * wconv_ref[1:2]
             + cu_s[CONV_HALO + 1:CONV_HALO + 1 + SB] * wconv_ref[2:3])
        ya = a_b * y * a_zs
        ua = _dot(ya.astype(BF16), wua_ref[...])
        mrg_s[...] += _gated(gate_s[0], ua)
        yield
        mq = _dot(h, win_ref[:, C_MQ:C_MQ + MEM_WIDTH]) * ((MEM_HEAD_DIM ** -0.5) * LOG2E)
        m_zs = _silu_half(_dot(h, win_ref[:, C_MZ:C_MZ + MEM_WIDTH]))
        scores = []
        for hh in range(MEM_HEADS):
            hs = slice(hh * MEM_HEAD_DIM, (hh + 1) * MEM_HEAD_DIM)
            scores.append(_dot(mq[:, hs].astype(BF16), mkt_ref[hs, :]))
        gate(C_G2, 2)
        ym_heads = []
        for hh in range(MEM_HEADS):
            s = scores[hh]
            p = jnp.exp2(s - jnp.max(s, axis=1, keepdims=True))
            ov = _dot(p.astype(BF16), mvx_ref[hh])
            ym_heads.append(ov[:, :MEM_HEAD_DIM] / ov[:, MEM_HEAD_DIM:])
        ym = (jnp.concatenate(ym_heads, axis=1) * m_zs).astype(BF16)
        yield
        if sb + 1 < NSB:
            tail(ym, x_ref[0, sb * SB:(sb + 1) * SB], sb, tile_row + sb * SB)
        else:
            ym_s[...] = ym
            xprev_s[...] = x_ref[0, sb * SB:(sb + 1) * SB]
        yield

    def tail(ym, x_rows, slot, dst_row):
        um = _dot(ym, wum_ref[...])
        merged = mrg_s[...] + _gated(gate_s[2], um)
        o = _dot(merged.astype(BF16), wout_ref[...])
        write_out(slot, x_rows + _rms(o, g_post), dst_row)

    gens = [sub_block(sb) for sb in range(NSB)]
    def run(sb, count=1):
        for _ in range(count):
            next(gens[sb])

    def norm_for_chunk(c):
        if c < NSB:
            run(c)
        else:
            norm_next_halo()

    wait_out_slots(range(OUT_SLOTS - 1))
    carry_prev_halo()
    prev_step = jnp.maximum(step - 1, 0)
    tail(ym_s[...], xprev_s[...], OUT_SLOTS - 1, prev_step * TS + (NSB - 1) * SB)
    run(0, 2)
    norm_for_chunk(1)
    kv_chunk(0)
    kv_chunk(1)
    for sb in range(NSB):
        run(sb)
        if sb > 0:
            run(sb - 1)
        run(sb, 2)
        if sb + 1 < NSB:
            norm_for_chunk(sb + 2)
        run(sb)
        flush_out()
        run(sb)
        if sb + 1 < NSB:
            run(sb + 1)
            kv_chunk(sb + 2)
    run(NSB - 1)
    assert not pending_out

    @pl.when(step == last_step)
    def _last_step():
        tail(ym_s[...], xprev_s[...], OUT_SLOTS - 1, tile_row + (NSB - 1) * SB)
        flush_out()
        wait_out_slots(range(OUT_SLOTS))


def _resident(shape):
    return pl.BlockSpec(shape, lambda b, i: (0,) * len(shape), pipeline_mode=pl.Buffered(1))


def _layer(x, mem, g_pre, w_in, w_conv, attn_sink, g_mem, w_mem_kv, w_up_a, w_up_b, w_up_m, w_out, g_post,
           cos_t, sin_t, bias_t):
    bsz, s, d = x.shape
    assert (s, d) == (SEQ, D_MODEL) and s % TS == 0 and SB % QB == 0 and SB == 2 * HALO
    assert (STAGE_ROWS, STAGE_COLS) == (SB, d)
    nb = TS // HALO
    hbm = pl.BlockSpec(memory_space=pl.ANY)
    in_specs = [
        pl.BlockSpec(memory_space=pltpu.SMEM),
        pl.BlockSpec((1, TS, d), lambda b, i: (b, i, 0)),
        pl.BlockSpec((1, HALO, d), lambda b, i: (b, jnp.minimum((i + 1) * nb, s // HALO - 1), 0)),
        _resident((1, d)),
        hbm,
        _resident((CONV_WIDTH, A_WIDTH)),
        _resident((s + 2 * HALO, LANES)),
        _resident((s + 2 * HALO, LANES)),
        _resident((3, QB, 3 * QB)),
        _resident((1, IN_WIDTH)),
        pl.BlockSpec((1, MEM_LEN, d), lambda b, i: (b, 0, 0)),
        _resident((1, d)),
        hbm, hbm, hbm, hbm, hbm,
        _resident((1, d)),
    ]
    scratch = [
        pltpu.VMEM((EXT, d), BF16),
        pltpu.VMEM((KV_WIDTH, EXT), BF16),
        pltpu.VMEM((KV_WIDTH, EXT), BF16),
        pltpu.VMEM((EXT, 2 * KV_WIDTH), BF16),
        pltpu.VMEM((EXT, 2 * KV_WIDTH), BF16),
        pltpu.VMEM((SB + 2 * CONV_HALO, A_WIDTH), F32),
        pltpu.VMEM((SB, d), F32),
        pltpu.VMEM((3, SB, d), F32),
        pltpu.VMEM((d, IN_WIDTH), BF16),
        pltpu.VMEM((d, 2 * MEM_WIDTH), BF16),
        pltpu.VMEM((A_WIDTH, d), BF16),
        pltpu.VMEM((ATTN_WIDTH, d), BF16),
        pltpu.VMEM((MEM_WIDTH, d), BF16),
        pltpu.VMEM((d, d), BF16),
        pltpu.VMEM((MEM_WIDTH, MEM_LEN), BF16),
        pltpu.VMEM((MEM_HEADS, MEM_LEN, 2 * MEM_HEAD_DIM), BF16),
        pltpu.SemaphoreType.DMA((4 + OUT_SLOTS // OUT_PER_BUF,)),
        pltpu.VMEM((OUT_SLOTS // OUT_PER_BUF, STAGE_ROWS, d), F32),
        pltpu.SemaphoreType.DMA((2,)),
        pltpu.VMEM((SB, MEM_WIDTH), BF16),
        pltpu.VMEM((SB, d), F32),
    ]
    out = pl.pallas_call(
        _layer_kernel,
        grid=(bsz, s // TS),
        in_specs=in_specs,
        out_specs=hbm,
        out_shape=jax.ShapeDtypeStruct((bsz * s, d), x.dtype),
        scratch_shapes=scratch,
        compiler_params=pltpu.CompilerParams(dimension_semantics=("arbitrary", "arbitrary"),
                                             vmem_limit_bytes=VMEM_LIMIT_BYTES),
        name="hybrid_layer",
    )(attn_sink, x, x, g_pre.reshape(1, d), w_in, w_conv, cos_t, sin_t, bias_t, jnp.asarray(_in_scale()), mem,
      g_mem.reshape(1, d), w_mem_kv, w_up_a, w_up_b, w_up_m, w_out, g_post.reshape(1, d))
    return out.reshape(bsz, s, d)


def kernel(x, mem, g_pre, w_in, w_conv, attn_sink, g_mem, w_mem_kv, w_up_a, w_up_b, w_up_m, w_out, g_post):
    cos_np, sin_np = _rope_tables()
    cos_t, sin_t, bias_t = jnp.asarray(cos_np), jnp.asarray(sin_np), jnp.asarray(_band_bias())
    for l in range(g_pre.shape[0]):
        x = _layer(x, mem, g_pre[l], w_in[l], w_conv[l], attn_sink[l], g_mem[l], w_mem_kv[l], w_up_a[l], w_up_b[l],
                   w_up_m[l], w_out[l], g_post[l], cos_t, sin_t, bias_t)
    return x
```

```python
import math

import numpy as np
import jax
import jax.numpy as jnp
from jax import lax
from jax.experimental import pallas as pl
from jax.experimental.pallas import tpu as pltpu

F32 = jnp.float32
BF16 = jnp.bfloat16

D_MODEL = 1024
SEQ = 4096
MEM_LEN = 256
EPS = 1e-6
CONV_WIDTH = 3
A_WIDTH = 512
HEAD_DIM = 64
ATTN_WIDTH = 512
KV_WIDTH = 128
WINDOW = 128
ROPE_THETA = 500000.0
ROT_DIM = 16
MEM_HEADS = 4
MEM_HEAD_DIM = 128
MEM_WIDTH = 512

C_AB, C_AC, C_AX, C_AZ = 0, 512, 1024, 1536
C_BQ, C_BK, C_BV, C_BZ = 2048, 2560, 2688, 2816
C_MQ, C_MZ = 3328, 3840
C_G0, C_G1, C_G2 = 4352, 5376, 6400
IN_WIDTH = 7424

LANES = 128
BF16_ROWS = 16
V7X_VMEM_BYTES = 64 * 1024 * 1024
VMEM_LIMIT_BYTES = V7X_VMEM_BYTES - 2 * 1024 * 1024

SB = 256
NSB = 4
TS = SB * NSB
QB = WINDOW
HALO = WINDOW
CONV_HALO = BF16_ROWS
EXT = TS + 2 * HALO
STAGE_ROWS = SB
STAGE_COLS = D_MODEL
OUT_ROWS = SB
OUT_SLOTS = TS // OUT_ROWS
OUT_PER_BUF = STAGE_ROWS // OUT_ROWS

LOG2E = math.log2(math.e)


def _silu_half(hv):
    return hv * jnp.tanh(hv) + hv


def _gated(t, hu):
    return t * hu + hu


def _in_scale():
    sc = np.ones((1, IN_WIDTH), np.float32)
    for c0, width in ((C_AZ, A_WIDTH), (C_BZ, ATTN_WIDTH), (C_MZ, MEM_WIDTH), (C_G0, 3 * D_MODEL)):
        sc[:, c0:c0 + width] = 0.5
    return sc


def _rms(v, g):
    ms = jnp.mean(v * v, axis=-1, keepdims=True)
    return v * lax.rsqrt(ms + EPS) * g


def _dot(a, b):
    return jnp.dot(a, b, preferred_element_type=F32)


def _rope(t, cos, ssin, low8):
    partner = jnp.where(low8, pltpu.roll(t, LANES - ROT_DIM // 2, 1), pltpu.roll(t, ROT_DIM // 2, 1))
    return t * cos + partner * ssin


def _rope_tables():
    half = ROT_DIM // 2
    inv_freq = np.power(np.float32(ROPE_THETA), -np.arange(half, dtype=np.float32) * np.float32(2.0 / ROT_DIM))
    pos = (np.arange(SEQ + 2 * HALO) - HALO).astype(np.float32)
    ang = (pos[:, None] * inv_freq[None, :]).astype(np.float32)
    cos, sin = np.cos(ang).astype(np.float32), np.sin(ang).astype(np.float32)
    ct = np.ones((SEQ + 2 * HALO, HEAD_DIM), np.float32)
    st = np.zeros((SEQ + 2 * HALO, HEAD_DIM), np.float32)
    ct[:, :half], ct[:, half:ROT_DIM] = cos, cos
    st[:, :half], st[:, half:ROT_DIM] = -sin, sin
    return np.tile(ct, (1, LANES // HEAD_DIM)), np.tile(st, (1, LANES // HEAD_DIM))


def _band_bias():
    r = np.arange(QB)[:, None]
    c = np.arange(3 * QB)[None, :]
    band = (c >= r) & (c <= r + 2 * WINDOW)
    masks = [band, band & (c >= QB), band & (c < 2 * QB)]
    return np.stack([np.where(m, 0.0, -np.inf) for m in masks]).astype(np.float32)


def _stream_cast(jobs, slots, sem):
    chunks = [(src, dst, scale, r0, c0, min(STAGE_COLS, src.shape[1] - c0))
              for src, dst, scale in jobs
              for r0 in range(0, src.shape[0], STAGE_ROWS)
              for c0 in range(0, src.shape[1], STAGE_COLS)]
    n_slots = len(slots)

    def copy(n):
        src, _, _, r0, c0, cols = chunks[n]
        k = n % n_slots
        return pltpu.make_async_copy(src.at[pl.ds(r0, STAGE_ROWS), pl.ds(c0, cols)],
                                     slots[k].at[:, pl.ds(0, cols)], sem.at[k])

    for n in range(min(n_slots, len(chunks))):
        copy(n).start()
    for n, (_, dst, scale, r0, c0, cols) in enumerate(chunks):
        copy(n).wait()
        w = slots[n % n_slots][:, 0:cols]
        w = w * (scale if isinstance(scale, float) else scale[:, c0:c0 + cols])
        dst[r0:r0 + STAGE_ROWS, c0:c0 + cols] = w.astype(BF16)
        if n + n_slots < len(chunks):
            copy(n + n_slots).start()


def _window_softmax_pv(s, v_win, sink_col):
    m = jnp.maximum(jnp.max(s, axis=1, keepdims=True), sink_col)
    p = jnp.exp2(s - m)
    ov = _dot(p.astype(BF16), v_win)
    return ov[:, :LANES] / (ov[:, LANES:] + jnp.exp2(sink_col - m))


def _layer_kernel(sink_ref, x_ref, xn_ref, gpre_ref, win_hbm, wconv_ref, cos_ref, sin_ref, bias_ref, wscale_ref,
                  mem_ref, gmem_ref, wmkv_hbm, wua_hbm, wub_hbm, wum_hbm, wout_hbm, gpost_ref,
                  out_hbm, h_s, k_s, ksw_s, v_s, vsw_s, cu_s, mrg_s, gate_s,
                  win_ref, wmkv_ref, wua_ref, wub_ref, wum_ref, wout_ref, mkt_ref, mvx_ref,
                  stage_sem, out_s, out_sem, ym_s, xprev_s):
    b = pl.program_id(0)
    i = pl.program_id(1)
    n_tiles = pl.num_programs(1)
    step = b * n_tiles + i
    last_step = pl.num_programs(0) * n_tiles - 1
    row0 = pl.multiple_of(i * TS, TS)
    tile_row = step * TS
    g_pre = gpre_ref[...]
    g_post = gpost_ref[...]

    def out_slot(slot):
        return out_s.at[slot // OUT_PER_BUF, pl.ds((slot % OUT_PER_BUF) * OUT_ROWS, OUT_ROWS)]

    def out_copy(slot, dst_row):
        sem = out_sem.at[1 if slot == OUT_SLOTS - 1 else 0]
        return pltpu.make_async_copy(out_slot(slot), out_hbm.at[pl.ds(dst_row, OUT_ROWS)], sem)

    pending_out = []

    def write_out(slot, value, dst_row):
        out_slot(slot)[...] = value
        pending_out.append((slot, dst_row))

    def flush_out():
        for slot, dst_row in pending_out:
            out_copy(slot, dst_row).start()
        pending_out.clear()

    def wait_out_slots(slots):
        for slot in slots:
            out_copy(slot, 0).wait()

    @pl.when(step == 0)
    def _first_step():
        slots = [gate_s.at[k] for k in range(3)] + [mrg_s] + [out_s.at[k] for k in range(OUT_SLOTS // OUT_PER_BUF)]
        _stream_cast([(win_hbm, win_ref, wscale_ref), (wmkv_hbm, wmkv_ref, 1.0), (wua_hbm, wua_ref, 0.5),
                      (wub_hbm, wub_ref, 0.5), (wum_hbm, wum_ref, 0.5), (wout_hbm, wout_ref, 1.0)],
                     slots, stage_sem)
        for buf in (k_s, ksw_s):
            buf[:, TS:TS + HALO] = jnp.zeros((KV_WIDTH, HALO), BF16)
        for buf in (v_s, vsw_s):
            buf[TS:TS + HALO] = jnp.zeros((HALO, buf.shape[1]), BF16)
        cu_s[...] = jnp.zeros(cu_s.shape, F32)
        ym_s[...] = jnp.zeros(ym_s.shape, BF16)
        xprev_s[...] = jnp.zeros(xprev_s.shape, F32)
        out_s[...] = jnp.zeros(out_s.shape, F32)
        for slot in range(OUT_SLOTS):
            out_copy(slot, slot * OUT_ROWS).start()

    @pl.when(i == 0)
    def _memory_kv():
        mn = _rms(mem_ref[0], gmem_ref[...]).astype(BF16)
        kv = _dot(mn, wmkv_ref[...])
        mkt_ref[...] = kv[:, :MEM_WIDTH].T.astype(BF16)
        ones_m = jnp.ones((MEM_LEN, MEM_HEAD_DIM), F32)
        for hh in range(MEM_HEADS):
            c0 = MEM_WIDTH + hh * MEM_HEAD_DIM
            mvx_ref[hh] = jnp.concatenate([kv[:, c0:c0 + MEM_HEAD_DIM], ones_m], axis=1).astype(BF16)

    lane = lax.broadcasted_iota(jnp.int32, (1, LANES), 1)
    low_head = lane < HEAD_DIM
    low8 = (lane % HEAD_DIM) < (ROT_DIM // 2)

    rows4 = lax.broadcasted_iota(jnp.int32, (4 * QB, 1), 0)
    def sink_rows(h0, h1, h2, h3):
        return LOG2E * jnp.where(rows4 < QB, sink_ref[h0],
                                 jnp.where(rows4 < 2 * QB, sink_ref[h1],
                                           jnp.where(rows4 < 3 * QB, sink_ref[h2], sink_ref[h3])))
    sink_a = sink_rows(0, 2, 5, 7)
    sink_b = sink_rows(1, 3, 4, 6)
    n_blocks = SEQ // QB
    ones = jnp.ones((SB, LANES), BF16)

    def norm_main(sb):
        r0 = HALO + sb * SB
        h_s[r0:r0 + SB] = _rms(x_ref[0, sb * SB:(sb + 1) * SB], g_pre).astype(BF16)

    def norm_next_halo():
        h_s[HALO + TS:EXT] = _rms(xn_ref[0], g_pre).astype(BF16)

    def carry_prev_halo():
        inside = i > 0
        for buf in (k_s, ksw_s):
            buf[:, 0:HALO] = jnp.where(inside, buf[:, TS:TS + HALO], jnp.zeros((KV_WIDTH, HALO), BF16))
        for buf in (v_s, vsw_s):
            buf[0:HALO] = jnp.where(inside, buf[TS:TS + HALO], jnp.zeros((HALO, buf.shape[1]), BF16))

    def kv_chunk(c):
        r0 = HALO if c == 0 else c * SB
        r1 = (c + 1) * SB
        n = r1 - r0
        kv = _dot(h_s[r0:r1], win_ref[:, C_BK:C_BV + KV_WIDTH])
        cos = cos_ref[pl.ds(row0 + r0, n), :]
        ssin = sin_ref[pl.ds(row0 + r0, n), :]
        k = _rope(kv[:, :KV_WIDTH], cos, ssin, low8)
        v = kv[:, KV_WIDTH:]
        k_s[:, r0:r1] = k.T.astype(BF16)
        ksw_s[:, r0:r1] = pltpu.roll(k, HEAD_DIM, 1).T.astype(BF16)
        v_s[r0:r1, 0:LANES] = v.astype(BF16)
        v_s[r0:r1, LANES:2 * LANES] = ones[0:n]
        vsw_s[r0:r1, 0:LANES] = pltpu.roll(v, HEAD_DIM, 1).astype(BF16)
        vsw_s[r0:r1, LANES:2 * LANES] = ones[0:n]

    def sub_block(sb):
        m0 = HALO + sb * SB

        def gate(col, slot):
            gate_s[slot] = jnp.tanh(_dot(h, win_ref[:, col:col + D_MODEL]))

        def conv_input():
            have = CONV_HALO if sb > 0 else 0
            before = cu_s[SB:SB + CONV_HALO + have]
            hx = h_s[m0 + have:m0 + SB + CONV_HALO]
            cx = _dot(hx, win_ref[:, C_AC:C_AX + A_WIDTH])
            cu = cx[:, :A_WIDTH] * cx[:, A_WIDTH:]
            first = row0 + sb * SB == 0
            last = row0 + (sb + 1) * SB == SEQ
            cu_s[0:CONV_HALO + have] = before if sb > 0 else jnp.where(first, 0.0, before)
            cu_s[CONV_HALO + have:CONV_HALO + SB] = cu[0:SB - have]
            cu_s[CONV_HALO + SB:] = jnp.where(last, 0.0, cu[SB - have:])

        norm_main(sb)
        yield
        h = h_s[m0:m0 + SB]
        q = _dot(h, win_ref[:, C_BQ:C_BQ + ATTN_WIDTH])
        b_zs = _silu_half(_dot(h, win_ref[:, C_BZ:C_BZ + ATTN_WIDTH]))
        yield
        a_b = _dot(h, win_ref[:, C_AB:C_AB + A_WIDTH])
        a_zs = _silu_half(_dot(h, win_ref[:, C_AZ:C_AZ + A_WIDTH]))
        scale = (HEAD_DIM ** -0.5) * LOG2E
        qcos = cos_ref[pl.ds(row0 + m0, SB), :] * scale
        qsin = sin_ref[pl.ds(row0 + m0, SB), :] * scale
        q_lo, q_hi = [], []
        for mblk in range(ATTN_WIDTH // LANES):
            qr = _rope(q[:, mblk * LANES:(mblk + 1) * LANES], qcos, qsin, low8)
            q_lo.append(jnp.where(low_head, qr, 0.0).astype(BF16))
            q_hi.append(jnp.where(low_head, 0.0, qr).astype(BF16))
        yield
        fillers = [lambda: gate(C_G1, 1), conv_input]
        yb_blocks = []
        for j in range(SB // QB):
            qs = slice(j * QB, (j + 1) * QB)
            jb = sb * (SB // QB) + j
            blk = i * (TS // QB) + jb
            variant = jnp.where(blk == 0, 1, jnp.where(blk == n_blocks - 1, 2, 0))
            bias1 = bias_ref[variant]
            bias = jnp.concatenate([bias1, bias1, bias1, bias1], axis=0)
            lhs_a = jnp.concatenate([q_lo[0][qs], q_lo[1][qs], q_hi[2][qs], q_hi[3][qs]], axis=0)
            lhs_b = jnp.concatenate([q_hi[0][qs], q_hi[1][qs], q_lo[2][qs], q_lo[3][qs]], axis=0)
            ws = slice(jb * QB, jb * QB + 3 * QB)
            s_a = _dot(lhs_a, k_s[:, ws]) + bias
            s_b = _dot(lhs_b, ksw_s[:, ws]) + bias
            fillers[j % len(fillers)]()
            o_a = _window_softmax_pv(s_a, v_s[ws], sink_a)
            o_b = _window_softmax_pv(s_b, vsw_s[ws], sink_b)
            cols = [jnp.where(low_head, o_a[0:QB], o_b[0:QB]),
                    jnp.where(low_head, o_a[QB:2 * QB], o_b[QB:2 * QB]),
                    jnp.where(low_head, o_b[2 * QB:3 * QB], o_a[2 * QB:3 * QB]),
                    jnp.where(low_head, o_b[3 * QB:4 * QB], o_a[3 * QB:4 * QB])]
            yb_blocks.append(jnp.concatenate(cols, axis=1))
            yield
        yb = jnp.concatenate(yb_blocks, axis=0) * b_zs
        ub = _dot(yb.astype(BF16), wub_ref[...])
        mrg_s[...] = _gated(gate_s[1], ub)
        gate(C_G0, 0)
        y = (cu_s[CONV_HALO - 1:CONV_HALO - 1 + SB] * wconv_ref[0:1]
             + cu_s[CONV_HALO:CONV_HALO + SB] * wconv_ref[1:2]
             + cu_s[CONV_HALO + 1:CONV_HALO + 1 + SB] * wconv_ref[2:3])
        ya = a_b * y * a_zs
        ua = _dot(ya.astype(BF16), wua_ref[...])
        mrg_s[...] += _gated(gate_s[0], ua)
        yield
        mq = _dot(h, win_ref[:, C_MQ:C_MQ + MEM_WIDTH]) * ((MEM_HEAD_DIM ** -0.5) * LOG2E)
        m_zs = _silu_half(_dot(h, win_ref[:, C_MZ:C_MZ + MEM_WIDTH]))
        scores = []
        for hh in range(MEM_HEADS):
            hs = slice(hh * MEM_HEAD_DIM, (hh + 1) * MEM_HEAD_DIM)
            scores.append(_dot(mq[:, hs].astype(BF16), mkt_ref[hs, :]))
        gate(C_G2, 2)
        ym_heads = []
        for hh in range(MEM_HEADS):
            s = scores[hh]
            p = jnp.exp2(s - jnp.max(s, axis=1, keepdims=True))
            ov = _dot(p.astype(BF16), mvx_ref[hh])
            ym_heads.append(ov[:, :MEM_HEAD_DIM] / ov[:, MEM_HEAD_DIM:])
        ym = (jnp.concatenate(ym_heads, axis=1) * m_zs).astype(BF16)
        yield
        if sb + 1 < NSB:
            tail(ym, x_ref[0, sb * SB:(sb + 1) * SB], sb, tile_row + sb * SB)
        else:
            ym_s[...] = ym
            xprev_s[...] = x_ref[0, sb * SB:(sb + 1) * SB]
        yield

    def tail(ym, x_rows, slot, dst_row):
        um = _dot(ym, wum_ref[...])
        merged = mrg_s[...] + _gated(gate_s[2], um)
        o = _dot(merged.astype(BF16), wout_ref[...])
        write_out(slot, x_rows + _rms(o, g_post), dst_row)

    gens = [sub_block(sb) for sb in range(NSB)]
    def run(sb, count=1):
        for _ in range(count):
            next(gens[sb])

    def norm_for_chunk(c):
        if c < NSB:
            run(c)
        else:
            norm_next_halo()

    wait_out_slots(range(OUT_SLOTS))
    carry_prev_halo()
    prev_step = jnp.maximum(step - 1, 0)
    tail(ym_s[...], xprev_s[...], OUT_SLOTS - 1, prev_step * TS + (NSB - 1) * SB)
    run(0, 2)
    norm_for_chunk(1)
    kv_chunk(0)
    kv_chunk(1)
    for sb in range(NSB):
        run(sb)
        if sb > 0:
            run(sb - 1)
        run(sb, 2)
        if sb + 1 < NSB:
            norm_for_chunk(sb + 2)
        run(sb)
        flush_out()
        run(sb)
        if sb + 1 < NSB:
            run(sb + 1)
            kv_chunk(sb + 2)
    run(NSB - 1)
    assert not pending_out

    @pl.when(step == last_step)
    def _last_step():
        wait_out_slots([OUT_SLOTS - 1])
        tail(ym_s[...], xprev_s[...], OUT_SLOTS - 1, tile_row + (NSB - 1) * SB)
        flush_out()
        wait_out_slots(range(OUT_SLOTS))


def _resident(shape):
    return pl.BlockSpec(shape, lambda b, i: (0,) * len(shape), pipeline_mode=pl.Buffered(1))


def _layer(x, mem, g_pre, w_in, w_conv, attn_sink, g_mem, w_mem_kv, w_up_a, w_up_b, w_up_m, w_out, g_post,
           cos_t, sin_t, bias_t):
    bsz, s, d = x.shape
    assert (s, d) == (SEQ, D_MODEL) and s % TS == 0 and SB % QB == 0 and SB == 2 * HALO
    assert (STAGE_ROWS, STAGE_COLS) == (SB, d)
    nb = TS // HALO
    hbm = pl.BlockSpec(memory_space=pl.ANY)
    in_specs = [
        pl.BlockSpec(memory_space=pltpu.SMEM),
        pl.BlockSpec((1, TS, d), lambda b, i: (b, i, 0)),
        pl.BlockSpec((1, HALO, d), lambda b, i: (b, jnp.minimum((i + 1) * nb, s // HALO - 1), 0)),
        _resident((1, d)),
        hbm,
        _resident((CONV_WIDTH, A_WIDTH)),
        _resident((s + 2 * HALO, LANES)),
        _resident((s + 2 * HALO, LANES)),
        _resident((3, QB, 3 * QB)),
        _resident((1, IN_WIDTH)),
        pl.BlockSpec((1, MEM_LEN, d), lambda b, i: (b, 0, 0)),
        _resident((1, d)),
        hbm, hbm, hbm, hbm, hbm,
        _resident((1, d)),
    ]
    scratch = [
        pltpu.VMEM((EXT, d), BF16),
        pltpu.VMEM((KV_WIDTH, EXT), BF16),
        pltpu.VMEM((KV_WIDTH, EXT), BF16),
        pltpu.VMEM((EXT, 2 * KV_WIDTH), BF16),
        pltpu.VMEM((EXT, 2 * KV_WIDTH), BF16),
        pltpu.VMEM((SB + 2 * CONV_HALO, A_WIDTH), F32),
        pltpu.VMEM((SB, d), F32),
        pltpu.VMEM((3, SB, d), F32),
        pltpu.VMEM((d, IN_WIDTH), BF16),
        pltpu.VMEM((d, 2 * MEM_WIDTH), BF16),
        pltpu.VMEM((A_WIDTH, d), BF16),
        pltpu.VMEM((ATTN_WIDTH, d), BF16),
        pltpu.VMEM((MEM_WIDTH, d), BF16),
        pltpu.VMEM((d, d), BF16),
        pltpu.VMEM((MEM_WIDTH, MEM_LEN), BF16),
        pltpu.VMEM((MEM_HEADS, MEM_LEN, 2 * MEM_HEAD_DIM), BF16),
        pltpu.SemaphoreType.DMA((4 + OUT_SLOTS // OUT_PER_BUF,)),
        pltpu.VMEM((OUT_SLOTS // OUT_PER_BUF, STAGE_ROWS, d), F32),
        pltpu.SemaphoreType.DMA((2,)),
        pltpu.VMEM((SB, MEM_WIDTH), BF16),
        pltpu.VMEM((SB, d), F32),
    ]
    out = pl.pallas_call(
        _layer_kernel,
        grid=(bsz, s // TS),
        in_specs=in_specs,
        out_specs=hbm,
        out_shape=jax.ShapeDtypeStruct((bsz * s, d), x.dtype),
        scratch_shapes=scratch,
        compiler_params=pltpu.CompilerParams(dimension_semantics=("arbitrary", "arbitrary"),
                                             vmem_limit_bytes=VMEM_LIMIT_BYTES),
        name="hybrid_layer",
    )(attn_sink, x, x, g_pre.reshape(1, d), w_in, w_conv, cos_t, sin_t, bias_t, jnp.asarray(_in_scale()), mem,
      g_mem.reshape(1, d), w_mem_kv, w_up_a, w_up_b, w_up_m, w_out, g_post.reshape(1, d))
    return out.reshape(bsz, s, d)


def kernel(x, mem, g_pre, w_in, w_conv, attn_sink, g_mem, w_mem_kv, w_up_a, w_up_b, w_up_m, w_out, g_post):
    cos_np, sin_np = _rope_tables()
    cos_t, sin_t, bias_t = jnp.asarray(cos_np), jnp.asarray(sin_np), jnp.asarray(_band_bias())
    for l in range(g_pre.shape[0]):
        x = _layer(x, mem, g_pre[l], w_in[l], w_conv[l], attn_sink[l], g_mem[l], w_mem_kv[l], w_up_a[l], w_up_b[l],
                   w_up_m[l], w_out[l], g_post[l], cos_t, sin_t, bias_t)
    return x
```

```python
import math

import numpy as np
import jax
import jax.numpy as jnp
from jax import lax
from jax.experimental import pallas as pl
from jax.experimental.pallas import tpu as pltpu

F32 = jnp.float32
BF16 = jnp.bfloat16

D_MODEL = 1024
SEQ = 4096
MEM_LEN = 256
EPS = 1e-6
CONV_WIDTH = 3
A_WIDTH = 512
HEAD_DIM = 64
ATTN_WIDTH = 512
KV_WIDTH = 128
WINDOW = 128
ROPE_THETA = 500000.0
ROT_DIM = 16
MEM_HEADS = 4
MEM_HEAD_DIM = 128
MEM_WIDTH = 512

C_AB, C_AC, C_AX, C_AZ = 0, 512, 1024, 1536
C_BQ, C_BK, C_BV, C_BZ = 2048, 2560, 2688, 2816
C_MQ, C_MZ = 3328, 3840
C_G0, C_G1, C_G2 = 4352, 5376, 6400
IN_WIDTH = 7424

LANES = 128
BF16_ROWS = 16
V7X_VMEM_BYTES = 64 * 1024 * 1024
VMEM_LIMIT_BYTES = V7X_VMEM_BYTES - 2 * 1024 * 1024

SB = 256
NSB = 4
TS = SB * NSB
QB = WINDOW
HALO = WINDOW
CONV_HALO = BF16_ROWS
EXT = TS + 2 * HALO
STAGE_ROWS = SB
STAGE_COLS = D_MODEL
OUT_ROWS = SB
OUT_SLOTS = TS // OUT_ROWS
OUT_PER_BUF = STAGE_ROWS // OUT_ROWS

LOG2E = math.log2(math.e)


def _silu_half(hv):
    return hv * jnp.tanh(hv) + hv


def _gated(t, hu):
    return t * hu + hu


def _in_scale():
    sc = np.ones((1, IN_WIDTH), np.float32)
    for c0, width in ((C_AZ, A_WIDTH), (C_BZ, ATTN_WIDTH), (C_MZ, MEM_WIDTH), (C_G0, 3 * D_MODEL)):
        sc[:, c0:c0 + width] = 0.5
    return sc


def _rms(v, g):
    ms = jnp.mean(v * v, axis=-1, keepdims=True)
    return v * lax.rsqrt(ms + EPS) * g


def _dot(a, b):
    return jnp.dot(a, b, preferred_element_type=F32)


def _rope(t, cos, ssin, low8):
    partner = jnp.where(low8, pltpu.roll(t, LANES - ROT_DIM // 2, 1), pltpu.roll(t, ROT_DIM // 2, 1))
    return t * cos + partner * ssin


def _rope_tables():
    half = ROT_DIM // 2
    inv_freq = np.power(np.float32(ROPE_THETA), -np.arange(half, dtype=np.float32) * np.float32(2.0 / ROT_DIM))
    pos = (np.arange(SEQ + 2 * HALO) - HALO).astype(np.float32)
    ang = (pos[:, None] * inv_freq[None, :]).astype(np.float32)
    cos, sin = np.cos(ang).astype(np.float32), np.sin(ang).astype(np.float32)
    ct = np.ones((SEQ + 2 * HALO, HEAD_DIM), np.float32)
    st = np.zeros((SEQ + 2 * HALO, HEAD_DIM), np.float32)
    ct[:, :half], ct[:, half:ROT_DIM] = cos, cos
    st[:, :half], st[:, half:ROT_DIM] = -sin, sin
    return np.tile(ct, (1, LANES // HEAD_DIM)), np.tile(st, (1, LANES // HEAD_DIM))


def _band_bias():
    r = np.arange(QB)[:, None]
    c = np.arange(3 * QB)[None, :]
    band = (c >= r) & (c <= r + 2 * WINDOW)
    masks = [band, band & (c >= QB), band & (c < 2 * QB)]
    return np.stack([np.where(m, 0.0, -np.inf) for m in masks]).astype(np.float32)


def _stream_cast(jobs, slots, sem):
    chunks = [(src, dst, scale, r0, c0, min(STAGE_COLS, src.shape[1] - c0))
              for src, dst, scale in jobs
              for r0 in range(0, src.shape[0], STAGE_ROWS)
              for c0 in range(0, src.shape[1], STAGE_COLS)]
    n_slots = len(slots)

    def copy(n):
        src, _, _, r0, c0, cols = chunks[n]
        k = n % n_slots
        return pltpu.make_async_copy(src.at[pl.ds(r0, STAGE_ROWS), pl.ds(c0, cols)],
                                     slots[k].at[:, pl.ds(0, cols)], sem.at[k])

    for n in range(min(n_slots, len(chunks))):
        copy(n).start()
    for n, (_, dst, scale, r0, c0, cols) in enumerate(chunks):
        copy(n).wait()
        w = slots[n % n_slots][:, 0:cols]
        if callable(dst):
            dst(w, r0)
        else:
            w = w * (scale if isinstance(scale, float) else scale[:, c0:c0 + cols])
            dst[r0:r0 + STAGE_ROWS, c0:c0 + cols] = w.astype(BF16)
        if n + n_slots < len(chunks):
            copy(n + n_slots).start()


def _window_softmax_pv(s, v_win, sink_col):
    m = jnp.maximum(jnp.max(s, axis=1, keepdims=True), sink_col)
    p = jnp.exp2(s - m)
    ov = _dot(p.astype(BF16), v_win)
    return ov[:, :LANES] / (ov[:, LANES:] + jnp.exp2(sink_col - m))


def _layer_kernel(sink_ref, x_ref, xn_ref, gpre_ref, win_hbm, wconv_ref, cos_ref, sin_ref, bias_ref, wscale_ref,
                  mem_hbm, gmem_ref, wmkv_hbm, wua_hbm, wub_hbm, wum_hbm, wout_hbm, gpost_ref,
                  out_hbm, h_s, k_s, ksw_s, v_s, vsw_s, cu_s, mrg_s, gate_s,
                  win_ref, wmkv_ref, wua_ref, wub_ref, wum_ref, wout_ref, mkt_ref, mvx_ref,
                  stage_sem, out_s, out_sem):
    b = pl.program_id(0)
    i = pl.program_id(1)
    n_tiles = pl.num_programs(1)
    step = b * n_tiles + i
    last_step = pl.num_programs(0) * n_tiles - 1
    row0 = pl.multiple_of(i * TS, TS)
    tile_row = step * TS
    g_pre = gpre_ref[...]
    g_post = gpost_ref[...]

    def out_slot(slot):
        return out_s.at[slot // OUT_PER_BUF, pl.ds((slot % OUT_PER_BUF) * OUT_ROWS, OUT_ROWS)]

    def out_copy(slot, dst_row):
        sem = out_sem.at[1 if slot == OUT_SLOTS - 1 else 0]
        return pltpu.make_async_copy(out_slot(slot), out_hbm.at[pl.ds(dst_row, OUT_ROWS)], sem)

    pending_out = []

    def write_out(slot, value, dst_row):
        if slot == OUT_SLOTS - 1:
            out_copy(slot, 0).wait()
        out_slot(slot)[...] = value
        pending_out.append((slot, dst_row))

    def flush_out():
        for slot, dst_row in pending_out:
            out_copy(slot, dst_row).start()
        pending_out.clear()

    def wait_out_slots(slots):
        for slot in slots:
            out_copy(slot, 0).wait()

    @pl.when(step == 0)
    def _first_step():
        slots = [gate_s.at[k] for k in range(3)] + [mrg_s] + [out_s.at[k] for k in range(OUT_SLOTS // OUT_PER_BUF)]
        def memory_kv(mem_rows, r0):
            bb = r0 // MEM_LEN
            mn = _rms(mem_rows, gmem_ref[...]).astype(BF16)
            kv = _dot(mn, wmkv_ref[...])
            mkt_ref[bb] = kv[:, :MEM_WIDTH].T.astype(BF16)
            ones_m = jnp.ones((MEM_LEN, MEM_HEAD_DIM), F32)
            for hh in range(MEM_HEADS):
                c0 = MEM_WIDTH + hh * MEM_HEAD_DIM
                mvx_ref[bb, hh] = jnp.concatenate([kv[:, c0:c0 + MEM_HEAD_DIM], ones_m], axis=1).astype(BF16)

        _stream_cast([(wmkv_hbm, wmkv_ref, 1.0), (win_hbm, win_ref, wscale_ref), (mem_hbm, memory_kv, None),
                      (wua_hbm, wua_ref, 0.5), (wub_hbm, wub_ref, 0.5), (wum_hbm, wum_ref, 0.5),
                      (wout_hbm, wout_ref, 1.0)], slots, stage_sem)
        for buf in (k_s, ksw_s):
            buf[:, TS:TS + HALO] = jnp.zeros((KV_WIDTH, HALO), BF16)
        for buf in (v_s, vsw_s):
            buf[TS:TS + HALO] = jnp.zeros((HALO, buf.shape[1]), BF16)
        cu_s[...] = jnp.zeros(cu_s.shape, F32)
        out_s[...] = jnp.zeros(out_s.shape, F32)
        for slot in range(OUT_SLOTS):
            out_copy(slot, slot * OUT_ROWS).start()

    lane = lax.broadcasted_iota(jnp.int32, (1, LANES), 1)
    low_head = lane < HEAD_DIM
    low8 = (lane % HEAD_DIM) < (ROT_DIM // 2)

    rows4 = lax.broadcasted_iota(jnp.int32, (4 * QB, 1), 0)
    def sink_rows(h0, h1, h2, h3):
        return LOG2E * jnp.where(rows4 < QB, sink_ref[h0],
                                 jnp.where(rows4 < 2 * QB, sink_ref[h1],
                                           jnp.where(rows4 < 3 * QB, sink_ref[h2], sink_ref[h3])))
    sink_a = sink_rows(0, 2, 5, 7)
    sink_b = sink_rows(1, 3, 4, 6)
    n_blocks = SEQ // QB
    ones = jnp.ones((SB, LANES), BF16)

    def norm_main(sb):
        r0 = HALO + sb * SB
        h_s[r0:r0 + SB] = _rms(x_ref[0, sb * SB:(sb + 1) * SB], g_pre).astype(BF16)

    def norm_next_halo():
        h_s[HALO + TS:EXT] = _rms(xn_ref[0], g_pre).astype(BF16)

    def carry_prev_halo():
        inside = i > 0
        for buf in (k_s, ksw_s):
            buf[:, 0:HALO] = jnp.where(inside, buf[:, TS:TS + HALO], jnp.zeros((KV_WIDTH, HALO), BF16))
        for buf in (v_s, vsw_s):
            buf[0:HALO] = jnp.where(inside, buf[TS:TS + HALO], jnp.zeros((HALO, buf.shape[1]), BF16))

    def kv_chunk(c):
        r0 = HALO if c == 0 else c * SB
        r1 = (c + 1) * SB
        n = r1 - r0
        kv = _dot(h_s[r0:r1], win_ref[:, C_BK:C_BV + KV_WIDTH])
        cos = cos_ref[pl.ds(row0 + r0, n), :]
        ssin = sin_ref[pl.ds(row0 + r0, n), :]
        k = _rope(kv[:, :KV_WIDTH], cos, ssin, low8)
        v = kv[:, KV_WIDTH:]
        k_s[:, r0:r1] = k.T.astype(BF16)
        ksw_s[:, r0:r1] = pltpu.roll(k, HEAD_DIM, 1).T.astype(BF16)
        v_s[r0:r1, 0:LANES] = v.astype(BF16)
        v_s[r0:r1, LANES:2 * LANES] = ones[0:n]
        vsw_s[r0:r1, 0:LANES] = pltpu.roll(v, HEAD_DIM, 1).astype(BF16)
        vsw_s[r0:r1, LANES:2 * LANES] = ones[0:n]

    def sub_block(sb):
        m0 = HALO + sb * SB

        def gate(col, slot):
            gate_s[slot] = jnp.tanh(_dot(h, win_ref[:, col:col + D_MODEL]))

        def conv_input():
            have = CONV_HALO if sb > 0 else 0
            before = cu_s[SB:SB + CONV_HALO + have]
            hx = h_s[m0 + have:m0 + SB + CONV_HALO]
            cx = _dot(hx, win_ref[:, C_AC:C_AX + A_WIDTH])
            cu = cx[:, :A_WIDTH] * cx[:, A_WIDTH:]
            first = row0 + sb * SB == 0
            last = row0 + (sb + 1) * SB == SEQ
            cu_s[0:CONV_HALO + have] = before if sb > 0 else jnp.where(first, 0.0, before)
            cu_s[CONV_HALO + have:CONV_HALO + SB] = cu[0:SB - have]
            cu_s[CONV_HALO + SB:] = jnp.where(last, 0.0, cu[SB - have:])

        norm_main(sb)
        yield
        h = h_s[m0:m0 + SB]
        q = _dot(h, win_ref[:, C_BQ:C_BQ + ATTN_WIDTH])
        b_zs = _silu_half(_dot(h, win_ref[:, C_BZ:C_BZ + ATTN_WIDTH]))
        yield
        a_b = _dot(h, win_ref[:, C_AB:C_AB + A_WIDTH])
        a_zs = _silu_half(_dot(h, win_ref[:, C_AZ:C_AZ + A_WIDTH]))
        scale = (HEAD_DIM ** -0.5) * LOG2E
        qcos = cos_ref[pl.ds(row0 + m0, SB), :] * scale
        qsin = sin_ref[pl.ds(row0 + m0, SB), :] * scale
        q_lo, q_hi = [], []
        for mblk in range(ATTN_WIDTH // LANES):
            qr = _rope(q[:, mblk * LANES:(mblk + 1) * LANES], qcos, qsin, low8)
            q_lo.append(jnp.where(low_head, qr, 0.0).astype(BF16))
            q_hi.append(jnp.where(low_head, 0.0, qr).astype(BF16))
        yield
        fillers = [lambda: gate(C_G1, 1), conv_input]
        yb_blocks = []
        for j in range(SB // QB):
            qs = slice(j * QB, (j + 1) * QB)
            jb = sb * (SB // QB) + j
            blk = i * (TS // QB) + jb
            variant = jnp.where(blk == 0, 1, jnp.where(blk == n_blocks - 1, 2, 0))
            bias1 = bias_ref[variant]
            bias = jnp.concatenate([bias1, bias1, bias1, bias1], axis=0)
            lhs_a = jnp.concatenate([q_lo[0][qs], q_lo[1][qs], q_hi[2][qs], q_hi[3][qs]], axis=0)
            lhs_b = jnp.concatenate([q_hi[0][qs], q_hi[1][qs], q_lo[2][qs], q_lo[3][qs]], axis=0)
            ws = slice(jb * QB, jb * QB + 3 * QB)
            s_a = _dot(lhs_a, k_s[:, ws]) + bias
            s_b = _dot(lhs_b, ksw_s[:, ws]) + bias
            fillers[j % len(fillers)]()
            o_a = _window_softmax_pv(s_a, v_s[ws], sink_a)
            o_b = _window_softmax_pv(s_b, vsw_s[ws], sink_b)
            cols = [jnp.where(low_head, o_a[0:QB], o_b[0:QB]),
                    jnp.where(low_head, o_a[QB:2 * QB], o_b[QB:2 * QB]),
                    jnp.where(low_head, o_b[2 * QB:3 * QB], o_a[2 * QB:3 * QB]),
                    jnp.where(low_head, o_b[3 * QB:4 * QB], o_a[3 * QB:4 * QB])]
            yb_blocks.append(jnp.concatenate(cols, axis=1))
            yield
        yb = jnp.concatenate(yb_blocks, axis=0) * b_zs
        ub = _dot(yb.astype(BF16), wub_ref[...])
        mrg_s[...] = _gated(gate_s[1], ub)
        gate(C_G0, 0)
        y = (cu_s[CONV_HALO - 1:CONV_HALO - 1 + SB] * wconv_ref[0:1]
             + cu_s[CONV_HALO:CONV_HALO + SB] * wconv_ref[1:2]
             + cu_s[CONV_HALO + 1:CONV_HALO + 1 + SB] * wconv_ref[2:3])
        ya = a_b * y * a_zs
        ua = _dot(ya.astype(BF16), wua_ref[...])
        mrg_s[...] += _gated(gate_s[0], ua)
        yield
        mq = _dot(h, win_ref[:, C_MQ:C_MQ + MEM_WIDTH]) * ((MEM_HEAD_DIM ** -0.5) * LOG2E)
        m_zs = _silu_half(_dot(h, win_ref[:, C_MZ:C_MZ + MEM_WIDTH]))
        scores = []
        for hh in range(MEM_HEADS):
            hs = slice(hh * MEM_HEAD_DIM, (hh + 1) * MEM_HEAD_DIM)
            scores.append(_dot(mq[:, hs].astype(BF16), mkt_ref[b, hs, :]))
        gate(C_G2, 2)
        ym_heads = []
        for hh in range(MEM_HEADS):
            s = scores[hh]
            p = jnp.exp2(s - jnp.max(s, axis=1, keepdims=True))
            ov = _dot(p.astype(BF16), mvx_ref[b, hh])
            ym_heads.append(ov[:, :MEM_HEAD_DIM] / ov[:, MEM_HEAD_DIM:])
        ym = (jnp.concatenate(ym_heads, axis=1) * m_zs).astype(BF16)
        yield
        for half in range(SB // OUT_ROWS):
            rs = slice(half * OUT_ROWS, (half + 1) * OUT_ROWS)
            t0 = sb * SB + half * OUT_ROWS
            um = _dot(ym[rs], wum_ref[...])
            merged = mrg_s[rs] + _gated(gate_s[2, rs], um)
            o = _dot(merged.astype(BF16), wout_ref[...])
            write_out(t0 // OUT_ROWS, x_ref[0, t0:t0 + OUT_ROWS] + _rms(o, g_post), tile_row + t0)
        yield

    gens = [sub_block(sb) for sb in range(NSB)]
    def run(sb, count=1):
        for _ in range(count):
            next(gens[sb])

    def norm_for_chunk(c):
        if c < NSB:
            run(c)
        else:
            norm_next_halo()

    wait_out_slots(range(OUT_SLOTS - 1))
    carry_prev_halo()
    run(0, 2)
    norm_for_chunk(1)
    kv_chunk(0)
    kv_chunk(1)
    for sb in range(NSB):
        run(sb)
        if sb > 0:
            run(sb - 1)
        run(sb, 2)
        if sb + 1 < NSB:
            norm_for_chunk(sb + 2)
        run(sb)
        flush_out()
        run(sb)
        if sb + 1 < NSB:
            run(sb + 1)
            kv_chunk(sb + 2)
    run(NSB - 1)
    flush_out()

    @pl.when(step == last_step)
    def _last_step():
        wait_out_slots(range(OUT_SLOTS))


def _resident(shape):
    return pl.BlockSpec(shape, lambda b, i: (0,) * len(shape), pipeline_mode=pl.Buffered(1))


def _layer(x, mem, g_pre, w_in, w_conv, attn_sink, g_mem, w_mem_kv, w_up_a, w_up_b, w_up_m, w_out, g_post,
           cos_t, sin_t, bias_t):
    bsz, s, d = x.shape
    assert (s, d) == (SEQ, D_MODEL) and s % TS == 0 and SB % QB == 0 and SB == 2 * HALO
    assert (STAGE_ROWS, STAGE_COLS) == (SB, d)
    nb = TS // HALO
    hbm = pl.BlockSpec(memory_space=pl.ANY)
    in_specs = [
        pl.BlockSpec(memory_space=pltpu.SMEM),
        pl.BlockSpec((1, TS, d), lambda b, i: (b, i, 0)),
        pl.BlockSpec((1, HALO, d), lambda b, i: (b, jnp.minimum((i + 1) * nb, s // HALO - 1), 0)),
        _resident((1, d)),
        hbm,
        _resident((CONV_WIDTH, A_WIDTH)),
        _resident((s + 2 * HALO, LANES)),
        _resident((s + 2 * HALO, LANES)),
        _resident((3, QB, 3 * QB)),
        _resident((1, IN_WIDTH)),
        hbm,
        _resident((1, d)),
        hbm, hbm, hbm, hbm, hbm,
        _resident((1, d)),
    ]
    scratch = [
        pltpu.VMEM((EXT, d), BF16),
        pltpu.VMEM((KV_WIDTH, EXT), BF16),
        pltpu.VMEM((KV_WIDTH, EXT), BF16),
        pltpu.VMEM((EXT, 2 * KV_WIDTH), BF16),
        pltpu.VMEM((EXT, 2 * KV_WIDTH), BF16),
        pltpu.VMEM((SB + 2 * CONV_HALO, A_WIDTH), F32),
        pltpu.VMEM((SB, d), F32),
        pltpu.VMEM((3, SB, d), F32),
        pltpu.VMEM((d, IN_WIDTH), BF16),
        pltpu.VMEM((d, 2 * MEM_WIDTH), BF16),
        pltpu.VMEM((A_WIDTH, d), BF16),
        pltpu.VMEM((ATTN_WIDTH, d), BF16),
        pltpu.VMEM((MEM_WIDTH, d), BF16),
        pltpu.VMEM((d, d), BF16),
        pltpu.VMEM((bsz, MEM_WIDTH, MEM_LEN), BF16),
        pltpu.VMEM((bsz, MEM_HEADS, MEM_LEN, 2 * MEM_HEAD_DIM), BF16),
        pltpu.SemaphoreType.DMA((4 + OUT_SLOTS // OUT_PER_BUF,)),
        pltpu.VMEM((OUT_SLOTS // OUT_PER_BUF, STAGE_ROWS, d), F32),
        pltpu.SemaphoreType.DMA((2,)),
    ]
    out = pl.pallas_call(
        _layer_kernel,
        grid=(bsz, s // TS),
        in_specs=in_specs,
        out_specs=hbm,
        out_shape=jax.ShapeDtypeStruct((bsz * s, d), x.dtype),
        scratch_shapes=scratch,
        compiler_params=pltpu.CompilerParams(dimension_semantics=("arbitrary", "arbitrary"),
                                             vmem_limit_bytes=VMEM_LIMIT_BYTES),
        name="hybrid_layer",
    )(attn_sink, x, x, g_pre.reshape(1, d), w_in, w_conv, cos_t, sin_t, bias_t, jnp.asarray(_in_scale()),
      mem.reshape(bsz * MEM_LEN, d),
      g_mem.reshape(1, d), w_mem_kv, w_up_a, w_up_b, w_up_m, w_out, g_post.reshape(1, d))
    return out.reshape(bsz, s, d)


def kernel(x, mem, g_pre, w_in, w_conv, attn_sink, g_mem, w_mem_kv, w_up_a, w_up_b, w_up_m, w_out, g_post):
    cos_np, sin_np = _rope_tables()
    cos_t, sin_t, bias_t = jnp.asarray(cos_np), jnp.asarray(sin_np), jnp.asarray(_band_bias())
    for l in range(g_pre.shape[0]):
        x = _layer(x, mem, g_pre[l], w_in[l], w_conv[l], attn_sink[l], g_mem[l], w_mem_kv[l], w_up_a[l], w_up_b[l],
                   w_up_m[l], w_out[l], g_post[l], cos_t, sin_t, bias_t)
    return x
```

```python
import math

import numpy as np
import jax
import jax.numpy as jnp
from jax import lax
from jax.experimental import pallas as pl
from jax.experimental.pallas import tpu as pltpu

F32 = jnp.float32
BF16 = jnp.bfloat16

D_MODEL = 1024
SEQ = 4096
MEM_LEN = 256
EPS = 1e-6
CONV_WIDTH = 3
A_WIDTH = 512
HEAD_DIM = 64
ATTN_WIDTH = 512
KV_WIDTH = 128
WINDOW = 128
ROPE_THETA = 500000.0
ROT_DIM = 16
MEM_HEADS = 4
MEM_HEAD_DIM = 128
MEM_WIDTH = 512

C_AB, C_AC, C_AX, C_AZ = 0, 512, 1024, 1536
C_BQ, C_BK, C_BV, C_BZ = 2048, 2560, 2688, 2816
C_MQ, C_MZ = 3328, 3840
C_G0, C_G1, C_G2 = 4352, 5376, 6400
IN_WIDTH = 7424

LANES = 128
BF16_ROWS = 16
V7X_VMEM_BYTES = 64 * 1024 * 1024
VMEM_LIMIT_BYTES = V7X_VMEM_BYTES - 2 * 1024 * 1024

SB = 256
NSB = 4
TS = SB * NSB
QB = WINDOW
HALO = WINDOW
CONV_HALO = BF16_ROWS
EXT = TS + 2 * HALO
STAGE_ROWS = SB
STAGE_COLS = D_MODEL
OUT_ROWS = SB
OUT_SLOTS = TS // OUT_ROWS
OUT_PER_BUF = STAGE_ROWS // OUT_ROWS

LOG2E = math.log2(math.e)


def _silu_half(hv):
    return hv * jnp.tanh(hv) + hv


def _gated(t, hu):
    return t * hu + hu


def _in_scale():
    sc = np.ones((1, IN_WIDTH), np.float32)
    for c0, width in ((C_AZ, A_WIDTH), (C_BZ, ATTN_WIDTH), (C_MZ, MEM_WIDTH), (C_G0, 3 * D_MODEL)):
        sc[:, c0:c0 + width] = 0.5
    return sc


def _rms(v, g):
    ms = jnp.mean(v * v, axis=-1, keepdims=True)
    return v * lax.rsqrt(ms + EPS) * g


def _dot(a, b):
    return jnp.dot(a, b, preferred_element_type=F32)


def _rope(t, cos, ssin, low8):
    partner = jnp.where(low8, pltpu.roll(t, LANES - ROT_DIM // 2, 1), pltpu.roll(t, ROT_DIM // 2, 1))
    return t * cos + partner * ssin


def _rope_tables():
    half = ROT_DIM // 2
    inv_freq = np.power(np.float32(ROPE_THETA), -np.arange(half, dtype=np.float32) * np.float32(2.0 / ROT_DIM))
    pos = (np.arange(SEQ + 2 * HALO) - HALO).astype(np.float32)
    ang = (pos[:, None] * inv_freq[None, :]).astype(np.float32)
    cos, sin = np.cos(ang).astype(np.float32), np.sin(ang).astype(np.float32)
    ct = np.ones((SEQ + 2 * HALO, HEAD_DIM), np.float32)
    st = np.zeros((SEQ + 2 * HALO, HEAD_DIM), np.float32)
    ct[:, :half], ct[:, half:ROT_DIM] = cos, cos
    st[:, :half], st[:, half:ROT_DIM] = -sin, sin
    return np.tile(ct, (1, LANES // HEAD_DIM)), np.tile(st, (1, LANES // HEAD_DIM))


def _band_bias():
    r = np.arange(QB)[:, None]
    c = np.arange(3 * QB)[None, :]
    band = (c >= r) & (c <= r + 2 * WINDOW)
    masks = [band, band & (c >= QB), band & (c < 2 * QB)]
    return np.stack([np.where(m, 0.0, -np.inf) for m in masks]).astype(np.float32)


def _stream_cast(jobs, slots, sem):
    chunks = [(src, dst, scale, r0, c0, min(STAGE_COLS, src.shape[1] - c0))
              for src, dst, scale in jobs
              for r0 in range(0, src.shape[0], STAGE_ROWS)
              for c0 in range(0, src.shape[1], STAGE_COLS)]
    n_slots = len(slots)

    def copy(n):
        src, _, _, r0, c0, cols = chunks[n]
        k = n % n_slots
        return pltpu.make_async_copy(src.at[pl.ds(r0, STAGE_ROWS), pl.ds(c0, cols)],
                                     slots[k].at[:, pl.ds(0, cols)], sem.at[k])

    for n in range(min(n_slots, len(chunks))):
        copy(n).start()
    for n, (_, dst, scale, r0, c0, cols) in enumerate(chunks):
        copy(n).wait()
        w = slots[n % n_slots][:, 0:cols]
        if callable(dst):
            dst(w, r0)
        else:
            w = w * (scale if isinstance(scale, float) else scale[:, c0:c0 + cols])
            dst[r0:r0 + STAGE_ROWS, c0:c0 + cols] = w.astype(BF16)
        if n + n_slots < len(chunks):
            copy(n + n_slots).start()


def _window_softmax_pv(s, v_win, sink_col):
    m = jnp.maximum(jnp.max(s, axis=1, keepdims=True), sink_col)
    p = jnp.exp2(s - m)
    ov = _dot(p.astype(BF16), v_win)
    return ov[:, :LANES] / (ov[:, LANES:] + jnp.exp2(sink_col - m))


def _layer_kernel(sink_ref, x_ref, xn_ref, gpre_ref, win_hbm, wconv_ref, cos_hbm, sin_hbm, bias_ref, wscale_ref,
                  mem_hbm, gmem_ref, wmkv_hbm, wua_hbm, wub_hbm, wum_hbm, wout_hbm, gpost_ref,
                  out_hbm, h_s, k_s, ksw_s, v_s, vsw_s, cu_s, mrg_s, gate_s,
                  win_ref, wmkv_ref, wua_ref, wub_ref, wum_ref, wout_ref, mkt_ref, mvx_ref,
                  stage_sem, out_s, out_sem, cos_ref, sin_ref, table_sem):
    b = pl.program_id(0)
    i = pl.program_id(1)
    n_tiles = pl.num_programs(1)
    step = b * n_tiles + i
    last_step = pl.num_programs(0) * n_tiles - 1
    row0 = pl.multiple_of(i * TS, TS)
    tile_row = step * TS
    g_pre = gpre_ref[...]
    g_post = gpost_ref[...]

    def out_slot(slot):
        return out_s.at[slot // OUT_PER_BUF, pl.ds((slot % OUT_PER_BUF) * OUT_ROWS, OUT_ROWS)]

    def out_copy(slot, dst_row):
        sem = out_sem.at[1 if slot == OUT_SLOTS - 1 else 0]
        return pltpu.make_async_copy(out_slot(slot), out_hbm.at[pl.ds(dst_row, OUT_ROWS)], sem)

    pending_out = []

    def write_out(slot, value, dst_row):
        if slot == OUT_SLOTS - 1:
            out_copy(slot, 0).wait()
        out_slot(slot)[...] = value
        pending_out.append((slot, dst_row))

    def flush_out():
        for slot, dst_row in pending_out:
            out_copy(slot, dst_row).start()
        pending_out.clear()

    def wait_out_slots(slots):
        for slot in slots:
            out_copy(slot, 0).wait()

    @pl.when(step == 0)
    def _first_step():
        slots = [gate_s.at[k] for k in range(3)] + [mrg_s] + [out_s.at[k] for k in range(OUT_SLOTS // OUT_PER_BUF)]
        table_copies = [pltpu.make_async_copy(src, dst, table_sem.at[k])
                        for k, (src, dst) in enumerate(((cos_hbm, cos_ref), (sin_hbm, sin_ref)))]
        for cp in table_copies:
            cp.start()

        def memory_kv(mem_rows, r0):
            bb = r0 // MEM_LEN
            mn = _rms(mem_rows, gmem_ref[...]).astype(BF16)
            kv = _dot(mn, wmkv_ref[...])
            mkt_ref[bb] = kv[:, :MEM_WIDTH].T.astype(BF16)
            ones_m = jnp.ones((MEM_LEN, MEM_HEAD_DIM), F32)
            for hh in range(MEM_HEADS):
                c0 = MEM_WIDTH + hh * MEM_HEAD_DIM
                mvx_ref[bb, hh] = jnp.concatenate([kv[:, c0:c0 + MEM_HEAD_DIM], ones_m], axis=1).astype(BF16)

        _stream_cast([(wmkv_hbm, wmkv_ref, 1.0), (win_hbm, win_ref, wscale_ref), (mem_hbm, memory_kv, None),
                      (wua_hbm, wua_ref, 0.5), (wub_hbm, wub_ref, 0.5), (wum_hbm, wum_ref, 0.5),
                      (wout_hbm, wout_ref, 1.0)], slots, stage_sem)
        for cp in table_copies:
            cp.wait()
        for buf in (k_s, ksw_s):
            buf[:, TS:TS + HALO] = jnp.zeros((KV_WIDTH, HALO), BF16)
        for buf in (v_s, vsw_s):
            buf[TS:TS + HALO] = jnp.zeros((HALO, buf.shape[1]), BF16)
        cu_s[...] = jnp.zeros(cu_s.shape, F32)
        out_s[...] = jnp.zeros(out_s.shape, F32)
        for slot in range(OUT_SLOTS):
            out_copy(slot, slot * OUT_ROWS).start()

    lane = lax.broadcasted_iota(jnp.int32, (1, LANES), 1)
    low_head = lane < HEAD_DIM
    low8 = (lane % HEAD_DIM) < (ROT_DIM // 2)

    rows4 = lax.broadcasted_iota(jnp.int32, (4 * QB, 1), 0)
    def sink_rows(h0, h1, h2, h3):
        return LOG2E * jnp.where(rows4 < QB, sink_ref[h0],
                                 jnp.where(rows4 < 2 * QB, sink_ref[h1],
                                           jnp.where(rows4 < 3 * QB, sink_ref[h2], sink_ref[h3])))
    sink_a = sink_rows(0, 2, 5, 7)
    sink_b = sink_rows(1, 3, 4, 6)
    n_blocks = SEQ // QB
    ones = jnp.ones((SB, LANES), BF16)

    def norm_main(sb):
        r0 = HALO + sb * SB
        h_s[r0:r0 + SB] = _rms(x_ref[0, sb * SB:(sb + 1) * SB], g_pre).astype(BF16)

    def norm_next_halo():
        h_s[HALO + TS:EXT] = _rms(xn_ref[0], g_pre).astype(BF16)

    def carry_prev_halo():
        inside = i > 0
        for buf in (k_s, ksw_s):
            buf[:, 0:HALO] = jnp.where(inside, buf[:, TS:TS + HALO], jnp.zeros((KV_WIDTH, HALO), BF16))
        for buf in (v_s, vsw_s):
            buf[0:HALO] = jnp.where(inside, buf[TS:TS + HALO], jnp.zeros((HALO, buf.shape[1]), BF16))

    def kv_chunk(c):
        r0 = HALO if c == 0 else c * SB
        r1 = (c + 1) * SB
        n = r1 - r0
        kv = _dot(h_s[r0:r1], win_ref[:, C_BK:C_BV + KV_WIDTH])
        cos = cos_ref[pl.ds(row0 + r0, n), :]
        ssin = sin_ref[pl.ds(row0 + r0, n), :]
        k = _rope(kv[:, :KV_WIDTH], cos, ssin, low8)
        v = kv[:, KV_WIDTH:]
        k_s[:, r0:r1] = k.T.astype(BF16)
        ksw_s[:, r0:r1] = pltpu.roll(k, HEAD_DIM, 1).T.astype(BF16)
        v_s[r0:r1, 0:LANES] = v.astype(BF16)
        v_s[r0:r1, LANES:2 * LANES] = ones[0:n]
        vsw_s[r0:r1, 0:LANES] = pltpu.roll(v, HEAD_DIM, 1).astype(BF16)
        vsw_s[r0:r1, LANES:2 * LANES] = ones[0:n]

    def sub_block(sb):
        m0 = HALO + sb * SB

        def gate(col, slot):
            gate_s[slot] = jnp.tanh(_dot(h, win_ref[:, col:col + D_MODEL]))

        def conv_input():
            have = CONV_HALO if sb > 0 else 0
            before = cu_s[SB:SB + CONV_HALO + have]
            hx = h_s[m0 + have:m0 + SB + CONV_HALO]
            cx = _dot(hx, win_ref[:, C_AC:C_AX + A_WIDTH])
            cu = cx[:, :A_WIDTH] * cx[:, A_WIDTH:]
            first = row0 + sb * SB == 0
            last = row0 + (sb + 1) * SB == SEQ
            cu_s[0:CONV_HALO + have] = before if sb > 0 else jnp.where(first, 0.0, before)
            cu_s[CONV_HALO + have:CONV_HALO + SB] = cu[0:SB - have]
            cu_s[CONV_HALO + SB:] = jnp.where(last, 0.0, cu[SB - have:])

        norm_main(sb)
        yield
        h = h_s[m0:m0 + SB]
        q = _dot(h, win_ref[:, C_BQ:C_BQ + ATTN_WIDTH])
        b_zs = _silu_half(_dot(h, win_ref[:, C_BZ:C_BZ + ATTN_WIDTH]))
        yield
        a_b = _dot(h, win_ref[:, C_AB:C_AB + A_WIDTH])
        a_zs = _silu_half(_dot(h, win_ref[:, C_AZ:C_AZ + A_WIDTH]))
        scale = (HEAD_DIM ** -0.5) * LOG2E
        qcos = cos_ref[pl.ds(row0 + m0, SB), :] * scale
        qsin = sin_ref[pl.ds(row0 + m0, SB), :] * scale
        q_lo, q_hi = [], []
        for mblk in range(ATTN_WIDTH // LANES):
            qr = _rope(q[:, mblk * LANES:(mblk + 1) * LANES], qcos, qsin, low8)
            q_lo.append(jnp.where(low_head, qr, 0.0).astype(BF16))
            q_hi.append(jnp.where(low_head, 0.0, qr).astype(BF16))
        yield
        fillers = [lambda: gate(C_G1, 1), conv_input]
        yb_blocks = []
        for j in range(SB // QB):
            qs = slice(j * QB, (j + 1) * QB)
            jb = sb * (SB // QB) + j
            blk = i * (TS // QB) + jb
            variant = jnp.where(blk == 0, 1, jnp.where(blk == n_blocks - 1, 2, 0))
            bias1 = bias_ref[variant]
            bias = jnp.concatenate([bias1, bias1, bias1, bias1], axis=0)
            lhs_a = jnp.concatenate([q_lo[0][qs], q_lo[1][qs], q_hi[2][qs], q_hi[3][qs]], axis=0)
            lhs_b = jnp.concatenate([q_hi[0][qs], q_hi[1][qs], q_lo[2][qs], q_lo[3][qs]], axis=0)
            ws = slice(jb * QB, jb * QB + 3 * QB)
            s_a = _dot(lhs_a, k_s[:, ws]) + bias
            s_b = _dot(lhs_b, ksw_s[:, ws]) + bias
            fillers[j % len(fillers)]()
            o_a = _window_softmax_pv(s_a, v_s[ws], sink_a)
            o_b = _window_softmax_pv(s_b, vsw_s[ws], sink_b)
            cols = [jnp.where(low_head, o_a[0:QB], o_b[0:QB]),
                    jnp.where(low_head, o_a[QB:2 * QB], o_b[QB:2 * QB]),
                    jnp.where(low_head, o_b[2 * QB:3 * QB], o_a[2 * QB:3 * QB]),
                    jnp.where(low_head, o_b[3 * QB:4 * QB], o_a[3 * QB:4 * QB])]
            yb_blocks.append(jnp.concatenate(cols, axis=1))
            yield
        yb = jnp.concatenate(yb_blocks, axis=0) * b_zs
        ub = _dot(yb.astype(BF16), wub_ref[...])
        mrg_s[...] = _gated(gate_s[1], ub)
        gate(C_G0, 0)
        y = (cu_s[CONV_HALO - 1:CONV_HALO - 1 + SB] * wconv_ref[0:1]
             + cu_s[CONV_HALO:CONV_HALO + SB] * wconv_ref[1:2]
             + cu_s[CONV_HALO + 1:CONV_HALO + 1 + SB] * wconv_ref[2:3])
        ya = a_b * y * a_zs
        ua = _dot(ya.astype(BF16), wua_ref[...])
        mrg_s[...] += _gated(gate_s[0], ua)
        yield
        mq = _dot(h, win_ref[:, C_MQ:C_MQ + MEM_WIDTH]) * ((MEM_HEAD_DIM ** -0.5) * LOG2E)
        m_zs = _silu_half(_dot(h, win_ref[:, C_MZ:C_MZ + MEM_WIDTH]))
        scores = []
        for hh in range(MEM_HEADS):
            hs = slice(hh * MEM_HEAD_DIM, (hh + 1) * MEM_HEAD_DIM)
            scores.append(_dot(mq[:, hs].astype(BF16), mkt_ref[b, hs, :]))
        gate(C_G2, 2)
        ym_heads = []
        for hh in range(MEM_HEADS):
            s = scores[hh]
            p = jnp.exp2(s - jnp.max(s, axis=1, keepdims=True))
            ov = _dot(p.astype(BF16), mvx_ref[b, hh])
            ym_heads.append(ov[:, :MEM_HEAD_DIM] / ov[:, MEM_HEAD_DIM:])
        ym = (jnp.concatenate(ym_heads, axis=1) * m_zs).astype(BF16)
        yield
        for half in range(SB // OUT_ROWS):
            rs = slice(half * OUT_ROWS, (half + 1) * OUT_ROWS)
            t0 = sb * SB + half * OUT_ROWS
            um = _dot(ym[rs], wum_ref[...])
            merged = mrg_s[rs] + _gated(gate_s[2, rs], um)
            o = _dot(merged.astype(BF16), wout_ref[...])
            write_out(t0 // OUT_ROWS, x_ref[0, t0:t0 + OUT_ROWS] + _rms(o, g_post), tile_row + t0)
        yield

    gens = [sub_block(sb) for sb in range(NSB)]
    def run(sb, count=1):
        for _ in range(count):
            next(gens[sb])

    def norm_for_chunk(c):
        if c < NSB:
            run(c)
        else:
            norm_next_halo()

    wait_out_slots(range(OUT_SLOTS - 1))
    carry_prev_halo()
    run(0, 2)
    norm_for_chunk(1)
    kv_chunk(0)
    kv_chunk(1)
    for sb in range(NSB):
        run(sb)
        if sb > 0:
            run(sb - 1)
        run(sb, 2)
        if sb + 1 < NSB:
            norm_for_chunk(sb + 2)
        run(sb)
        flush_out()
        run(sb)
        if sb + 1 < NSB:
            run(sb + 1)
            kv_chunk(sb + 2)
    run(NSB - 1)
    flush_out()

    @pl.when(step == last_step)
    def _last_step():
        wait_out_slots(range(OUT_SLOTS))


def _resident(shape):
    return pl.BlockSpec(shape, lambda b, i: (0,) * len(shape), pipeline_mode=pl.Buffered(1))


def _layer(x, mem, g_pre, w_in, w_conv, attn_sink, g_mem, w_mem_kv, w_up_a, w_up_b, w_up_m, w_out, g_post,
           cos_t, sin_t, bias_t):
    bsz, s, d = x.shape
    assert (s, d) == (SEQ, D_MODEL) and s % TS == 0 and SB % QB == 0 and SB == 2 * HALO
    assert (STAGE_ROWS, STAGE_COLS) == (SB, d)
    nb = TS // HALO
    hbm = pl.BlockSpec(memory_space=pl.ANY)
    in_specs = [
        pl.BlockSpec(memory_space=pltpu.SMEM),
        pl.BlockSpec((1, TS, d), lambda b, i: (b, i, 0)),
        pl.BlockSpec((1, HALO, d), lambda b, i: (b, jnp.minimum((i + 1) * nb, s // HALO - 1), 0)),
        _resident((1, d)),
        hbm,
        _resident((CONV_WIDTH, A_WIDTH)),
        hbm, hbm,
        _resident((3, QB, 3 * QB)),
        _resident((1, IN_WIDTH)),
        hbm,
        _resident((1, d)),
        hbm, hbm, hbm, hbm, hbm,
        _resident((1, d)),
    ]
    scratch = [
        pltpu.VMEM((EXT, d), BF16),
        pltpu.VMEM((KV_WIDTH, EXT), BF16),
        pltpu.VMEM((KV_WIDTH, EXT), BF16),
        pltpu.VMEM((EXT, 2 * KV_WIDTH), BF16),
        pltpu.VMEM((EXT, 2 * KV_WIDTH), BF16),
        pltpu.VMEM((SB + 2 * CONV_HALO, A_WIDTH), F32),
        pltpu.VMEM((SB, d), F32),
        pltpu.VMEM((3, SB, d), F32),
        pltpu.VMEM((d, IN_WIDTH), BF16),
        pltpu.VMEM((d, 2 * MEM_WIDTH), BF16),
        pltpu.VMEM((A_WIDTH, d), BF16),
        pltpu.VMEM((ATTN_WIDTH, d), BF16),
        pltpu.VMEM((MEM_WIDTH, d), BF16),
        pltpu.VMEM((d, d), BF16),
        pltpu.VMEM((bsz, MEM_WIDTH, MEM_LEN), BF16),
        pltpu.VMEM((bsz, MEM_HEADS, MEM_LEN, 2 * MEM_HEAD_DIM), BF16),
        pltpu.SemaphoreType.DMA((4 + OUT_SLOTS // OUT_PER_BUF,)),
        pltpu.VMEM((OUT_SLOTS // OUT_PER_BUF, STAGE_ROWS, d), F32),
        pltpu.SemaphoreType.DMA((2,)),
        pltpu.VMEM((s + 2 * HALO, LANES), F32),
        pltpu.VMEM((s + 2 * HALO, LANES), F32),
        pltpu.SemaphoreType.DMA((2,)),
    ]
    out = pl.pallas_call(
        _layer_kernel,
        grid=(bsz, s // TS),
        in_specs=in_specs,
        out_specs=hbm,
        out_shape=jax.ShapeDtypeStruct((bsz * s, d), x.dtype),
        scratch_shapes=scratch,
        compiler_params=pltpu.CompilerParams(dimension_semantics=("arbitrary", "arbitrary"),
                                             vmem_limit_bytes=VMEM_LIMIT_BYTES),
        name="hybrid_layer",
    )(attn_sink, x, x, g_pre.reshape(1, d), w_in, w_conv, cos_t, sin_t, bias_t, jnp.asarray(_in_scale()),
      mem.reshape(bsz * MEM_LEN, d),
      g_mem.reshape(1, d), w_mem_kv, w_up_a, w_up_b, w_up_m, w_out, g_post.reshape(1, d))
    return out.reshape(bsz, s, d)


def kernel(x, mem, g_pre, w_in, w_conv, attn_sink, g_mem, w_mem_kv, w_up_a, w_up_b, w_up_m, w_out, g_post):
    cos_np, sin_np = _rope_tables()
    cos_t, sin_t, bias_t = jnp.asarray(cos_np), jnp.asarray(sin_np), jnp.asarray(_band_bias())
    for l in range(g_pre.shape[0]):
        x = _layer(x, mem, g_pre[l], w_in[l], w_conv[l], attn_sink[l], g_mem[l], w_mem_kv[l], w_up_a[l], w_up_b[l],
                   w_up_m[l], w_out[l], g_post[l], cos_t, sin_t, bias_t)
    return x
```
